```python
import math
import jax, jax.numpy as jnp
from jax import lax
import numpy as np

D_MODEL = 2048
BATCH = 4
SEQ = 2048
DEPTH = 1

ATTN_WIDTH = D_MODEL // 2
RWKV_WIDTH = D_MODEL - ATTN_WIDTH
ATTN_HEAD_DIM = 64
ATTN_HEADS = ATTN_WIDTH // (2 * ATTN_HEAD_DIM)
RWKV_HEAD_SIZE = 64
RWKV_HEADS = RWKV_WIDTH // RWKV_HEAD_SIZE
DECAY_LORA = 64
ICLR_LORA = 64
GATE_LORA = 160
D_FF = 5632
Q_BLOCK = 128
LN_EPS = 1e-5
ATTN_NORM_EPS = 1e-5
GN_EPS = 64e-5
ATTN_COLS = 3 * ATTN_WIDTH
RWKV_COLS = 3 * RWKV_WIDTH + DECAY_LORA + ICLR_LORA + GATE_LORA
IN_COLS = ATTN_COLS + RWKV_COLS

kernel_name = "hymba_diffattn_rwkv7_macaron_deepnorm"


def layer_norm(x, g, b):
    xf = x.astype(jnp.float32)
    mu = xf.mean(-1, keepdims=True)
    var = jnp.square(xf - mu).mean(-1, keepdims=True)
    return ((xf - mu) * lax.rsqrt(var + LN_EPS) * g + b).astype(x.dtype)


def swiglu(x, w_gate, w_up, w_down):
    return (jax.nn.silu(x @ w_gate) * (x @ w_up)) @ w_down


def alibi_slopes(n_heads):
    return jnp.exp2(-8.0 * (jnp.arange(n_heads, dtype=jnp.float32) + 1.0) / n_heads)


def diff_attention(q1, q2, k1, k2, v, lam):
    B, H, S, d = q1.shape
    nb = S // Q_BLOCK
    scale = d ** -0.5
    slopes = alibi_slopes(H)
    kpos = jnp.arange(S)

    def block(args):
        qa, qb, i = args
        qpos = i * Q_BLOCK + jnp.arange(Q_BLOCK)
        dist = qpos[:, None] - kpos[None, :]
        bias = jnp.where(dist[None] >= 0,
                         -slopes[:, None, None] * dist[None].astype(jnp.float32),
                         -jnp.inf)
        s1 = jnp.einsum('bhqd,bhkd->bhqk', qa, k1).astype(jnp.float32) * scale + bias
        s2 = jnp.einsum('bhqd,bhkd->bhqk', qb, k2).astype(jnp.float32) * scale + bias
        p = jax.nn.softmax(s1, axis=-1) - lam * jax.nn.softmax(s2, axis=-1)
        return jnp.einsum('bhqk,bhke->bhqe', p.astype(v.dtype), v)

    to_blocks = lambda q: q.reshape(B, H, nb, Q_BLOCK, d).transpose(2, 0, 1, 3, 4)
    o = lax.map(block, (to_blocks(q1), to_blocks(q2), jnp.arange(nb)))
    return o.transpose(1, 2, 0, 3, 4).reshape(B, H, S, 2 * d)


def diff_attn_group(pa, lq1, lk1, lq2, lk2, norm_g, lambda_init):
    B, S, _ = pa.shape
    H, d = ATTN_HEADS, ATTN_HEAD_DIM
    q = pa[..., :ATTN_WIDTH].reshape(B, S, H, 2, d).transpose(0, 2, 1, 3, 4)
    k = pa[..., ATTN_WIDTH:2 * ATTN_WIDTH].reshape(B, S, H, 2, d).transpose(0, 2, 1, 3, 4)
    v = pa[..., 2 * ATTN_WIDTH:].reshape(B, S, H, 2 * d).transpose(0, 2, 1, 3)
    lam = (jnp.exp(jnp.sum(lq1.astype(jnp.float32) * lk1.astype(jnp.float32)))
           - jnp.exp(jnp.sum(lq2.astype(jnp.float32) * lk2.astype(jnp.float32)))
           + lambda_init)
    o = diff_attention(q[..., 0, :], q[..., 1, :], k[..., 0, :], k[..., 1, :], v, lam)
    of = o.astype(jnp.float32)
    of = of * lax.rsqrt(jnp.mean(of * of, -1, keepdims=True) + ATTN_NORM_EPS) * norm_g
    of = of * (1.0 - lambda_init)
    return of.transpose(0, 2, 1, 3).reshape(B, S, ATTN_WIDTH).astype(pa.dtype)


def rwkv7_group(pr, mu, w0, w2, a0, a2, g2, k_k, k_a, r_k, gn_g, gn_b):
    B, S, _ = pr.shape
    H, N, C = RWKV_HEADS, RWKV_HEAD_SIZE, RWKV_WIDTH
    f32 = jnp.float32
    prev = jnp.pad(pr, ((0, 0), (1, 0), (0, 0)))[:, :-1]
    pr = pr + mu * (prev - pr)
    c1, c2, c3 = C, 2 * C, 3 * C
    c4, c5 = c3 + DECAY_LORA, c3 + DECAY_LORA + ICLR_LORA
    r, k, v = pr[..., :c1], pr[..., c1:c2], pr[..., c2:c3]
    wd, ad, gd = pr[..., c3:c4], pr[..., c4:c5], pr[..., c5:]
    w = -jax.nn.softplus(-(w0 + jnp.tanh(wd) @ w2)) - 0.5
    decay = jnp.exp(-jnp.exp(w.astype(f32)))
    a = jax.nn.sigmoid(a0 + ad @ a2)
    g = jax.nn.sigmoid(gd) @ g2
    kk = (k * k_k).astype(f32).reshape(B, S, H, N)
    kk = kk / jnp.maximum(jnp.linalg.norm(kk, axis=-1, keepdims=True), 1e-12)
    k = k * (1.0 + (a - 1.0) * k_a)
    heads = lambda t: t.astype(f32).reshape(B, S, H, N)
    r_h, k_h, v_h, a_h, w_h = heads(r), heads(k), heads(v), heads(a), heads(decay)

    def step(state, xs):
        r_t, w_t, k_t, v_t, kk_t, a_t = xs
        sa = jnp.einsum('bhvk,bhk->bhv', state, -kk_t)
        state = (state * w_t[:, :, None, :]
                 + sa[..., None] * (kk_t * a_t)[:, :, None, :]
                 + v_t[..., None] * k_t[:, :, None, :])
        return state, jnp.einsum('bhvk,bhk->bhv', state, r_t)

    tmaj = lambda t: t.transpose(1, 0, 2, 3)
    state0 = jnp.zeros((B, H, N, N), f32)
    _, y = lax.scan(step, state0, (tmaj(r_h), tmaj(w_h), tmaj(k_h), tmaj(v_h), tmaj(kk), tmaj(a_h)))
    y = y.transpose(1, 0, 2, 3)
    ym = y.mean(-1, keepdims=True)
    yv = jnp.square(y - ym).mean(-1, keepdims=True)
    yn = (y - ym) * lax.rsqrt(yv + GN_EPS) * gn_g.reshape(H, N) + gn_b.reshape(H, N)
    bonus = jnp.sum(r_h * k_h * r_k.astype(f32), -1, keepdims=True) * v_h
    return (yn + bonus).reshape(B, S, C).astype(pr.dtype) * g


def setup_inputs(seed: int = 0) -> dict:
    key = jax.random.key(seed)
    ks = jax.random.split(key, 40)
    f32 = jnp.float32
    L = DEPTH
    beta = (8.0 * DEPTH) ** -0.25
    nrm = lambda k, shape, s: jax.random.normal(k, shape, f32) * s
    gain = lambda k, shape: 1.0 + 0.02 * jax.random.normal(k, shape, f32)
    return {
        "x": nrm(ks[0], (BATCH, SEQ, D_MODEL), 1.0),
        "ffn1_w_gate": nrm(ks[1], (L, D_MODEL, D_FF), D_MODEL ** -0.5),
        "ffn1_w_up": nrm(ks[2], (L, D_MODEL, D_FF), D_MODEL ** -0.5),
        "ffn1_w_down": nrm(ks[3], (L, D_FF, D_MODEL), beta * D_FF ** -0.5),
        "ln1_g": gain(ks[4], (L, D_MODEL)),
        "ln1_b": nrm(ks[5], (L, D_MODEL), 0.02),
        "w_in": nrm(ks[6], (L, D_MODEL, IN_COLS), D_MODEL ** -0.5),
        "lambda_q1": nrm(ks[7], (L, ATTN_HEAD_DIM), 0.1),
        "lambda_k1": nrm(ks[8], (L, ATTN_HEAD_DIM), 0.1),
        "lambda_q2": nrm(ks[9], (L, ATTN_HEAD_DIM), 0.1),
        "lambda_k2": nrm(ks[10], (L, ATTN_HEAD_DIM), 0.1),
        "attn_norm_g": gain(ks[11], (L, 2 * ATTN_HEAD_DIM)),
        "rwkv_mu": jax.random.uniform(ks[12], (L, RWKV_COLS), f32),
        "rwkv_w0": nrm(ks[13], (L, RWKV_WIDTH), 1.0),
        "rwkv_w2": nrm(ks[14], (L, DECAY_LORA, RWKV_WIDTH), 0.3 * DECAY_LORA ** -0.5),
        "rwkv_a0": nrm(ks[15], (L, RWKV_WIDTH), 0.1),
        "rwkv_a2": nrm(ks[16], (L, ICLR_LORA, RWKV_WIDTH), 0.3 * ICLR_LORA ** -0.5),
        "rwkv_g2": nrm(ks[17], (L, GATE_LORA, RWKV_WIDTH), GATE_LORA ** -0.5),
        "rwkv_k_k": 0.85 + nrm(ks[18], (L, RWKV_WIDTH), 0.05),
        "rwkv_k_a": 1.0 + nrm(ks[19], (L, RWKV_WIDTH), 0.05),
        "rwkv_r_k": nrm(ks[20], (L, RWKV_HEADS, RWKV_HEAD_SIZE), 0.1),
        "rwkv_gn_g": gain(ks[21], (L, RWKV_WIDTH)),
        "rwkv_gn_b": nrm(ks[22], (L, RWKV_WIDTH), 0.02),
        "w_out": nrm(ks[23], (L, ATTN_WIDTH + RWKV_WIDTH, D_MODEL), beta * (ATTN_WIDTH + RWKV_WIDTH) ** -0.5),
        "ln2_g": gain(ks[24], (L, D_MODEL)),
        "ln2_b": nrm(ks[25], (L, D_MODEL), 0.02),
        "ffn2_w_gate": nrm(ks[26], (L, D_MODEL, D_FF), D_MODEL ** -0.5),
        "ffn2_w_up": nrm(ks[27], (L, D_MODEL, D_FF), D_MODEL ** -0.5),
        "ffn2_w_down": nrm(ks[28], (L, D_FF, D_MODEL), beta * D_FF ** -0.5),
        "ln3_g": gain(ks[29], (L, D_MODEL)),
        "ln3_b": nrm(ks[30], (L, D_MODEL), 0.02),
    }


def reference(x, ffn1_w_gate, ffn1_w_up, ffn1_w_down, ln1_g, ln1_b, w_in,
              lambda_q1, lambda_k1, lambda_q2, lambda_k2, attn_norm_g,
              rwkv_mu, rwkv_w0, rwkv_w2, rwkv_a0, rwkv_a2, rwkv_g2,
              rwkv_k_k, rwkv_k_a, rwkv_r_k, rwkv_gn_g, rwkv_gn_b,
              w_out, ln2_g, ln2_b, ffn2_w_gate, ffn2_w_up, ffn2_w_down, ln3_g, ln3_b):
    alpha = (2.0 * DEPTH) ** 0.25
    for l in range(DEPTH):
        lambda_init = 0.8 - 0.6 * math.exp(-0.3 * l)
        x = layer_norm(alpha * x + 0.5 * swiglu(x, ffn1_w_gate[l], ffn1_w_up[l], ffn1_w_down[l]),
                       ln1_g[l], ln1_b[l])
        p = x @ w_in[l]
        o_attn = diff_attn_group(p[..., :ATTN_COLS], lambda_q1[l], lambda_k1[l],
                                 lambda_q2[l], lambda_k2[l], attn_norm_g[l], lambda_init)
        o_rwkv = rwkv7_group(p[..., ATTN_COLS:], rwkv_mu[l], rwkv_w0[l], rwkv_w2[l],
                             rwkv_a0[l], rwkv_a2[l], rwkv_g2[l], rwkv_k_k[l], rwkv_k_a[l],
                             rwkv_r_k[l], rwkv_gn_g[l], rwkv_gn_b[l])
        mix = jnp.concatenate([o_attn, o_rwkv], axis=-1) @ w_out[l]
        x = layer_norm(alpha * x + mix, ln2_g[l], ln2_b[l])
        x = layer_norm(alpha * x + 0.5 * swiglu(x, ffn2_w_gate[l], ffn2_w_up[l], ffn2_w_down[l]),
                       ln3_g[l], ln3_b[l])
    return x
```

```python
import functools
import math

import jax
import jax.numpy as jnp
from jax import lax
from jax.experimental import pallas as pl
from jax.experimental.pallas import tpu as pltpu

F32 = jnp.float32
BF16 = jnp.bfloat16

DEPTH = 1
LN_EPS = 1e-5
ATTN_NORM_EPS = 1e-5
GN_EPS = 64e-5
HEAD = 64
LANES = 128
CHUNK = 64
VMEM_LIMIT = 56 * 1024 * 1024


def _cparams(sem):
    return pltpu.CompilerParams(dimension_semantics=sem, vmem_limit_bytes=VMEM_LIMIT)


def _layer_norm(y, g, b):
    mu = jnp.mean(y, axis=-1, keepdims=True)
    d = y - mu
    var = jnp.mean(d * d, axis=-1, keepdims=True)
    return d * lax.rsqrt(var + LN_EPS) * g + b


def _dot(a, b):
    return jnp.dot(a, b, preferred_element_type=F32)


def _dot_nt(a, b):
    return lax.dot_general(a, b, (((1,), (1,)), ((), ())), preferred_element_type=F32)


def _dot_tn(a, b):
    return lax.dot_general(a, b, (((0,), (0,)), ((), ())), preferred_element_type=F32)


def _ffn_ln_kernel(x_ref, wg_ref, wu_ref, wd_ref, g_ref, b_ref, o_ref, xb_ref, acc_ref, *, alpha):
    j = pl.program_id(1)

    @pl.when(j == 0)
    def _():
        xb_ref[...] = x_ref[...].astype(BF16)
        acc_ref[...] = jnp.zeros_like(acc_ref)

    xb = xb_ref[...]
    hg = _dot(xb, wg_ref[...])
    hu = _dot(xb, wu_ref[...])
    h = hg * jax.nn.sigmoid(hg) * hu
    acc_ref[...] += _dot(h.astype(BF16), wd_ref[...])

    @pl.when(j == pl.num_programs(1) - 1)
    def _():
        y = alpha * x_ref[...] + 0.5 * acc_ref[...]
        o_ref[...] = _layer_norm(y, g_ref[...], b_ref[...])


def ffn_ln(x, wg, wu, wd, g, b, *, alpha, tm=512, tf=512):
    T, D = x.shape
    Fd = wg.shape[1]
    tm = min(tm, T)
    tf = min(tf, Fd)
    return pl.pallas_call(
        functools.partial(_ffn_ln_kernel, alpha=alpha),
        grid=(T // tm, Fd // tf),
        in_specs=[
            pl.BlockSpec((tm, D), lambda i, j: (i, 0)),
            pl.BlockSpec((D, tf), lambda i, j: (0, j)),
            pl.BlockSpec((D, tf), lambda i, j: (0, j)),
            pl.BlockSpec((tf, D), lambda i, j: (j, 0)),
            pl.BlockSpec((1, D), lambda i, j: (0, 0)),
            pl.BlockSpec((1, D), lambda i, j: (0, 0)),
        ],
        out_specs=pl.BlockSpec((tm, D), lambda i, j: (i, 0)),
        out_shape=jax.ShapeDtypeStruct((T, D), F32),
        scratch_shapes=[pltpu.VMEM((tm, D), BF16), pltpu.VMEM((tm, D), F32)],
        compiler_params=_cparams(("parallel", "arbitrary")),
    )(x, wg, wu, wd, g, b)


def _in_proj_kernel(x_ref, w_ref, o_ref, xb_ref):
    @pl.when(pl.program_id(1) == 0)
    def _():
        xb_ref[...] = x_ref[...].astype(BF16)

    o_ref[...] = _dot(xb_ref[...], w_ref[...]).astype(o_ref.dtype)


def in_proj(x, w, out_dtype, *, tm=512, tn=512):
    T, D = x.shape
    N = w.shape[1]
    tm = min(tm, T)
    tn = min(tn, N)
    return pl.pallas_call(
        _in_proj_kernel,
        grid=(T // tm, N // tn),
        in_specs=[
            pl.BlockSpec((tm, D), lambda i, j: (i, 0)),
            pl.BlockSpec((D, tn), lambda i, j: (0, j)),
        ],
        out_specs=pl.BlockSpec((tm, tn), lambda i, j: (i, j)),
        out_shape=jax.ShapeDtypeStruct((T, N), out_dtype),
        scratch_shapes=[pltpu.VMEM((tm, D), BF16)],
        compiler_params=_cparams(("parallel", "arbitrary")),
    )(x, w)


def _diff_attn_kernel(slopes_ref, q_ref, k_ref, v_ref, lq1_ref, lk1_ref, lq2_ref, lk2_ref, ng_ref,
                      o_ref, qs_ref, *, tq, lambda_init):
    h = pl.program_id(1)
    i = pl.program_id(2)
    slope = slopes_ref[h]

    lane = lax.broadcasted_iota(jnp.int32, (tq, LANES), 1)
    q = q_ref[...] * 0.125
    zero = jnp.zeros_like(q)
    qs_ref[0:tq, :] = jnp.where(lane < HEAD, q, zero)
    qs_ref[tq:2 * tq, :] = jnp.where(lane >= HEAD, q, zero)
    qs = qs_ref[...]

    r = lax.broadcasted_iota(jnp.int32, (2 * tq, tq), 0)
    r = jnp.where(r >= tq, r - tq, r)
    c = lax.broadcasted_iota(jnp.int32, (2 * tq, tq), 1)
    rel = (r - c).astype(F32)
    rel_bias = -slope * rel

    def step(j, carry, masked):
        m, l, acc = carry
        start = pl.multiple_of(j * tq, tq)
        k = k_ref[pl.ds(start, tq), :]
        v = v_ref[pl.ds(start, tq), :]
        s = _dot_nt(qs, k)
        off = -slope * ((i - j) * tq).astype(F32)
        s = s + (rel_bias + off)
        if masked:
            s = jnp.where(rel >= 0.0, s, -jnp.inf)
        m_new = jnp.maximum(m, jnp.max(s, axis=-1, keepdims=True))
        a = jnp.exp(m - m_new)
        p = jnp.exp(s - m_new)
        l = a * l + jnp.sum(p, axis=-1, keepdims=True)
        acc = a * acc + _dot(p.astype(BF16), v)
        return m_new, l, acc

    init = (jnp.full((2 * tq, 1), -1e30, F32), jnp.zeros((2 * tq, 1), F32), jnp.zeros((2 * tq, LANES), F32))
    carry = lax.fori_loop(0, i, lambda j, cr: step(j, cr, False), init)
    m, l, acc = step(i, carry, True)

    lam = (jnp.exp(jnp.sum(lq1_ref[...] * lk1_ref[...], axis=-1, keepdims=True))
           - jnp.exp(jnp.sum(lq2_ref[...] * lk2_ref[...], axis=-1, keepdims=True)) + lambda_init)
    o = acc[0:tq] / l[0:tq] - lam * (acc[tq:2 * tq] / l[tq:2 * tq])
    o = o * lax.rsqrt(jnp.mean(o * o, axis=-1, keepdims=True) + ATTN_NORM_EPS) * ng_ref[...]
    o_ref[...] = (o * (1.0 - lambda_init)).astype(o_ref.dtype)


def diff_attn(pa, slopes, lq1, lk1, lq2, lk2, norm_g, *, B, S, H, lambda_init, tq=256):
    tq = min(tq, S)
    nq = S // tq
    small = lambda n: pl.BlockSpec((1, n), lambda b, h, i: (0, 0))
    return pl.pallas_call(
        functools.partial(_diff_attn_kernel, tq=tq, lambda_init=lambda_init),
        grid=(B, H, nq),
        in_specs=[
            pl.BlockSpec(memory_space=pltpu.SMEM),
            pl.BlockSpec((tq, LANES), lambda b, h, i: (b * nq + i, h)),
            pl.BlockSpec((S, LANES), lambda b, h, i: (b, H + h)),
            pl.BlockSpec((S, LANES), lambda b, h, i: (b, 2 * H + h)),
            small(HEAD), small(HEAD), small(HEAD), small(HEAD), small(2 * HEAD),
        ],
        out_specs=pl.BlockSpec((tq, LANES), lambda b, h, i: (b * nq + i, h)),
        out_shape=jax.ShapeDtypeStruct((B * S, H * LANES), BF16),
        scratch_shapes=[pltpu.VMEM((2 * tq, LANES), BF16)],
        compiler_params=_cparams(("parallel", "parallel", "arbitrary")),
    )(slopes, pa, pa, pa, lq1, lk1, lq2, lk2, norm_g)


def _pair_ones():
    r = lax.broadcasted_iota(jnp.int32, (LANES, LANES), 0)
    c = lax.broadcasted_iota(jnp.int32, (LANES, LANES), 1)
    return jnp.where((r < HEAD) == (c < HEAD), 1.0, 0.0).astype(BF16)


def _head_sum(x, ones):
    hi = x.astype(BF16)
    lo = (x - hi.astype(F32)).astype(BF16)
    return _dot(hi, ones) + _dot(lo, ones)


def _rwkv_prep_kernel(pr_ref, prev_ref, mu_ref, w0_ref, a0_ref, kk_ref, ka_ref, w12_ref, g2_ref,
                      r_ref, lw_ref, k2_ref, v_ref, kkn_ref, b_ref, g_ref, *, width):
    i = pl.program_id(1)
    x = pr_ref[0]
    tt = x.shape[0]
    prow = prev_ref[0, 7:8, :]
    prow = jnp.where(i == 0, jnp.zeros_like(prow), prow)
    row = lax.broadcasted_iota(jnp.int32, x.shape, 0)
    xs = jnp.where(row == 0, prow, pltpu.roll(x, 1, axis=0))
    xm = x + mu_ref[...] * (xs - x)

    c3 = 3 * width
    lora_in = xm[:, c3:c3 + LANES]
    lane = lax.broadcasted_iota(jnp.int32, lora_in.shape, 1)
    lora_in = jnp.where(lane < HEAD, jnp.tanh(lora_in), lora_in)
    z = _dot(lora_in.astype(BF16), w12_ref[...])
    gate_in = jax.nn.sigmoid(xm[:, c3 + LANES:c3 + 3 * LANES])
    g_ref[0] = _dot(gate_in.astype(BF16), g2_ref[...])

    lw_ref[0] = -math.exp(-0.5) * jax.nn.sigmoid(w0_ref[...] + z[:, 0:width])
    a = jax.nn.sigmoid(a0_ref[...] + z[:, width:2 * width])

    r_ref[0] = xm[:, 0:width]
    v_ref[0] = xm[:, 2 * width:3 * width]
    k = xm[:, width:2 * width]
    k2_ref[0] = k * (1.0 + (a - 1.0) * ka_ref[...])

    kx = k * kk_ref[...]
    ones = _pair_ones()
    ss = jnp.concatenate(
        [_head_sum(kx[:, t * LANES:(t + 1) * LANES] * kx[:, t * LANES:(t + 1) * LANES], ones)
         for t in range(width // LANES)], axis=1)
    kkn = kx / jnp.maximum(jnp.sqrt(ss), 1e-12)
    kkn_ref[0] = kkn
    b_ref[0] = kkn * a


def rwkv_prep(pr, mu, w0, a0, k_k, k_a, w12, g2p, *, width, tt=256):
    B, S, W = pr.shape
    tt = min(tt, S)
    row = lambda n: pl.BlockSpec((1, n), lambda b, i: (0, 0))
    out = pl.BlockSpec((1, tt, width), lambda b, i: (b, i, 0))
    return pl.pallas_call(
        functools.partial(_rwkv_prep_kernel, width=width),
        grid=(B, S // tt),
        in_specs=[
            pl.BlockSpec((1, tt, W), lambda b, i: (b, i, 0)),
            pl.BlockSpec((1, 8, W), lambda b, i: (b, jnp.maximum(i * (tt // 8) - 1, 0), 0)),
            row(W), row(width), row(width), row(width), row(width),
            pl.BlockSpec((LANES, 2 * width), lambda b, i: (0, 0)),
            pl.BlockSpec((2 * LANES, width), lambda b, i: (0, 0)),
        ],
        out_specs=[out] * 7,
        out_shape=[jax.ShapeDtypeStruct((B, S, width), F32)] * 7,
        compiler_params=_cparams(("parallel", "arbitrary")),
    )(pr, pr, mu, w0, a0, k_k, k_a, w12, g2p)


def _block_diag(x):
    lane = lax.broadcasted_iota(jnp.int32, x.shape, 1)
    zero = jnp.zeros_like(x)
    return jnp.concatenate([jnp.where(lane < HEAD, x, zero), jnp.where(lane >= HEAD, x, zero)], axis=0)


def _pair_chunk(r, lw, k2, v, kk, b, st):
    C = CHUNK
    row = lax.broadcasted_iota(jnp.int32, (C, LANES), 0)
    lane = lax.broadcasted_iota(jnp.int32, (C, LANES), 1)
    tcol = jnp.where(lane >= HEAD, lane - HEAD, lane)

    cum = lw
    sh = 1
    while sh < C:
        cum = cum + jnp.where(row >= sh, pltpu.roll(cum, sh, axis=0), 0.0)
        sh *= 2
    tot = cum[C - 1:C, :]
    e_neg = jnp.exp(-cum)
    e_rem = jnp.exp(tot - cum)
    A = -kk * jnp.exp(cum - lw)
    R = r * jnp.exp(cum)
    AR = jnp.concatenate([A, R], axis=0).astype(BF16)
    Mb = _dot_nt(AR, _block_diag(b * e_neg).astype(BF16))
    Mk = _dot_nt(AR, _block_diag(k2 * e_neg).astype(BF16))
    strict = row > tcol
    incl = row >= tcol
    Mak = jnp.where(strict, Mk[0:C], 0.0)
    Mrb = jnp.where(incl, Mb[C:2 * C], 0.0)
    Mrk = jnp.where(incl, Mk[C:2 * C], 0.0)

    L = jnp.where(strict, Mb[0:C], 0.0)
    P = jnp.where(row == tcol, 1.0, 0.0) + L
    Lp = _dot(L.astype(BF16), _block_diag(L).astype(BF16))
    n = 2
    while 2 * n < C:
        both = _dot(jnp.concatenate([Lp, P], axis=0).astype(BF16), _block_diag(Lp).astype(BF16))
        P = P + both[C:2 * C]
        Lp = both[0:C]
        n *= 2
    P = P + _dot(P.astype(BF16), _block_diag(Lp).astype(BF16))

    vbd = _block_diag(v).astype(BF16)
    ARS = _dot_nt(AR, st.astype(BF16))
    X = ARS[0:C] + _dot(Mak.astype(BF16), vbd)
    U = _dot(P.astype(BF16), _block_diag(X).astype(BF16))
    Y = ARS[C:2 * C] + _dot(jnp.concatenate([Mrb, Mrk], axis=1).astype(BF16),
                            jnp.concatenate([_block_diag(U).astype(BF16), vbd], axis=0))
    UV = jnp.concatenate([U, v], axis=0).astype(BF16)
    BK = jnp.concatenate([b * e_rem, k2 * e_rem], axis=0).astype(BF16)
    upd = _dot_tn(UV, BK)
    r2 = lax.broadcasted_iota(jnp.int32, (LANES, LANES), 0)
    c2 = lax.broadcasted_iota(jnp.int32, (LANES, LANES), 1)
    st_new = jnp.where((r2 < HEAD) == (c2 < HEAD), st * jnp.exp(tot) + upd, 0.0)
    return Y, st_new


def _rwkv_chunk_kernel(r_ref, lw_ref, k2_ref, v_ref, kk_ref, b_ref, g_ref, rk_ref, gng_ref, gnb_ref,
                       o_ref, st_ref, *, npairs):
    c = pl.program_id(2)

    @pl.when(c == 0)
    def _():
        st_ref[...] = jnp.zeros_like(st_ref)

    ones = _pair_ones()
    for p in range(npairs):
        sl = slice(p * LANES, (p + 1) * LANES)
        r, lw, k2, v = r_ref[0, :, sl], lw_ref[0, :, sl], k2_ref[0, :, sl], v_ref[0, :, sl]
        y, st_new = _pair_chunk(r, lw, k2, v, kk_ref[0, :, sl], b_ref[0, :, sl], st_ref[p])
        st_ref[p] = st_new
        d = y - _head_sum(y, ones) * (1.0 / HEAD)
        var = _head_sum(d * d, ones) * (1.0 / HEAD)
        yn = d * lax.rsqrt(var + GN_EPS) * gng_ref[:, sl] + gnb_ref[:, sl]
        bonus = _head_sum(r * k2 * rk_ref[:, sl], ones) * v
        o_ref[0, :, sl] = ((yn + bonus) * g_ref[0, :, sl]).astype(o_ref.dtype)


def rwkv_chunk(r, lw, k2, v, kk, b, g, r_k, gn_g, gn_b, *, npairs=4):
    B, S, W = r.shape
    npairs = min(npairs, W // LANES)
    wb = npairs * LANES
    seq = pl.BlockSpec((1, CHUNK, wb), lambda bi, p, c: (bi, c, p))
    par = pl.BlockSpec((1, wb), lambda bi, p, c: (0, p))
    return pl.pallas_call(
        functools.partial(_rwkv_chunk_kernel, npairs=npairs),
        grid=(B, W // wb, S // CHUNK),
        in_specs=[seq] * 7 + [par] * 3,
        out_specs=seq,
        out_shape=jax.ShapeDtypeStruct((B, S, W), BF16),
        scratch_shapes=[pltpu.VMEM((npairs, LANES, LANES), F32)],
        compiler_params=_cparams(("parallel", "parallel", "arbitrary")),
    )(r, lw, k2, v, kk, b, g, r_k, gn_g, gn_b)


def _out_ln_kernel(x_ref, oa_ref, or_ref, wa_ref, wr_ref, g_ref, b_ref, o_ref, *, alpha):
    mix = _dot(oa_ref[...], wa_ref[...]) + _dot(or_ref[...], wr_ref[...])
    o_ref[...] = _layer_norm(alpha * x_ref[...] + mix, g_ref[...], b_ref[...])


def out_ln(x, oa, orw, wa, wr, g, b, *, alpha, tm=512):
    T, D = x.shape
    tm = min(tm, T)
    ka, kr = oa.shape[1], orw.shape[1]
    return pl.pallas_call(
        functools.partial(_out_ln_kernel, alpha=alpha),
        grid=(T // tm,),
        in_specs=[
            pl.BlockSpec((tm, D), lambda i: (i, 0)),
            pl.BlockSpec((tm, ka), lambda i: (i, 0)),
            pl.BlockSpec((tm, kr), lambda i: (i, 0)),
            pl.BlockSpec((ka, D), lambda i: (0, 0)),
            pl.BlockSpec((kr, D), lambda i: (0, 0)),
            pl.BlockSpec((1, D), lambda i: (0, 0)),
            pl.BlockSpec((1, D), lambda i: (0, 0)),
        ],
        out_specs=pl.BlockSpec((tm, D), lambda i: (i, 0)),
        out_shape=jax.ShapeDtypeStruct((T, D), F32),
        compiler_params=_cparams(("parallel",)),
    )(x, oa, orw, wa, wr, g, b)


def _layer(x, l, ffn1_w_gate, ffn1_w_up, ffn1_w_down, ln1_g, ln1_b, w_in,
           lambda_q1, lambda_k1, lambda_q2, lambda_k2, attn_norm_g,
           rwkv_mu, rwkv_w0, rwkv_w2, rwkv_a0, rwkv_a2, rwkv_g2,
           rwkv_k_k, rwkv_k_a, rwkv_r_k, rwkv_gn_g, rwkv_gn_b,
           w_out, ln2_g, ln2_b, ffn2_w_gate, ffn2_w_up, ffn2_w_down, ln3_g, ln3_b):
    B, S, D = x.shape
    T = B * S
    alpha = (2.0 * DEPTH) ** 0.25
    lambda_init = 0.8 - 0.6 * math.exp(-0.3 * l)
    rw = rwkv_w0.shape[-1]
    aw = w_out.shape[1] - rw
    H = aw // LANES
    n_wd, n_ad, n_gd = rwkv_w2.shape[1], rwkv_a2.shape[1], rwkv_g2.shape[1]
    assert n_wd == HEAD and n_ad == HEAD and n_gd <= 2 * LANES
    row = lambda a: a.reshape(1, -1)
    bf = lambda a: a.astype(BF16)

    x1 = ffn_ln(x.reshape(T, D), bf(ffn1_w_gate[l]), bf(ffn1_w_up[l]), bf(ffn1_w_down[l]),
                row(ln1_g[l]), row(ln1_b[l]), alpha=alpha)

    wi = w_in[l]
    pad = 3 * LANES - (n_wd + n_ad + n_gd)
    w_rwkv = jnp.pad(wi[:, 3 * aw:], ((0, 0), (0, pad)))
    p_attn = in_proj(x1, bf(wi[:, :3 * aw]), BF16, tn=1024)
    p_rwkv = in_proj(x1, bf(w_rwkv), F32, tn=w_rwkv.shape[1] // 3)

    slopes = jnp.exp2(-8.0 * (jnp.arange(H, dtype=F32) + 1.0) / H)
    o_attn = diff_attn(p_attn, slopes, row(lambda_q1[l]), row(lambda_k1[l]), row(lambda_q2[l]),
                       row(lambda_k2[l]), row(attn_norm_g[l]), B=B, S=S, H=H, lambda_init=lambda_init)

    mu = jnp.pad(rwkv_mu[l], (0, pad))
    w12 = jnp.zeros((LANES, 2 * rw), F32)
    w12 = w12.at[:HEAD, :rw].set(rwkv_w2[l]).at[HEAD:, rw:].set(rwkv_a2[l])
    g2p = jnp.pad(rwkv_g2[l], ((0, 2 * LANES - n_gd), (0, 0)))
    r, lw, k2, v, kk, b, g = rwkv_prep(
        p_rwkv.reshape(B, S, -1), row(mu), row(rwkv_w0[l]), row(rwkv_a0[l]), row(rwkv_k_k[l]),
        row(rwkv_k_a[l]), bf(w12), bf(g2p), width=rw)
    o_rwkv = rwkv_chunk(r, lw, k2, v, kk, b, g, row(rwkv_r_k[l]), row(rwkv_gn_g[l]), row(rwkv_gn_b[l]))

    wo = w_out[l]
    x2 = out_ln(x1, o_attn, o_rwkv.reshape(T, rw), bf(wo[:aw]), bf(wo[aw:]),
                row(ln2_g[l]), row(ln2_b[l]), alpha=alpha)
    x3 = ffn_ln(x2, bf(ffn2_w_gate[l]), bf(ffn2_w_up[l]), bf(ffn2_w_down[l]),
                row(ln3_g[l]), row(ln3_b[l]), alpha=alpha)
    return x3.reshape(B, S, D)


def kernel(x, ffn1_w_gate, ffn1_w_up, ffn1_w_down, ln1_g, ln1_b, w_in, lambda_q1, lambda_k1, lambda_q2, lambda_k2, attn_norm_g, rwkv_mu, rwkv_w0, rwkv_w2, rwkv_a0, rwkv_a2, rwkv_g2, rwkv_k_k, rwkv_k_a, rwkv_r_k, rwkv_gn_g, rwkv_gn_b, w_out, ln2_g, ln2_b, ffn2_w_gate, ffn2_w_up, ffn2_w_down, ln3_g, ln3_b):
    for l in range(DEPTH):
        x = _layer(x, l, ffn1_w_gate, ffn1_w_up, ffn1_w_down, ln1_g, ln1_b, w_in,
                   lambda_q1, lambda_k1, lambda_q2, lambda_k2, attn_norm_g,
                   rwkv_mu, rwkv_w0, rwkv_w2, rwkv_a0, rwkv_a2, rwkv_g2,
                   rwkv_k_k, rwkv_k_a, rwkv_r_k, rwkv_gn_g, rwkv_gn_b,
                   w_out, ln2_g, ln2_b, ffn2_w_gate, ffn2_w_up, ffn2_w_down, ln3_g, ln3_b)
    return x
```

```python
import functools
import math

import jax
import jax.numpy as jnp
from jax import lax
from jax.experimental import pallas as pl
from jax.experimental.pallas import tpu as pltpu

F32 = jnp.float32
BF16 = jnp.bfloat16

DEPTH = 1
LN_EPS = 1e-5
ATTN_NORM_EPS = 1e-5
GN_EPS = 64e-5
HEAD = 64
LANES = 128
CHUNK = 64
VMEM_LIMIT = 56 * 1024 * 1024


def _cparams(sem):
    return pltpu.CompilerParams(dimension_semantics=sem, vmem_limit_bytes=VMEM_LIMIT)


def _layer_norm(y, g, b):
    mu = jnp.mean(y, axis=-1, keepdims=True)
    d = y - mu
    var = jnp.mean(d * d, axis=-1, keepdims=True)
    return d * lax.rsqrt(var + LN_EPS) * g + b


def _dot(a, b):
    return jnp.dot(a, b, preferred_element_type=F32)


def _dot_nt(a, b):
    return lax.dot_general(a, b, (((1,), (1,)), ((), ())), preferred_element_type=F32)


def _dot_tn(a, b):
    return lax.dot_general(a, b, (((0,), (0,)), ((), ())), preferred_element_type=F32)


def _ffn_ln_kernel(x_ref, wg_ref, wu_ref, wd_ref, g_ref, b_ref, o_ref, xb_ref, acc_ref, *, alpha):
    j = pl.program_id(1)

    @pl.when(j == 0)
    def _():
        xb_ref[...] = x_ref[...].astype(BF16)
        acc_ref[...] = jnp.zeros_like(acc_ref)

    xb = xb_ref[...]
    hg = _dot(xb, wg_ref[...])
    hu = _dot(xb, wu_ref[...])
    h = hg * jax.nn.sigmoid(hg) * hu
    acc_ref[...] += _dot(h.astype(BF16), wd_ref[...])

    @pl.when(j == pl.num_programs(1) - 1)
    def _():
        y = alpha * x_ref[...] + 0.5 * acc_ref[...]
        o_ref[...] = _layer_norm(y, g_ref[...], b_ref[...])


def ffn_ln(x, wg, wu, wd, g, b, *, alpha, tm=512, tf=512):
    T, D = x.shape
    Fd = wg.shape[1]
    tm = min(tm, T)
    tf = min(tf, Fd)
    return pl.pallas_call(
        functools.partial(_ffn_ln_kernel, alpha=alpha),
        grid=(T // tm, Fd // tf),
        in_specs=[
            pl.BlockSpec((tm, D), lambda i, j: (i, 0)),
            pl.BlockSpec((D, tf), lambda i, j: (0, j)),
            pl.BlockSpec((D, tf), lambda i, j: (0, j)),
            pl.BlockSpec((tf, D), lambda i, j: (j, 0)),
            pl.BlockSpec((1, D), lambda i, j: (0, 0)),
            pl.BlockSpec((1, D), lambda i, j: (0, 0)),
        ],
        out_specs=pl.BlockSpec((tm, D), lambda i, j: (i, 0)),
        out_shape=jax.ShapeDtypeStruct((T, D), F32),
        scratch_shapes=[pltpu.VMEM((tm, D), BF16), pltpu.VMEM((tm, D), F32)],
        compiler_params=_cparams(("parallel", "arbitrary")),
    )(x, wg, wu, wd, g, b)


def _in_proj_kernel(x_ref, w_ref, o_ref, xb_ref):
    @pl.when(pl.program_id(1) == 0)
    def _():
        xb_ref[...] = x_ref[...].astype(BF16)

    o_ref[...] = _dot(xb_ref[...], w_ref[...]).astype(o_ref.dtype)


def in_proj(x, w, out_dtype, *, tm=512, tn=512):
    T, D = x.shape
    N = w.shape[1]
    tm = min(tm, T)
    tn = min(tn, N)
    return pl.pallas_call(
        _in_proj_kernel,
        grid=(T // tm, N // tn),
        in_specs=[
            pl.BlockSpec((tm, D), lambda i, j: (i, 0)),
            pl.BlockSpec((D, tn), lambda i, j: (0, j)),
        ],
        out_specs=pl.BlockSpec((tm, tn), lambda i, j: (i, j)),
        out_shape=jax.ShapeDtypeStruct((T, N), out_dtype),
        scratch_shapes=[pltpu.VMEM((tm, D), BF16)],
        compiler_params=_cparams(("parallel", "arbitrary")),
    )(x, w)


def _diff_attn_kernel(slopes_ref, q_ref, k_ref, v_ref, lq1_ref, lk1_ref, lq2_ref, lk2_ref, ng_ref,
                      o_ref, qs_ref, *, tq, lambda_init):
    h = pl.program_id(1)
    i = pl.program_id(2)
    slope = slopes_ref[h]

    lane = lax.broadcasted_iota(jnp.int32, (tq, LANES), 1)
    q = q_ref[...] * 0.125
    zero = jnp.zeros_like(q)
    qs_ref[0:tq, :] = jnp.where(lane < HEAD, q, zero)
    qs_ref[tq:2 * tq, :] = jnp.where(lane >= HEAD, q, zero)
    qs = qs_ref[...]

    r = lax.broadcasted_iota(jnp.int32, (2 * tq, tq), 0)
    r = jnp.where(r >= tq, r - tq, r)
    c = lax.broadcasted_iota(jnp.int32, (2 * tq, tq), 1)
    rel = (r - c).astype(F32)
    rel_bias = -slope * rel

    def step(j, carry, masked):
        m, l, acc = carry
        start = pl.multiple_of(j * tq, tq)
        k = k_ref[pl.ds(start, tq), :]
        v = v_ref[pl.ds(start, tq), :]
        s = _dot_nt(qs, k)
        off = -slope * ((i - j) * tq).astype(F32)
        s = s + (rel_bias + off)
        if masked:
            s = jnp.where(rel >= 0.0, s, -jnp.inf)
        m_new = jnp.maximum(m, jnp.max(s, axis=-1, keepdims=True))
        a = jnp.exp(m - m_new)
        p = jnp.exp(s - m_new)
        l = a * l + jnp.sum(p, axis=-1, keepdims=True)
        acc = a * acc + _dot(p.astype(BF16), v)
        return m_new, l, acc

    init = (jnp.full((2 * tq, 1), -1e30, F32), jnp.zeros((2 * tq, 1), F32), jnp.zeros((2 * tq, LANES), F32))
    carry = lax.fori_loop(0, i, lambda j, cr: step(j, cr, False), init)
    m, l, acc = step(i, carry, True)

    lam = (jnp.exp(jnp.sum(lq1_ref[...] * lk1_ref[...], axis=-1, keepdims=True))
           - jnp.exp(jnp.sum(lq2_ref[...] * lk2_ref[...], axis=-1, keepdims=True)) + lambda_init)
    o = acc[0:tq] / l[0:tq] - lam * (acc[tq:2 * tq] / l[tq:2 * tq])
    o = o * lax.rsqrt(jnp.mean(o * o, axis=-1, keepdims=True) + ATTN_NORM_EPS) * ng_ref[...]
    o_ref[...] = (o * (1.0 - lambda_init)).astype(o_ref.dtype)


def diff_attn(pa, slopes, lq1, lk1, lq2, lk2, norm_g, *, B, S, H, lambda_init, tq=256):
    tq = min(tq, S)
    nq = S // tq
    small = lambda n: pl.BlockSpec((1, n), lambda b, h, i: (0, 0))
    return pl.pallas_call(
        functools.partial(_diff_attn_kernel, tq=tq, lambda_init=lambda_init),
        grid=(B, H, nq),
        in_specs=[
            pl.BlockSpec(memory_space=pltpu.SMEM),
            pl.BlockSpec((tq, LANES), lambda b, h, i: (b * nq + i, h)),
            pl.BlockSpec((S, LANES), lambda b, h, i: (b, H + h)),
            pl.BlockSpec((S, LANES), lambda b, h, i: (b, 2 * H + h)),
            small(HEAD), small(HEAD), small(HEAD), small(HEAD), small(2 * HEAD),
        ],
        out_specs=pl.BlockSpec((tq, LANES), lambda b, h, i: (b * nq + i, h)),
        out_shape=jax.ShapeDtypeStruct((B * S, H * LANES), BF16),
        scratch_shapes=[pltpu.VMEM((2 * tq, LANES), BF16)],
        compiler_params=_cparams(("parallel", "parallel", "arbitrary")),
    )(slopes, pa, pa, pa, lq1, lk1, lq2, lk2, norm_g)


def _pair_ones():
    r = lax.broadcasted_iota(jnp.int32, (LANES, LANES), 0)
    c = lax.broadcasted_iota(jnp.int32, (LANES, LANES), 1)
    return jnp.where((r < HEAD) == (c < HEAD), 1.0, 0.0).astype(BF16)


def _head_sum(x, ones):
    hi = x.astype(BF16)
    lo = (x - hi.astype(F32)).astype(BF16)
    return _dot(hi, ones) + _dot(lo, ones)


def _rwkv_prep_kernel(pr_ref, prev_ref, mu_ref, w0_ref, a0_ref, kk_ref, ka_ref, w12_ref, g2_ref,
                      r_ref, lw_ref, k2_ref, v_ref, kkn_ref, b_ref, g_ref, *, width):
    i = pl.program_id(1)
    x = pr_ref[0]
    tt = x.shape[0]
    prow = prev_ref[0, 7:8, :]
    prow = jnp.where(i == 0, jnp.zeros_like(prow), prow)
    row = lax.broadcasted_iota(jnp.int32, x.shape, 0)
    xs = jnp.where(row == 0, prow, pltpu.roll(x, 1, axis=0))
    xm = x + mu_ref[...] * (xs - x)

    c3 = 3 * width
    lora_in = xm[:, c3:c3 + LANES]
    lane = lax.broadcasted_iota(jnp.int32, lora_in.shape, 1)
    lora_in = jnp.where(lane < HEAD, jnp.tanh(lora_in), lora_in)
    z = _dot(lora_in.astype(BF16), w12_ref[...])
    gate_in = jax.nn.sigmoid(xm[:, c3 + LANES:c3 + 3 * LANES])
    g_ref[0] = _dot(gate_in.astype(BF16), g2_ref[...])

    lw_ref[0] = -math.exp(-0.5) * jax.nn.sigmoid(w0_ref[...] + z[:, 0:width])
    a = jax.nn.sigmoid(a0_ref[...] + z[:, width:2 * width])

    r_ref[0] = xm[:, 0:width]
    v_ref[0] = xm[:, 2 * width:3 * width]
    k = xm[:, width:2 * width]
    k2_ref[0] = k * (1.0 + (a - 1.0) * ka_ref[...])

    kx = k * kk_ref[...]
    ones = _pair_ones()
    ss = jnp.concatenate(
        [_head_sum(kx[:, t * LANES:(t + 1) * LANES] * kx[:, t * LANES:(t + 1) * LANES], ones)
         for t in range(width // LANES)], axis=1)
    kkn = kx / jnp.maximum(jnp.sqrt(ss), 1e-12)
    kkn_ref[0] = kkn
    b_ref[0] = kkn * a


def rwkv_prep(pr, mu, w0, a0, k_k, k_a, w12, g2p, *, width, tt=256):
    B, S, W = pr.shape
    tt = min(tt, S)
    row = lambda n: pl.BlockSpec((1, n), lambda b, i: (0, 0))
    out = pl.BlockSpec((1, tt, width), lambda b, i: (b, i, 0))
    return pl.pallas_call(
        functools.partial(_rwkv_prep_kernel, width=width),
        grid=(B, S // tt),
        in_specs=[
            pl.BlockSpec((1, tt, W), lambda b, i: (b, i, 0)),
            pl.BlockSpec((1, 8, W), lambda b, i: (b, jnp.maximum(i * (tt // 8) - 1, 0), 0)),
            row(W), row(width), row(width), row(width), row(width),
            pl.BlockSpec((LANES, 2 * width), lambda b, i: (0, 0)),
            pl.BlockSpec((2 * LANES, width), lambda b, i: (0, 0)),
        ],
        out_specs=[out] * 7,
        out_shape=[jax.ShapeDtypeStruct((B, S, width), F32)] * 7,
        compiler_params=_cparams(("parallel", "arbitrary")),
    )(pr, pr, mu, w0, a0, k_k, k_a, w12, g2p)


def _block_diag(x):
    lane = lax.broadcasted_iota(jnp.int32, x.shape, 1)
    zero = jnp.zeros_like(x)
    return jnp.concatenate([jnp.where(lane < HEAD, x, zero), jnp.where(lane >= HEAD, x, zero)], axis=0)


def _chunk_step(rs, lws, k2s, vs, kks, bs, sts):
    C = CHUNK
    n = len(rs)
    idx = range(n)
    row = lax.broadcasted_iota(jnp.int32, (C, LANES), 0)
    lane = lax.broadcasted_iota(jnp.int32, (C, LANES), 1)
    tcol = jnp.where(lane >= HEAD, lane - HEAD, lane)
    strict = row > tcol
    incl = row >= tcol
    eye = jnp.where(row == tcol, 1.0, 0.0)
    bdb = lambda x: _block_diag(x).astype(BF16)

    cums = list(lws)
    sh = 1
    while sh < C:
        cums = [cm + jnp.where(row >= sh, pltpu.roll(cm, sh, axis=0), 0.0) for cm in cums]
        sh *= 2
    tots = [cm[C - 1:C, :] for cm in cums]
    e_negs = [jnp.exp(-cm) for cm in cums]
    ARs = [jnp.concatenate([-kks[i] * jnp.exp(cums[i] - lws[i]), rs[i] * jnp.exp(cums[i])], axis=0).astype(BF16)
           for i in idx]
    Mbs = [_dot_nt(ARs[i], bdb(bs[i] * e_negs[i])) for i in idx]
    Mks = [_dot_nt(ARs[i], bdb(k2s[i] * e_negs[i])) for i in idx]
    ARSs = [_dot_nt(ARs[i], sts[i].astype(BF16)) for i in idx]
    vbds = [bdb(v) for v in vs]

    Ls = [jnp.where(strict, Mbs[i][0:C], 0.0) for i in idx]
    Ps = [eye + L for L in Ls]
    Lps = [_dot(L.astype(BF16), bdb(L)) for L in Ls]
    Xs = [ARSs[i][0:C] + _dot(jnp.where(strict, Mks[i][0:C], 0.0).astype(BF16), vbds[i]) for i in idx]
    m = 2
    while 2 * m < C:
        both = [_dot(jnp.concatenate([Lps[i], Ps[i]], axis=0).astype(BF16), bdb(Lps[i])) for i in idx]
        Ps = [Ps[i] + both[i][C:2 * C] for i in idx]
        Lps = [bt[0:C] for bt in both]
        m *= 2
    Ps = [Ps[i] + _dot(Ps[i].astype(BF16), bdb(Lps[i])) for i in idx]

    Us = [_dot(Ps[i].astype(BF16), bdb(Xs[i])) for i in idx]
    Ys = [ARSs[i][C:2 * C]
          + _dot(jnp.concatenate([jnp.where(incl, Mbs[i][C:2 * C], 0.0),
                                  jnp.where(incl, Mks[i][C:2 * C], 0.0)], axis=1).astype(BF16),
                 jnp.concatenate([bdb(Us[i]), vbds[i]], axis=0)) for i in idx]
    r2 = lax.broadcasted_iota(jnp.int32, (LANES, LANES), 0)
    c2 = lax.broadcasted_iota(jnp.int32, (LANES, LANES), 1)
    same_head = (r2 < HEAD) == (c2 < HEAD)
    st_news = []
    for i in idx:
        e_rem = jnp.exp(tots[i] - cums[i])
        UV = jnp.concatenate([Us[i], vs[i]], axis=0).astype(BF16)
        BK = jnp.concatenate([bs[i] * e_rem, k2s[i] * e_rem], axis=0).astype(BF16)
        st_news.append(jnp.where(same_head, sts[i] * jnp.exp(tots[i]) + _dot_tn(UV, BK), 0.0))
    return Ys, st_news


def _rwkv_chunk_kernel(r_ref, lw_ref, k2_ref, v_ref, kk_ref, b_ref, g_ref, rk_ref, gng_ref, gnb_ref,
                       o_ref, st_ref, *, nbatch, npairs):
    c = pl.program_id(2)

    @pl.when(c == 0)
    def _():
        st_ref[...] = jnp.zeros_like(st_ref)

    ones = _pair_ones()
    chains = [(bi, p) for bi in range(nbatch) for p in range(npairs)]
    sl = lambda p: slice(p * LANES, (p + 1) * LANES)
    get = lambda ref: [ref[bi, :, sl(p)] for bi, p in chains]
    rs, k2s, vs = get(r_ref), get(k2_ref), get(v_ref)
    ys, st_news = _chunk_step(rs, get(lw_ref), k2s, vs, get(kk_ref), get(b_ref),
                              [st_ref[i] for i in range(len(chains))])
    for i in range(len(chains)):
        st_ref[i] = st_news[i]
    ds = [y - _head_sum(y, ones) * (1.0 / HEAD) for y in ys]
    vars_ = [_head_sum(d * d, ones) * (1.0 / HEAD) for d in ds]
    for i, (bi, p) in enumerate(chains):
        yn = ds[i] * lax.rsqrt(vars_[i] + GN_EPS) * gng_ref[:, sl(p)] + gnb_ref[:, sl(p)]
        bonus = _head_sum(rs[i] * k2s[i] * rk_ref[:, sl(p)], ones) * vs[i]
        o_ref[bi, :, sl(p)] = ((yn + bonus) * g_ref[bi, :, sl(p)]).astype(o_ref.dtype)


def rwkv_chunk(r, lw, k2, v, kk, b, g, r_k, gn_g, gn_b, *, nbatch=2, npairs=8):
    B, S, W = r.shape
    npairs = min(npairs, W // LANES)
    nbatch = min(nbatch, B)
    wb = npairs * LANES
    seq = pl.BlockSpec((nbatch, CHUNK, wb), lambda bi, p, c: (bi, c, p))
    par = pl.BlockSpec((1, wb), lambda bi, p, c: (0, p))
    return pl.pallas_call(
        functools.partial(_rwkv_chunk_kernel, nbatch=nbatch, npairs=npairs),
        grid=(B // nbatch, W // wb, S // CHUNK),
        in_specs=[seq] * 7 + [par] * 3,
        out_specs=seq,
        out_shape=jax.ShapeDtypeStruct((B, S, W), BF16),
        scratch_shapes=[pltpu.VMEM((nbatch * npairs, LANES, LANES), F32)],
        compiler_params=_cparams(("parallel", "parallel", "arbitrary")),
    )(r, lw, k2, v, kk, b, g, r_k, gn_g, gn_b)


def _out_ln_kernel(x_ref, oa_ref, or_ref, wa_ref, wr_ref, g_ref, b_ref, o_ref, *, alpha):
    mix = _dot(oa_ref[...], wa_ref[...]) + _dot(or_ref[...], wr_ref[...])
    o_ref[...] = _layer_norm(alpha * x_ref[...] + mix, g_ref[...], b_ref[...])


def out_ln(x, oa, orw, wa, wr, g, b, *, alpha, tm=512):
    T, D = x.shape
    tm = min(tm, T)
    ka, kr = oa.shape[1], orw.shape[1]
    return pl.pallas_call(
        functools.partial(_out_ln_kernel, alpha=alpha),
        grid=(T // tm,),
        in_specs=[
            pl.BlockSpec((tm, D), lambda i: (i, 0)),
            pl.BlockSpec((tm, ka), lambda i: (i, 0)),
            pl.BlockSpec((tm, kr), lambda i: (i, 0)),
            pl.BlockSpec((ka, D), lambda i: (0, 0)),
            pl.BlockSpec((kr, D), lambda i: (0, 0)),
            pl.BlockSpec((1, D), lambda i: (0, 0)),
            pl.BlockSpec((1, D), lambda i: (0, 0)),
        ],
        out_specs=pl.BlockSpec((tm, D), lambda i: (i, 0)),
        out_shape=jax.ShapeDtypeStruct((T, D), F32),
        compiler_params=_cparams(("parallel",)),
    )(x, oa, orw, wa, wr, g, b)


def _layer(x, l, ffn1_w_gate, ffn1_w_up, ffn1_w_down, ln1_g, ln1_b, w_in,
           lambda_q1, lambda_k1, lambda_q2, lambda_k2, attn_norm_g,
           rwkv_mu, rwkv_w0, rwkv_w2, rwkv_a0, rwkv_a2, rwkv_g2,
           rwkv_k_k, rwkv_k_a, rwkv_r_k, rwkv_gn_g, rwkv_gn_b,
           w_out, ln2_g, ln2_b, ffn2_w_gate, ffn2_w_up, ffn2_w_down, ln3_g, ln3_b):
    B, S, D = x.shape
    T = B * S
    alpha = (2.0 * DEPTH) ** 0.25
    lambda_init = 0.8 - 0.6 * math.exp(-0.3 * l)
    rw = rwkv_w0.shape[-1]
    aw = w_out.shape[1] - rw
    H = aw // LANES
    n_wd, n_ad, n_gd = rwkv_w2.shape[1], rwkv_a2.shape[1], rwkv_g2.shape[1]
    assert n_wd == HEAD and n_ad == HEAD and n_gd <= 2 * LANES
    row = lambda a: a.reshape(1, -1)
    bf = lambda a: a.astype(BF16)

    x1 = ffn_ln(x.reshape(T, D), bf(ffn1_w_gate[l]), bf(ffn1_w_up[l]), bf(ffn1_w_down[l]),
                row(ln1_g[l]), row(ln1_b[l]), alpha=alpha)

    wi = w_in[l]
    pad = 3 * LANES - (n_wd + n_ad + n_gd)
    w_rwkv = jnp.pad(wi[:, 3 * aw:], ((0, 0), (0, pad)))
    p_attn = in_proj(x1, bf(wi[:, :3 * aw]), BF16, tn=1024)
    p_rwkv = in_proj(x1, bf(w_rwkv), F32, tn=w_rwkv.shape[1] // 3)

    slopes = jnp.exp2(-8.0 * (jnp.arange(H, dtype=F32) + 1.0) / H)
    o_attn = diff_attn(p_attn, slopes, row(lambda_q1[l]), row(lambda_k1[l]), row(lambda_q2[l]),
                       row(lambda_k2[l]), row(attn_norm_g[l]), B=B, S=S, H=H, lambda_init=lambda_init)

    mu = jnp.pad(rwkv_mu[l], (0, pad))
    w12 = jnp.zeros((LANES, 2 * rw), F32)
    w12 = w12.at[:HEAD, :rw].set(rwkv_w2[l]).at[HEAD:, rw:].set(rwkv_a2[l])
    g2p = jnp.pad(rwkv_g2[l], ((0, 2 * LANES - n_gd), (0, 0)))
    r, lw, k2, v, kk, b, g = rwkv_prep(
        p_rwkv.reshape(B, S, -1), row(mu), row(rwkv_w0[l]), row(rwkv_a0[l]), row(rwkv_k_k[l]),
        row(rwkv_k_a[l]), bf(w12), bf(g2p), width=rw)
    o_rwkv = rwkv_chunk(r, lw, k2, v, kk, b, g, row(rwkv_r_k[l]), row(rwkv_gn_g[l]), row(rwkv_gn_b[l]))

    wo = w_out[l]
    x2 = out_ln(x1, o_attn, o_rwkv.reshape(T, rw), bf(wo[:aw]), bf(wo[aw:]),
                row(ln2_g[l]), row(ln2_b[l]), alpha=alpha)
    x3 = ffn_ln(x2, bf(ffn2_w_gate[l]), bf(ffn2_w_up[l]), bf(ffn2_w_down[l]),
                row(ln3_g[l]), row(ln3_b[l]), alpha=alpha)
    return x3.reshape(B, S, D)


def kernel(x, ffn1_w_gate, ffn1_w_up, ffn1_w_down, ln1_g, ln1_b, w_in, lambda_q1, lambda_k1, lambda_q2, lambda_k2, attn_norm_g, rwkv_mu, rwkv_w0, rwkv_w2, rwkv_a0, rwkv_a2, rwkv_g2, rwkv_k_k, rwkv_k_a, rwkv_r_k, rwkv_gn_g, rwkv_gn_b, w_out, ln2_g, ln2_b, ffn2_w_gate, ffn2_w_up, ffn2_w_down, ln3_g, ln3_b):
    for l in range(DEPTH):
        x = _layer(x, l, ffn1_w_gate, ffn1_w_up, ffn1_w_down, ln1_g, ln1_b, w_in,
                   lambda_q1, lambda_k1, lambda_q2, lambda_k2, attn_norm_g,
                   rwkv_mu, rwkv_w0, rwkv_w2, rwkv_a0, rwkv_a2, rwkv_g2,
                   rwkv_k_k, rwkv_k_a, rwkv_r_k, rwkv_gn_g, rwkv_gn_b,
                   w_out, ln2_g, ln2_b, ffn2_w_gate, ffn2_w_up, ffn2_w_down, ln3_g, ln3_b)
    return x
```

```python
import functools
import math

import jax
import jax.numpy as jnp
from jax import lax
from jax.experimental import pallas as pl
from jax.experimental.pallas import tpu as pltpu

F32 = jnp.float32
BF16 = jnp.bfloat16

DEPTH = 1
LN_EPS = 1e-5
ATTN_NORM_EPS = 1e-5
GN_EPS = 64e-5
HEAD = 64
LANES = 128
CHUNK = 64
VMEM_LIMIT = 56 * 1024 * 1024


def _cparams(sem):
    return pltpu.CompilerParams(dimension_semantics=sem, vmem_limit_bytes=VMEM_LIMIT)


def _layer_norm(y, g, b):
    mu = jnp.mean(y, axis=-1, keepdims=True)
    d = y - mu
    var = jnp.mean(d * d, axis=-1, keepdims=True)
    return d * lax.rsqrt(var + LN_EPS) * g + b


def _dot(a, b):
    return jnp.dot(a, b, preferred_element_type=F32)


def _dot_nt(a, b):
    return lax.dot_general(a, b, (((1,), (1,)), ((), ())), preferred_element_type=F32)


def _dot_tn(a, b):
    return lax.dot_general(a, b, (((0,), (0,)), ((), ())), preferred_element_type=F32)


def _ffn_ln_kernel(x_ref, wg_ref, wu_ref, wd_ref, g_ref, b_ref, o_ref, xb_ref, acc_ref, *, alpha):
    j = pl.program_id(1)

    @pl.when(j == 0)
    def _():
        xb_ref[...] = x_ref[...].astype(BF16)
        acc_ref[...] = jnp.zeros_like(acc_ref)

    xb = xb_ref[...]
    hg = _dot(xb, wg_ref[...])
    hu = _dot(xb, wu_ref[...])
    h = hg * jax.nn.sigmoid(hg) * hu
    acc_ref[...] += _dot(h.astype(BF16), wd_ref[...])

    @pl.when(j == pl.num_programs(1) - 1)
    def _():
        y = alpha * x_ref[...] + 0.5 * acc_ref[...]
        o_ref[...] = _layer_norm(y, g_ref[...], b_ref[...])


def ffn_ln(x, wg, wu, wd, g, b, *, alpha, tm=512, tf=512):
    T, D = x.shape
    Fd = wg.shape[1]
    tm = min(tm, T)
    tf = min(tf, Fd)
    return pl.pallas_call(
        functools.partial(_ffn_ln_kernel, alpha=alpha),
        grid=(T // tm, Fd // tf),
        in_specs=[
            pl.BlockSpec((tm, D), lambda i, j: (i, 0)),
            pl.BlockSpec((D, tf), lambda i, j: (0, j)),
            pl.BlockSpec((D, tf), lambda i, j: (0, j)),
            pl.BlockSpec((tf, D), lambda i, j: (j, 0)),
            pl.BlockSpec((1, D), lambda i, j: (0, 0)),
            pl.BlockSpec((1, D), lambda i, j: (0, 0)),
        ],
        out_specs=pl.BlockSpec((tm, D), lambda i, j: (i, 0)),
        out_shape=jax.ShapeDtypeStruct((T, D), F32),
        scratch_shapes=[pltpu.VMEM((tm, D), BF16), pltpu.VMEM((tm, D), F32)],
        compiler_params=_cparams(("parallel", "arbitrary")),
    )(x, wg, wu, wd, g, b)


def _in_proj_kernel(x_ref, w_ref, o_ref, xb_ref, *, n_scaled, scale):
    j = pl.program_id(1)

    @pl.when(j == 0)
    def _():
        xb_ref[...] = x_ref[...].astype(BF16)

    acc = _dot(xb_ref[...], w_ref[...])
    if n_scaled:
        acc = acc * jnp.where(j < n_scaled, scale, 1.0)
    o_ref[...] = acc.astype(o_ref.dtype)


def in_proj(x, w, out_dtype, *, tm=512, tn=512, n_scaled=0, scale=1.0):
    T, D = x.shape
    N = w.shape[1]
    tm = min(tm, T)
    tn = min(tn, N)
    return pl.pallas_call(
        functools.partial(_in_proj_kernel, n_scaled=n_scaled, scale=scale),
        grid=(T // tm, N // tn),
        in_specs=[
            pl.BlockSpec((tm, D), lambda i, j: (i, 0)),
            pl.BlockSpec((D, tn), lambda i, j: (0, j)),
        ],
        out_specs=pl.BlockSpec((tm, tn), lambda i, j: (i, j)),
        out_shape=jax.ShapeDtypeStruct((T, N), out_dtype),
        scratch_shapes=[pltpu.VMEM((tm, D), BF16)],
        compiler_params=_cparams(("parallel", "arbitrary")),
    )(x, w)


ATT_TQ = 128
ATT_TK = 512
N_BIAS_TERMS = 3
BF16_ROWS = 16
VT_ROWS = LANES + BF16_ROWS


def alibi_tables(H, S):
    slope = jnp.exp2(-8.0 * (jnp.arange(H, dtype=F32) + 1.0) / H) * math.log2(math.e)
    terms, rest = [], slope
    for _ in range(N_BIAS_TERMS):
        t = rest.astype(BF16).astype(F32)
        terms += [64.0 * t, t]
        rest = rest - t
    tab = jnp.stack(terms + [jnp.zeros_like(slope)] * (8 - len(terms)), axis=1)
    pos = jnp.arange(S, dtype=jnp.int32)
    hi, lo = (pos // 64).astype(F32), (pos % 64).astype(F32)
    feat = jnp.stack([hi, lo] * N_BIAS_TERMS, axis=1)
    feat = jnp.pad(feat, ((0, 0), (0, LANES - feat.shape[1]))).astype(BF16)
    return tab, feat


def _diff_attn_kernel(tab_ref, q_ref, k_ref, v_ref, feat_ref, lq1_ref, lk1_ref, lq2_ref, lk2_ref, ng_ref,
                      o_ref, ka_ref, vt_ref, qa_ref, acc_ref, m_ref, *, hb, nck, lambda_init):
    tq = ATT_TQ
    tk = vt_ref.shape[2]
    g = pl.program_id(1)
    i = pl.program_id(2)
    heads = range(hb)
    hs = lambda h: slice(h * LANES, (h + 1) * LANES)

    @pl.when(i == 0)
    def _():
        for h in heads:
            ka_ref[h, :, 0:LANES] = k_ref[:, hs(h)]
            ka_ref[h, :, LANES:2 * LANES] = feat_ref[...]
            for c in range(nck):
                vt_ref[h * nck + c, 0:LANES, :] = v_ref[c * tk:(c + 1) * tk, hs(h)].astype(F32).T.astype(BF16)
                vt_ref[h * nck + c, LANES:VT_ROWS, :] = jnp.ones((BF16_ROWS, tk), BF16)

    lane = lax.broadcasted_iota(jnp.int32, (tq, LANES), 1)
    for h in heads:
        q = q_ref[:, hs(h)]
        zero = jnp.zeros_like(q)
        qf = jnp.zeros((tq, LANES), F32)
        for n in range(2 * N_BIAS_TERMS):
            qf = jnp.where(lane == n, tab_ref[g * hb + h, n], qf)
        qf = qf.astype(BF16)
        qa_ref[h, 0:tq, 0:LANES] = jnp.where(lane < HEAD, q, zero)
        qa_ref[h, tq:2 * tq, 0:LANES] = jnp.where(lane >= HEAD, q, zero)
        qa_ref[h, 0:tq, LANES:2 * LANES] = qf
        qa_ref[h, tq:2 * tq, LANES:2 * LANES] = qf
        m_ref[h] = jnp.full((1, 2 * tq), -1e30, F32)
        acc_ref[h] = jnp.zeros((VT_ROWS, 2 * tq), F32)

    def chunk(c, masked):
        start = pl.multiple_of(c * tk, tk)
        ss = [_dot_nt(ka_ref[h, pl.ds(start, tk), :], qa_ref[h]) for h in heads]
        if masked:
            key = start + lax.broadcasted_iota(jnp.int32, (tk, 2 * tq), 0)
            qcol = lax.broadcasted_iota(jnp.int32, (tk, 2 * tq), 1)
            qpos = i * tq + jnp.where(qcol >= tq, qcol - tq, qcol)
            keep = key <= qpos
            ss = [jnp.where(keep, s, -jnp.inf) for s in ss]
        m_olds = [m_ref[h] for h in heads]
        m_news = [jnp.maximum(m_olds[h], jnp.max(ss[h], axis=0, keepdims=True)) for h in heads]
        ps = [jnp.exp2(ss[h] - m_news[h]).astype(BF16) for h in heads]
        alphas = [jnp.exp2(m_olds[h] - m_news[h]) for h in heads]
        pvs = [_dot(vt_ref[h * nck + c], ps[h]) for h in heads]
        for h in heads:
            m_ref[h] = m_news[h]
            acc_ref[h] = alphas[h] * acc_ref[h] + pvs[h]

    n_full = i // (tk // tq)

    def body(c, carry):
        chunk(c, False)
        return carry

    lax.fori_loop(0, n_full, body, 0)
    chunk(n_full, True)

    lam = (jnp.exp(jnp.sum(lq1_ref[...] * lk1_ref[...], axis=-1, keepdims=True))
           - jnp.exp(jnp.sum(lq2_ref[...] * lk2_ref[...], axis=-1, keepdims=True)) + lambda_init)
    for h in heads:
        acc = acc_ref[h]
        rl = 1.0 / acc[LANES:LANES + 1, :]
        ot = (acc[0:LANES, 0:tq] * rl[:, 0:tq]
              - lam * (acc[0:LANES, tq:2 * tq] * rl[:, tq:2 * tq]))
        o = ot.T
        o = o * lax.rsqrt(jnp.mean(o * o, axis=-1, keepdims=True) + ATTN_NORM_EPS) * ng_ref[...]
        o_ref[:, hs(h)] = (o * (1.0 - lambda_init)).astype(o_ref.dtype)


def diff_attn(pa, tab, feat, lq1, lk1, lq2, lk2, norm_g, *, B, S, H, lambda_init, hb=8):
    hb = min(hb, H)
    tq, tk = ATT_TQ, min(ATT_TK, S)
    assert S % tk == 0 and H % hb == 0 and tk % tq == 0
    nq, nck, ng = S // tq, S // tk, H // hb
    wb = hb * LANES
    small = lambda n: pl.BlockSpec((1, n), lambda b, g, i: (0, 0))
    return pl.pallas_call(
        functools.partial(_diff_attn_kernel, hb=hb, nck=nck, lambda_init=lambda_init),
        grid=(B, ng, nq),
        in_specs=[
            pl.BlockSpec(memory_space=pltpu.SMEM),
            pl.BlockSpec((tq, wb), lambda b, g, i: (b * nq + i, g)),
            pl.BlockSpec((S, wb), lambda b, g, i: (b, ng + g)),
            pl.BlockSpec((S, wb), lambda b, g, i: (b, 2 * ng + g)),
            pl.BlockSpec((S, LANES), lambda b, g, i: (0, 0)),
            small(HEAD), small(HEAD), small(HEAD), small(HEAD), small(2 * HEAD),
        ],
        out_specs=pl.BlockSpec((tq, wb), lambda b, g, i: (b * nq + i, g)),
        out_shape=jax.ShapeDtypeStruct((B * S, H * LANES), BF16),
        scratch_shapes=[
            pltpu.VMEM((hb, S, 2 * LANES), BF16),
            pltpu.VMEM((hb * nck, VT_ROWS, tk), BF16),
            pltpu.VMEM((hb, 2 * tq, 2 * LANES), BF16),
            pltpu.VMEM((hb, VT_ROWS, 2 * tq), F32),
            pltpu.VMEM((hb, 1, 2 * tq), F32),
        ],
        compiler_params=_cparams(("parallel", "parallel", "arbitrary")),
    )(tab, pa, pa, pa, feat, lq1, lk1, lq2, lk2, norm_g)


def _pair_ones():
    r = lax.broadcasted_iota(jnp.int32, (LANES, LANES), 0)
    c = lax.broadcasted_iota(jnp.int32, (LANES, LANES), 1)
    return jnp.where((r < HEAD) == (c < HEAD), 1.0, 0.0).astype(BF16)


def _head_sum(x, ones):
    hi = x.astype(BF16)
    lo = (x - hi.astype(F32)).astype(BF16)
    return _dot(hi, ones) + _dot(lo, ones)


def _rwkv_prep_kernel(pr_ref, prev_ref, mu_ref, w0_ref, a0_ref, kk_ref, ka_ref, w12_ref, g2_ref,
                      r_ref, lw_ref, k2_ref, v_ref, kkn_ref, b_ref, g_ref, *, width):
    i = pl.program_id(1)
    x = pr_ref[0]
    tt = x.shape[0]
    prow = prev_ref[0, 7:8, :]
    prow = jnp.where(i == 0, jnp.zeros_like(prow), prow)
    row = lax.broadcasted_iota(jnp.int32, x.shape, 0)
    xs = jnp.where(row == 0, prow, pltpu.roll(x, 1, axis=0))
    xm = x + mu_ref[...] * (xs - x)

    c3 = 3 * width
    lora_in = xm[:, c3:c3 + LANES]
    lane = lax.broadcasted_iota(jnp.int32, lora_in.shape, 1)
    lora_in = jnp.where(lane < HEAD, jnp.tanh(lora_in), lora_in)
    z = _dot(lora_in.astype(BF16), w12_ref[...])
    gate_in = jax.nn.sigmoid(xm[:, c3 + LANES:c3 + 3 * LANES])
    g_ref[0] = _dot(gate_in.astype(BF16), g2_ref[...])

    lw_ref[0] = -math.exp(-0.5) * jax.nn.sigmoid(w0_ref[...] + z[:, 0:width])
    a = jax.nn.sigmoid(a0_ref[...] + z[:, width:2 * width])

    r_ref[0] = xm[:, 0:width]
    v_ref[0] = xm[:, 2 * width:3 * width]
    k = xm[:, width:2 * width]
    k2_ref[0] = k * (1.0 + (a - 1.0) * ka_ref[...])

    kx = k * kk_ref[...]
    ones = _pair_ones()
    ss = jnp.concatenate(
        [_head_sum(kx[:, t * LANES:(t + 1) * LANES] * kx[:, t * LANES:(t + 1) * LANES], ones)
         for t in range(width // LANES)], axis=1)
    kkn = kx / jnp.maximum(jnp.sqrt(ss), 1e-12)
    kkn_ref[0] = kkn
    b_ref[0] = kkn * a


def rwkv_prep(pr, mu, w0, a0, k_k, k_a, w12, g2p, *, width, tt=256):
    B, S, W = pr.shape
    tt = min(tt, S)
    row = lambda n: pl.BlockSpec((1, n), lambda b, i: (0, 0))
    out = pl.BlockSpec((1, tt, width), lambda b, i: (b, i, 0))
    return pl.pallas_call(
        functools.partial(_rwkv_prep_kernel, width=width),
        grid=(B, S // tt),
        in_specs=[
            pl.BlockSpec((1, tt, W), lambda b, i: (b, i, 0)),
            pl.BlockSpec((1, 8, W), lambda b, i: (b, jnp.maximum(i * (tt // 8) - 1, 0), 0)),
            row(W), row(width), row(width), row(width), row(width),
            pl.BlockSpec((LANES, 2 * width), lambda b, i: (0, 0)),
            pl.BlockSpec((2 * LANES, width), lambda b, i: (0, 0)),
        ],
        out_specs=[out] * 7,
        out_shape=[jax.ShapeDtypeStruct((B, S, width), F32)] * 7,
        compiler_params=_cparams(("parallel", "arbitrary")),
    )(pr, pr, mu, w0, a0, k_k, k_a, w12, g2p)


def _block_diag(x):
    lane = lax.broadcasted_iota(jnp.int32, x.shape, 1)
    zero = jnp.zeros_like(x)
    return jnp.concatenate([jnp.where(lane < HEAD, x, zero), jnp.where(lane >= HEAD, x, zero)], axis=0)


def _chunk_step(rs, lws, k2s, vs, kks, bs, sts):
    C = CHUNK
    n = len(rs)
    idx = range(n)
    row = lax.broadcasted_iota(jnp.int32, (C, LANES), 0)
    lane = lax.broadcasted_iota(jnp.int32, (C, LANES), 1)
    tcol = jnp.where(lane >= HEAD, lane - HEAD, lane)
    strict = row > tcol
    incl = row >= tcol
    eye = jnp.where(row == tcol, 1.0, 0.0)
    bdb = lambda x: _block_diag(x).astype(BF16)

    cums = list(lws)
    sh = 1
    while sh < C:
        cums = [cm + jnp.where(row >= sh, pltpu.roll(cm, sh, axis=0), 0.0) for cm in cums]
        sh *= 2
    tots = [cm[C - 1:C, :] for cm in cums]
    e_negs = [jnp.exp(-cm) for cm in cums]
    ARs = [jnp.concatenate([-kks[i] * jnp.exp(cums[i] - lws[i]), rs[i] * jnp.exp(cums[i])], axis=0).astype(BF16)
           for i in idx]
    Mbs = [_dot_nt(ARs[i], bdb(bs[i] * e_negs[i])) for i in idx]
    Mks = [_dot_nt(ARs[i], bdb(k2s[i] * e_negs[i])) for i in idx]
    ARSs = [_dot_nt(ARs[i], sts[i].astype(BF16)) for i in idx]
    vbds = [bdb(v) for v in vs]

    Ls = [jnp.where(strict, Mbs[i][0:C], 0.0) for i in idx]
    Ps = [eye + L for L in Ls]
    Lps = [_dot(L.astype(BF16), bdb(L)) for L in Ls]
    Xs = [ARSs[i][0:C] + _dot(jnp.where(strict, Mks[i][0:C], 0.0).astype(BF16), vbds[i]) for i in idx]
    m = 2
    while 2 * m < C:
        both = [_dot(jnp.concatenate([Lps[i], Ps[i]], axis=0).astype(BF16), bdb(Lps[i])) for i in idx]
        Ps = [Ps[i] + both[i][C:2 * C] for i in idx]
        Lps = [bt[0:C] for bt in both]
        m *= 2
    Ps = [Ps[i] + _dot(Ps[i].astype(BF16), bdb(Lps[i])) for i in idx]

    Us = [_dot(Ps[i].astype(BF16), bdb(Xs[i])) for i in idx]
    Ys = [ARSs[i][C:2 * C]
          + _dot(jnp.concatenate([jnp.where(incl, Mbs[i][C:2 * C], 0.0),
                                  jnp.where(incl, Mks[i][C:2 * C], 0.0)], axis=1).astype(BF16),
                 jnp.concatenate([bdb(Us[i]), vbds[i]], axis=0)) for i in idx]
    r2 = lax.broadcasted_iota(jnp.int32, (LANES, LANES), 0)
    c2 = lax.broadcasted_iota(jnp.int32, (LANES, LANES), 1)
    same_head = (r2 < HEAD) == (c2 < HEAD)
    st_news = []
    for i in idx:
        e_rem = jnp.exp(tots[i] - cums[i])
        UV = jnp.concatenate([Us[i], vs[i]], axis=0).astype(BF16)
        BK = jnp.concatenate([bs[i] * e_rem, k2s[i] * e_rem], axis=0).astype(BF16)
        st_news.append(jnp.where(same_head, sts[i] * jnp.exp(tots[i]) + _dot_tn(UV, BK), 0.0))
    return Ys, st_news


def _rwkv_chunk_kernel(r_ref, lw_ref, k2_ref, v_ref, kk_ref, b_ref, g_ref, rk_ref, gng_ref, gnb_ref,
                       o_ref, st_ref, *, nbatch, npairs):
    c = pl.program_id(2)

    @pl.when(c == 0)
    def _():
        st_ref[...] = jnp.zeros_like(st_ref)

    ones = _pair_ones()
    chains = [(bi, p) for bi in range(nbatch) for p in range(npairs)]
    sl = lambda p: slice(p * LANES, (p + 1) * LANES)
    get = lambda ref: [ref[bi, :, sl(p)] for bi, p in chains]
    rs, k2s, vs = get(r_ref), get(k2_ref), get(v_ref)
    ys, st_news = _chunk_step(rs, get(lw_ref), k2s, vs, get(kk_ref), get(b_ref),
                              [st_ref[i] for i in range(len(chains))])
    for i in range(len(chains)):
        st_ref[i] = st_news[i]
    ds = [y - _head_sum(y, ones) * (1.0 / HEAD) for y in ys]
    vars_ = [_head_sum(d * d, ones) * (1.0 / HEAD) for d in ds]
    for i, (bi, p) in enumerate(chains):
        yn = ds[i] * lax.rsqrt(vars_[i] + GN_EPS) * gng_ref[:, sl(p)] + gnb_ref[:, sl(p)]
        bonus = _head_sum(rs[i] * k2s[i] * rk_ref[:, sl(p)], ones) * vs[i]
        o_ref[bi, :, sl(p)] = ((yn + bonus) * g_ref[bi, :, sl(p)]).astype(o_ref.dtype)


def rwkv_chunk(r, lw, k2, v, kk, b, g, r_k, gn_g, gn_b, *, nbatch=2, npairs=8):
    B, S, W = r.shape
    npairs = min(npairs, W // LANES)
    nbatch = min(nbatch, B)
    wb = npairs * LANES
    seq = pl.BlockSpec((nbatch, CHUNK, wb), lambda bi, p, c: (bi, c, p))
    par = pl.BlockSpec((1, wb), lambda bi, p, c: (0, p))
    return pl.pallas_call(
        functools.partial(_rwkv_chunk_kernel, nbatch=nbatch, npairs=npairs),
        grid=(B // nbatch, W // wb, S // CHUNK),
        in_specs=[seq] * 7 + [par] * 3,
        out_specs=seq,
        out_shape=jax.ShapeDtypeStruct((B, S, W), BF16),
        scratch_shapes=[pltpu.VMEM((nbatch * npairs, LANES, LANES), F32)],
        compiler_params=_cparams(("parallel", "parallel", "arbitrary")),
    )(r, lw, k2, v, kk, b, g, r_k, gn_g, gn_b)


def _out_ln_kernel(x_ref, oa_ref, or_ref, wa_ref, wr_ref, g_ref, b_ref, o_ref, *, alpha):
    mix = _dot(oa_ref[...], wa_ref[...]) + _dot(or_ref[...], wr_ref[...])
    o_ref[...] = _layer_norm(alpha * x_ref[...] + mix, g_ref[...], b_ref[...])


def out_ln(x, oa, orw, wa, wr, g, b, *, alpha, tm=512):
    T, D = x.shape
    tm = min(tm, T)
    ka, kr = oa.shape[1], orw.shape[1]
    return pl.pallas_call(
        functools.partial(_out_ln_kernel, alpha=alpha),
        grid=(T // tm,),
        in_specs=[
            pl.BlockSpec((tm, D), lambda i: (i, 0)),
            pl.BlockSpec((tm, ka), lambda i: (i, 0)),
            pl.BlockSpec((tm, kr), lambda i: (i, 0)),
            pl.BlockSpec((ka, D), lambda i: (0, 0)),
            pl.BlockSpec((kr, D), lambda i: (0, 0)),
            pl.BlockSpec((1, D), lambda i: (0, 0)),
            pl.BlockSpec((1, D), lambda i: (0, 0)),
        ],
        out_specs=pl.BlockSpec((tm, D), lambda i: (i, 0)),
        out_shape=jax.ShapeDtypeStruct((T, D), F32),
        compiler_params=_cparams(("parallel",)),
    )(x, oa, orw, wa, wr, g, b)


def _layer(x, l, ffn1_w_gate, ffn1_w_up, ffn1_w_down, ln1_g, ln1_b, w_in,
           lambda_q1, lambda_k1, lambda_q2, lambda_k2, attn_norm_g,
           rwkv_mu, rwkv_w0, rwkv_w2, rwkv_a0, rwkv_a2, rwkv_g2,
           rwkv_k_k, rwkv_k_a, rwkv_r_k, rwkv_gn_g, rwkv_gn_b,
           w_out, ln2_g, ln2_b, ffn2_w_gate, ffn2_w_up, ffn2_w_down, ln3_g, ln3_b):
    B, S, D = x.shape
    T = B * S
    alpha = (2.0 * DEPTH) ** 0.25
    lambda_init = 0.8 - 0.6 * math.exp(-0.3 * l)
    rw = rwkv_w0.shape[-1]
    aw = w_out.shape[1] - rw
    H = aw // LANES
    n_wd, n_ad, n_gd = rwkv_w2.shape[1], rwkv_a2.shape[1], rwkv_g2.shape[1]
    assert n_wd == HEAD and n_ad == HEAD and n_gd <= 2 * LANES
    row = lambda a: a.reshape(1, -1)
    bf = lambda a: a.astype(BF16)

    x1 = ffn_ln(x.reshape(T, D), bf(ffn1_w_gate[l]), bf(ffn1_w_up[l]), bf(ffn1_w_down[l]),
                row(ln1_g[l]), row(ln1_b[l]), alpha=alpha)

    wi = w_in[l]
    pad = 3 * LANES - (n_wd + n_ad + n_gd)
    w_rwkv = jnp.pad(wi[:, 3 * aw:], ((0, 0), (0, pad)))
    p_attn = in_proj(x1, bf(wi[:, :3 * aw]), BF16, tn=aw, n_scaled=1, scale=HEAD ** -0.5 * math.log2(math.e))
    p_rwkv = in_proj(x1, bf(w_rwkv), F32, tn=w_rwkv.shape[1] // 3)

    tab, feat = alibi_tables(H, S)
    o_attn = diff_attn(p_attn, tab, feat, row(lambda_q1[l]), row(lambda_k1[l]), row(lambda_q2[l]),
                       row(lambda_k2[l]), row(attn_norm_g[l]), B=B, S=S, H=H, lambda_init=lambda_init)

    mu = jnp.pad(rwkv_mu[l], (0, pad))
    w12 = jnp.zeros((LANES, 2 * rw), F32)
    w12 = w12.at[:HEAD, :rw].set(rwkv_w2[l]).at[HEAD:, rw:].set(rwkv_a2[l])
    g2p = jnp.pad(rwkv_g2[l], ((0, 2 * LANES - n_gd), (0, 0)))
    r, lw, k2, v, kk, b, g = rwkv_prep(
        p_rwkv.reshape(B, S, -1), row(mu), row(rwkv_w0[l]), row(rwkv_a0[l]), row(rwkv_k_k[l]),
        row(rwkv_k_a[l]), bf(w12), bf(g2p), width=rw)
    o_rwkv = rwkv_chunk(r, lw, k2, v, kk, b, g, row(rwkv_r_k[l]), row(rwkv_gn_g[l]), row(rwkv_gn_b[l]))

    wo = w_out[l]
    x2 = out_ln(x1, o_attn, o_rwkv.reshape(T, rw), bf(wo[:aw]), bf(wo[aw:]),
                row(ln2_g[l]), row(ln2_b[l]), alpha=alpha)
    x3 = ffn_ln(x2, bf(ffn2_w_gate[l]), bf(ffn2_w_up[l]), bf(ffn2_w_down[l]),
                row(ln3_g[l]), row(ln3_b[l]), alpha=alpha)
    return x3.reshape(B, S, D)


def kernel(x, ffn1_w_gate, ffn1_w_up, ffn1_w_down, ln1_g, ln1_b, w_in, lambda_q1, lambda_k1, lambda_q2, lambda_k2, attn_norm_g, rwkv_mu, rwkv_w0, rwkv_w2, rwkv_a0, rwkv_a2, rwkv_g2, rwkv_k_k, rwkv_k_a, rwkv_r_k, rwkv_gn_g, rwkv_gn_b, w_out, ln2_g, ln2_b, ffn2_w_gate, ffn2_w_up, ffn2_w_down, ln3_g, ln3_b):
    for l in range(DEPTH):
        x = _layer(x, l, ffn1_w_gate, ffn1_w_up, ffn1_w_down, ln1_g, ln1_b, w_in,
                   lambda_q1, lambda_k1, lambda_q2, lambda_k2, attn_norm_g,
                   rwkv_mu, rwkv_w0, rwkv_w2, rwkv_a0, rwkv_a2, rwkv_g2,
                   rwkv_k_k, rwkv_k_a, rwkv_r_k, rwkv_gn_g, rwkv_gn_b,
                   w_out, ln2_g, ln2_b, ffn2_w_gate, ffn2_w_up, ffn2_w_down, ln3_g, ln3_b)
    return x
```

```python
import functools
import math

import jax
import jax.numpy as jnp
from jax import lax
from jax.experimental import pallas as pl
from jax.experimental.pallas import tpu as pltpu

F32 = jnp.float32
BF16 = jnp.bfloat16

DEPTH = 1
LN_EPS = 1e-5
ATTN_NORM_EPS = 1e-5
GN_EPS = 64e-5
HEAD = 64
LANES = 128
CHUNK = 64
VMEM_LIMIT = 56 * 1024 * 1024


def _cparams(sem):
    return pltpu.CompilerParams(dimension_semantics=sem, vmem_limit_bytes=VMEM_LIMIT)


def _layer_norm(y, g, b):
    mu = jnp.mean(y, axis=-1, keepdims=True)
    d = y - mu
    var = jnp.mean(d * d, axis=-1, keepdims=True)
    return d * lax.rsqrt(var + LN_EPS) * g + b


def _dot(a, b):
    return jnp.dot(a, b, preferred_element_type=F32)


def _dot_nt(a, b):
    return lax.dot_general(a, b, (((1,), (1,)), ((), ())), preferred_element_type=F32)


def _dot_tn(a, b):
    return lax.dot_general(a, b, (((0,), (0,)), ((), ())), preferred_element_type=F32)


def _ffn_ln_kernel(x_ref, wg_ref, wu_ref, wd_ref, g_ref, b_ref, o_ref, xb_ref, *, alpha, ncol):
    j = pl.program_id(1)

    @pl.when(j == 0)
    def _():
        x = x_ref[...]
        xb_ref[...] = x.astype(BF16)
        o_ref[...] = alpha * x

    xb = xb_ref[...]
    hg = _dot(xb, wg_ref[...].astype(BF16))
    hu = _dot(xb, wu_ref[...].astype(BF16))
    h = (0.5 * hg * jax.nn.sigmoid(hg) * hu).astype(BF16)
    D = o_ref.shape[1]
    for n in range(D // ncol):
        sl = slice(n * ncol, (n + 1) * ncol)
        o_ref[:, sl] += _dot(h, wd_ref[:, sl].astype(BF16))

    @pl.when(j == pl.num_programs(1) - 1)
    def _():
        o_ref[...] = _layer_norm(o_ref[...], g_ref[...], b_ref[...])


def ffn_ln(x, wg, wu, wd, g, b, *, alpha, tm=1024, tf=256, ncol=512):
    T, D = x.shape
    Fd = wg.shape[1]
    tm = min(tm, T)
    tf = min(tf, Fd)
    ncol = min(ncol, D)
    return pl.pallas_call(
        functools.partial(_ffn_ln_kernel, alpha=alpha, ncol=ncol),
        grid=(T // tm, Fd // tf),
        in_specs=[
            pl.BlockSpec((tm, D), lambda i, j: (i, 0), pipeline_mode=pl.Buffered(1)),
            pl.BlockSpec((D, tf), lambda i, j: (0, j)),
            pl.BlockSpec((D, tf), lambda i, j: (0, j)),
            pl.BlockSpec((tf, D), lambda i, j: (j, 0)),
            pl.BlockSpec((1, D), lambda i, j: (0, 0)),
            pl.BlockSpec((1, D), lambda i, j: (0, 0)),
        ],
        out_specs=pl.BlockSpec((tm, D), lambda i, j: (i, 0)),
        out_shape=jax.ShapeDtypeStruct((T, D), F32),
        scratch_shapes=[pltpu.VMEM((tm, D), BF16)],
        compiler_params=_cparams(("parallel", "arbitrary")),
    )(x, wg, wu, wd, g, b)


def _pick_tile(limit, *sizes):
    t = limit // LANES * LANES
    while any(s % t for s in sizes):
        t -= LANES
    return t


def _stage_weight_tile(w_ref, wb_ref, s, first_tile, n_cols):
    tile = w_ref.shape[1]
    col = (first_tile + s) * tile + lax.broadcasted_iota(jnp.int32, w_ref.shape, 1)
    wb_ref[s] = jnp.where(col < n_cols, w_ref[...], 0.0).astype(BF16)


def _proj_attn_kernel(x_ref, w_ref, o_ref, wb_ref, *, n_wt, n_cols, n_scaled, scale):
    s = pl.program_id(0)

    @pl.when(s < n_wt)
    def _():
        _stage_weight_tile(w_ref, wb_ref, s, 0, n_cols)

    @pl.when(s >= n_wt)
    def _():
        tile = w_ref.shape[1]
        xb = x_ref[...].astype(BF16)
        for n in range(n_wt):
            acc = _dot(xb, wb_ref[n])
            if n < n_scaled:
                acc = acc * scale
            o_ref[:, n * tile:(n + 1) * tile] = acc.astype(o_ref.dtype)


def proj_attn(x, w, *, n_cols, n_scaled_cols, scale, tm=512, tile=512):
    T, D = x.shape
    tm = min(tm, T)
    tile = _pick_tile(tile, n_cols, n_scaled_cols)
    n_wt = n_cols // tile
    tok = lambda s: (jnp.maximum(s - n_wt, 0), 0)
    return pl.pallas_call(
        functools.partial(_proj_attn_kernel, n_wt=n_wt, n_cols=w.shape[1],
                          n_scaled=n_scaled_cols // tile, scale=scale),
        grid=(n_wt + T // tm,),
        in_specs=[
            pl.BlockSpec((tm, D), tok),
            pl.BlockSpec((D, tile), lambda s: (0, jnp.minimum(s, n_wt - 1))),
        ],
        out_specs=pl.BlockSpec((tm, n_cols), tok),
        out_shape=jax.ShapeDtypeStruct((T, n_cols), BF16),
        scratch_shapes=[pltpu.VMEM((n_wt, D, tile), BF16)],
        compiler_params=_cparams(("arbitrary",)),
    )(x, w)


ATT_TQ = 128
ATT_TK = 512
N_BIAS_TERMS = 3
BF16_ROWS = 16
VT_ROWS = LANES + BF16_ROWS


def alibi_tables(H, S):
    slope = jnp.exp2(-8.0 * (jnp.arange(H, dtype=F32) + 1.0) / H) * math.log2(math.e)
    terms, rest = [], slope
    for _ in range(N_BIAS_TERMS):
        t = rest.astype(BF16).astype(F32)
        terms += [64.0 * t, t]
        rest = rest - t
    tab = jnp.stack(terms + [jnp.zeros_like(slope)] * (8 - len(terms)), axis=1)
    pos = jnp.arange(S, dtype=jnp.int32)
    hi, lo = (pos // 64).astype(F32), (pos % 64).astype(F32)
    feat = jnp.stack([hi, lo] * N_BIAS_TERMS, axis=1)
    feat = jnp.pad(feat, ((0, 0), (0, LANES - feat.shape[1]))).astype(BF16)
    return tab, feat


def _diff_attn_kernel(tab_ref, q_ref, k_ref, v_ref, feat_ref, lq1_ref, lk1_ref, lq2_ref, lk2_ref, ng_ref,
                      o_ref, ka_ref, vt_ref, qa_ref, acc_ref, m_ref, *, hb, nck, lambda_init):
    tq = ATT_TQ
    tk = vt_ref.shape[2]
    g = pl.program_id(1)
    i = pl.program_id(2)
    heads = range(hb)
    hs = lambda h: slice(h * LANES, (h + 1) * LANES)

    @pl.when(i == 0)
    def _():
        for h in heads:
            ka_ref[h, :, 0:LANES] = k_ref[:, hs(h)]
            ka_ref[h, :, LANES:2 * LANES] = feat_ref[...]
            for c in range(nck):
                vt_ref[h * nck + c, 0:LANES, :] = v_ref[c * tk:(c + 1) * tk, hs(h)].astype(F32).T.astype(BF16)
                vt_ref[h * nck + c, LANES:VT_ROWS, :] = jnp.ones((BF16_ROWS, tk), BF16)

    lane = lax.broadcasted_iota(jnp.int32, (tq, LANES), 1)
    for h in heads:
        q = q_ref[:, hs(h)]
        zero = jnp.zeros_like(q)
        qf = jnp.zeros((tq, LANES), F32)
        for n in range(2 * N_BIAS_TERMS):
            qf = jnp.where(lane == n, tab_ref[g * hb + h, n], qf)
        qf = qf.astype(BF16)
        qa_ref[h, 0:tq, 0:LANES] = jnp.where(lane < HEAD, q, zero)
        qa_ref[h, tq:2 * tq, 0:LANES] = jnp.where(lane >= HEAD, q, zero)
        qa_ref[h, 0:tq, LANES:2 * LANES] = qf
        qa_ref[h, tq:2 * tq, LANES:2 * LANES] = qf
        m_ref[h] = jnp.full((1, 2 * tq), -1e30, F32)
        acc_ref[h] = jnp.zeros((VT_ROWS, 2 * tq), F32)

    def chunk(c, masked):
        start = pl.multiple_of(c * tk, tk)
        ss = [_dot_nt(ka_ref[h, pl.ds(start, tk), :], qa_ref[h]) for h in heads]
        if masked:
            key = start + lax.broadcasted_iota(jnp.int32, (tk, 2 * tq), 0)
            qcol = lax.broadcasted_iota(jnp.int32, (tk, 2 * tq), 1)
            qpos = i * tq + jnp.where(qcol >= tq, qcol - tq, qcol)
            keep = key <= qpos
            ss = [jnp.where(keep, s, -jnp.inf) for s in ss]
        m_olds = [m_ref[h] for h in heads]
        m_news = [jnp.maximum(m_olds[h], jnp.max(ss[h], axis=0, keepdims=True)) for h in heads]
        ps = [jnp.exp2(ss[h] - m_news[h]).astype(BF16) for h in heads]
        alphas = [jnp.exp2(m_olds[h] - m_news[h]) for h in heads]
        pvs = [_dot(vt_ref[h * nck + c], ps[h]) for h in heads]
        for h in heads:
            m_ref[h] = m_news[h]
            acc_ref[h] = alphas[h] * acc_ref[h] + pvs[h]

    n_full = i // (tk // tq)

    def body(c, carry):
        chunk(c, False)
        return carry

    lax.fori_loop(0, n_full, body, 0)
    chunk(n_full, True)

    lam = (jnp.exp(jnp.sum(lq1_ref[...] * lk1_ref[...], axis=-1, keepdims=True))
           - jnp.exp(jnp.sum(lq2_ref[...] * lk2_ref[...], axis=-1, keepdims=True)) + lambda_init)
    for h in heads:
        acc = acc_ref[h]
        rl = 1.0 / acc[LANES:LANES + 1, :]
        ot = (acc[0:LANES, 0:tq] * rl[:, 0:tq]
              - lam * (acc[0:LANES, tq:2 * tq] * rl[:, tq:2 * tq]))
        o = ot.T
        o = o * lax.rsqrt(jnp.mean(o * o, axis=-1, keepdims=True) + ATTN_NORM_EPS) * ng_ref[...]
        o_ref[:, hs(h)] = (o * (1.0 - lambda_init)).astype(o_ref.dtype)


def diff_attn(pa, tab, feat, lq1, lk1, lq2, lk2, norm_g, *, B, S, H, lambda_init, hb=8):
    hb = min(hb, H)
    tq, tk = ATT_TQ, min(ATT_TK, S)
    assert S % tk == 0 and H % hb == 0 and tk % tq == 0
    nq, nck, ng = S // tq, S // tk, H // hb
    wb = hb * LANES
    small = lambda n: pl.BlockSpec((1, n), lambda b, g, i: (0, 0))
    return pl.pallas_call(
        functools.partial(_diff_attn_kernel, hb=hb, nck=nck, lambda_init=lambda_init),
        grid=(B, ng, nq),
        in_specs=[
            pl.BlockSpec(memory_space=pltpu.SMEM),
            pl.BlockSpec((tq, wb), lambda b, g, i: (b * nq + i, g)),
            pl.BlockSpec((S, wb), lambda b, g, i: (b, ng + g)),
            pl.BlockSpec((S, wb), lambda b, g, i: (b, 2 * ng + g)),
            pl.BlockSpec((S, LANES), lambda b, g, i: (0, 0)),
            small(HEAD), small(HEAD), small(HEAD), small(HEAD), small(2 * HEAD),
        ],
        out_specs=pl.BlockSpec((tq, wb), lambda b, g, i: (b * nq + i, g)),
        out_shape=jax.ShapeDtypeStruct((B * S, H * LANES), BF16),
        scratch_shapes=[
            pltpu.VMEM((hb, S, 2 * LANES), BF16),
            pltpu.VMEM((hb * nck, VT_ROWS, tk), BF16),
            pltpu.VMEM((hb, 2 * tq, 2 * LANES), BF16),
            pltpu.VMEM((hb, VT_ROWS, 2 * tq), F32),
            pltpu.VMEM((hb, 1, 2 * tq), F32),
        ],
        compiler_params=_cparams(("parallel", "parallel", "arbitrary")),
    )(tab, pa, pa, pa, feat, lq1, lk1, lq2, lk2, norm_g)


def _pair_ones():
    r = lax.broadcasted_iota(jnp.int32, (LANES, LANES), 0)
    c = lax.broadcasted_iota(jnp.int32, (LANES, LANES), 1)
    return jnp.where((r < HEAD) == (c < HEAD), 1.0, 0.0).astype(BF16)


def _head_sum(x, ones):
    hi = x.astype(BF16)
    lo = (x - hi.astype(F32)).astype(BF16)
    return _dot(hi, ones) + _dot(lo, ones)


RWKV_TAIL = 3 * LANES


def _proj_rwkv_kernel(x_ref, w_ref, mu_ref, w0_ref, a0_ref, kk_ref, ka_ref, w12_ref, g2_ref,
                      r_ref, lw_ref, k2_ref, v_ref, kkn_ref, b_ref, g_ref, wb_ref, pr_ref, prev_ref,
                      *, n_wt, first_tile, n_cols, width, tiles_per_seq):
    s = pl.program_id(0)

    @pl.when(s == 0)
    def _():
        prev_ref[...] = jnp.zeros_like(prev_ref)

    @pl.when(s < n_wt)
    def _():
        _stage_weight_tile(w_ref, wb_ref, s, first_tile, n_cols)

    @pl.when(s >= n_wt)
    def _():
        tile = w_ref.shape[1]
        tm = x_ref.shape[0]
        xb = x_ref[...].astype(BF16)
        for n in range(n_wt):
            pr_ref[:, n * tile:(n + 1) * tile] = _dot(xb, wb_ref[n])

        first = (s - n_wt) % tiles_per_seq == 0
        row = lax.broadcasted_iota(jnp.int32, (tm, 1), 0)

        def mixed(c0, c1):
            x = pr_ref[:, c0:c1]
            prow = jnp.where(first, 0.0, prev_ref[0:1, c0:c1])
            xs = jnp.where(row == 0, prow, pltpu.roll(x, 1, axis=0))
            return x + mu_ref[:, c0:c1] * (xs - x)

        c3 = 3 * width
        lora_in = mixed(c3, c3 + LANES)
        lane = lax.broadcasted_iota(jnp.int32, lora_in.shape, 1)
        lora_in = jnp.where(lane < HEAD, jnp.tanh(lora_in), lora_in)
        z = _dot(lora_in.astype(BF16), w12_ref[...])
        g_ref[...] = _dot(jax.nn.sigmoid(mixed(c3 + LANES, c3 + RWKV_TAIL)).astype(BF16), g2_ref[...])

        lw_ref[...] = -math.exp(-0.5) * jax.nn.sigmoid(w0_ref[...] + z[:, 0:width])
        a = jax.nn.sigmoid(a0_ref[...] + z[:, width:2 * width])

        r_ref[...] = mixed(0, width)
        v_ref[...] = mixed(2 * width, c3)
        k = mixed(width, 2 * width)
        k2_ref[...] = k * (1.0 + (a - 1.0) * ka_ref[...])

        kx = k * kk_ref[...]
        ones = _pair_ones()
        ss = jnp.concatenate(
            [_head_sum(kx[:, t * LANES:(t + 1) * LANES] * kx[:, t * LANES:(t + 1) * LANES], ones)
             for t in range(width // LANES)], axis=1)
        kkn = kx / jnp.maximum(jnp.sqrt(ss), 1e-12)
        kkn_ref[...] = kkn
        b_ref[...] = kkn * a
        prev_ref[0:1, :] = pr_ref[tm - 1:tm, :]


def proj_rwkv(x, w, mu, w0, a0, k_k, k_a, w12, g2p, *, S, col0, width, tm=256, tile=384):
    T, D = x.shape
    tm = min(tm, S)
    W = 3 * width + RWKV_TAIL
    tile = _pick_tile(tile, col0, W)
    assert S % tm == 0
    n_wt, first_tile = W // tile, col0 // tile
    tok = lambda s: (jnp.maximum(s - n_wt, 0), 0)
    row = lambda n: pl.BlockSpec((1, n), lambda s: (0, 0))
    out = pl.BlockSpec((tm, width), tok)
    return pl.pallas_call(
        functools.partial(_proj_rwkv_kernel, n_wt=n_wt, first_tile=first_tile, n_cols=w.shape[1],
                          width=width, tiles_per_seq=S // tm),
        grid=(n_wt + T // tm,),
        in_specs=[
            pl.BlockSpec((tm, D), tok),
            pl.BlockSpec((D, tile), lambda s: (0, first_tile + jnp.minimum(s, n_wt - 1))),
            row(W), row(width), row(width), row(width), row(width),
            pl.BlockSpec((LANES, 2 * width), lambda s: (0, 0)),
            pl.BlockSpec((2 * LANES, width), lambda s: (0, 0)),
        ],
        out_specs=[out] * 7,
        out_shape=[jax.ShapeDtypeStruct((T, width), F32)] * 7,
        scratch_shapes=[pltpu.VMEM((n_wt, D, tile), BF16), pltpu.VMEM((tm, W), F32), pltpu.VMEM((8, W), F32)],
        compiler_params=_cparams(("arbitrary",)),
    )(x, w, mu, w0, a0, k_k, k_a, w12, g2p)


def _block_diag(x):
    lane = lax.broadcasted_iota(jnp.int32, x.shape, 1)
    zero = jnp.zeros_like(x)
    return jnp.concatenate([jnp.where(lane < HEAD, x, zero), jnp.where(lane >= HEAD, x, zero)], axis=0)


def _chunk_step(rs, lws, k2s, vs, kks, bs, sts):
    C = CHUNK
    n = len(rs)
    idx = range(n)
    row = lax.broadcasted_iota(jnp.int32, (C, LANES), 0)
    lane = lax.broadcasted_iota(jnp.int32, (C, LANES), 1)
    tcol = jnp.where(lane >= HEAD, lane - HEAD, lane)
    strict = row > tcol
    incl = row >= tcol
    eye = jnp.where(row == tcol, 1.0, 0.0)
    bdb = lambda x: _block_diag(x).astype(BF16)

    cums = list(lws)
    sh = 1
    while sh < C:
        cums = [cm + jnp.where(row >= sh, pltpu.roll(cm, sh, axis=0), 0.0) for cm in cums]
        sh *= 2
    tots = [cm[C - 1:C, :] for cm in cums]
    e_negs = [jnp.exp(-cm) for cm in cums]
    ARs = [jnp.concatenate([-kks[i] * jnp.exp(cums[i] - lws[i]), rs[i] * jnp.exp(cums[i])], axis=0).astype(BF16)
           for i in idx]
    Mbs = [_dot_nt(ARs[i], bdb(bs[i] * e_negs[i])) for i in idx]
    Mks = [_dot_nt(ARs[i], bdb(k2s[i] * e_negs[i])) for i in idx]
    ARSs = [_dot_nt(ARs[i], sts[i].astype(BF16)) for i in idx]
    vbds = [bdb(v) for v in vs]

    Ls = [jnp.where(strict, Mbs[i][0:C], 0.0) for i in idx]
    Ps = [eye + L for L in Ls]
    Lps = [_dot(L.astype(BF16), bdb(L)) for L in Ls]
    Xs = [ARSs[i][0:C] + _dot(jnp.where(strict, Mks[i][0:C], 0.0).astype(BF16), vbds[i]) for i in idx]
    m = 2
    while 2 * m < C:
        both = [_dot(jnp.concatenate([Lps[i], Ps[i]], axis=0).astype(BF16), bdb(Lps[i])) for i in idx]
        Ps = [Ps[i] + both[i][C:2 * C] for i in idx]
        Lps = [bt[0:C] for bt in both]
        m *= 2
    Ps = [Ps[i] + _dot(Ps[i].astype(BF16), bdb(Lps[i])) for i in idx]

    Us = [_dot(Ps[i].astype(BF16), bdb(Xs[i])) for i in idx]
    Ys = [ARSs[i][C:2 * C]
          + _dot(jnp.concatenate([jnp.where(incl, Mbs[i][C:2 * C], 0.0),
                                  jnp.where(incl, Mks[i][C:2 * C], 0.0)], axis=1).astype(BF16),
                 jnp.concatenate([bdb(Us[i]), vbds[i]], axis=0)) for i in idx]
    r2 = lax.broadcasted_iota(jnp.int32, (LANES, LANES), 0)
    c2 = lax.broadcasted_iota(jnp.int32, (LANES, LANES), 1)
    same_head = (r2 < HEAD) == (c2 < HEAD)
    st_news = []
    for i in idx:
        e_rem = jnp.exp(tots[i] - cums[i])
        UV = jnp.concatenate([Us[i], vs[i]], axis=0).astype(BF16)
        BK = jnp.concatenate([bs[i] * e_rem, k2s[i] * e_rem], axis=0).astype(BF16)
        st_news.append(jnp.where(same_head, sts[i] * jnp.exp(tots[i]) + _dot_tn(UV, BK), 0.0))
    return Ys, st_news


def _rwkv_chunk_kernel(r_ref, lw_ref, k2_ref, v_ref, kk_ref, b_ref, g_ref, rk_ref, gng_ref, gnb_ref,
                       o_ref, st_ref, *, nbatch, npairs):
    c = pl.program_id(2)

    @pl.when(c == 0)
    def _():
        st_ref[...] = jnp.zeros_like(st_ref)

    ones = _pair_ones()
    chains = [(bi, p) for bi in range(nbatch) for p in range(npairs)]
    sl = lambda p: slice(p * LANES, (p + 1) * LANES)
    get = lambda ref: [ref[bi, :, sl(p)] for bi, p in chains]
    rs, k2s, vs = get(r_ref), get(k2_ref), get(v_ref)
    ys, st_news = _chunk_step(rs, get(lw_ref), k2s, vs, get(kk_ref), get(b_ref),
                              [st_ref[i] for i in range(len(chains))])
    for i in range(len(chains)):
        st_ref[i] = st_news[i]
    ds = [y - _head_sum(y, ones) * (1.0 / HEAD) for y in ys]
    vars_ = [_head_sum(d * d, ones) * (1.0 / HEAD) for d in ds]
    for i, (bi, p) in enumerate(chains):
        yn = ds[i] * lax.rsqrt(vars_[i] + GN_EPS) * gng_ref[:, sl(p)] + gnb_ref[:, sl(p)]
        bonus = _head_sum(rs[i] * k2s[i] * rk_ref[:, sl(p)], ones) * vs[i]
        o_ref[bi, :, sl(p)] = ((yn + bonus) * g_ref[bi, :, sl(p)]).astype(o_ref.dtype)


def rwkv_chunk(r, lw, k2, v, kk, b, g, r_k, gn_g, gn_b, *, nbatch=2, npairs=8):
    B, S, W = r.shape
    npairs = min(npairs, W // LANES)
    nbatch = min(nbatch, B)
    wb = npairs * LANES
    seq = pl.BlockSpec((nbatch, CHUNK, wb), lambda bi, p, c: (bi, c, p))
    par = pl.BlockSpec((1, wb), lambda bi, p, c: (0, p))
    return pl.pallas_call(
        functools.partial(_rwkv_chunk_kernel, nbatch=nbatch, npairs=npairs),
        grid=(B // nbatch, W // wb, S // CHUNK),
        in_specs=[seq] * 7 + [par] * 3,
        out_specs=seq,
        out_shape=jax.ShapeDtypeStruct((B, S, W), BF16),
        scratch_shapes=[pltpu.VMEM((nbatch * npairs, LANES, LANES), F32)],
        compiler_params=_cparams(("parallel", "parallel", "arbitrary")),
    )(r, lw, k2, v, kk, b, g, r_k, gn_g, gn_b)


def _out_ln_kernel(x_ref, oa_ref, or_ref, wa_ref, wr_ref, g_ref, b_ref, o_ref, *, alpha):
    mix = _dot(oa_ref[...], wa_ref[...]) + _dot(or_ref[...], wr_ref[...])
    o_ref[...] = _layer_norm(alpha * x_ref[...] + mix, g_ref[...], b_ref[...])


def out_ln(x, oa, orw, wa, wr, g, b, *, alpha, tm=512):
    T, D = x.shape
    tm = min(tm, T)
    ka, kr = oa.shape[1], orw.shape[1]
    return pl.pallas_call(
        functools.partial(_out_ln_kernel, alpha=alpha),
        grid=(T // tm,),
        in_specs=[
            pl.BlockSpec((tm, D), lambda i: (i, 0)),
            pl.BlockSpec((tm, ka), lambda i: (i, 0)),
            pl.BlockSpec((tm, kr), lambda i: (i, 0)),
            pl.BlockSpec((ka, D), lambda i: (0, 0)),
            pl.BlockSpec((kr, D), lambda i: (0, 0)),
            pl.BlockSpec((1, D), lambda i: (0, 0)),
            pl.BlockSpec((1, D), lambda i: (0, 0)),
        ],
        out_specs=pl.BlockSpec((tm, D), lambda i: (i, 0)),
        out_shape=jax.ShapeDtypeStruct((T, D), F32),
        compiler_params=_cparams(("parallel",)),
    )(x, oa, orw, wa, wr, g, b)


def _layer(x, l, ffn1_w_gate, ffn1_w_up, ffn1_w_down, ln1_g, ln1_b, w_in,
           lambda_q1, lambda_k1, lambda_q2, lambda_k2, attn_norm_g,
           rwkv_mu, rwkv_w0, rwkv_w2, rwkv_a0, rwkv_a2, rwkv_g2,
           rwkv_k_k, rwkv_k_a, rwkv_r_k, rwkv_gn_g, rwkv_gn_b,
           w_out, ln2_g, ln2_b, ffn2_w_gate, ffn2_w_up, ffn2_w_down, ln3_g, ln3_b):
    B, S, D = x.shape
    T = B * S
    alpha = (2.0 * DEPTH) ** 0.25
    lambda_init = 0.8 - 0.6 * math.exp(-0.3 * l)
    rw = rwkv_w0.shape[-1]
    aw = w_out.shape[1] - rw
    H = aw // LANES
    n_wd, n_ad, n_gd = rwkv_w2.shape[1], rwkv_a2.shape[1], rwkv_g2.shape[1]
    assert n_wd == HEAD and n_ad == HEAD and n_gd <= 2 * LANES
    row = lambda a: a.reshape(1, -1)
    bf = lambda a: a.astype(BF16)

    x1 = ffn_ln(x.reshape(T, D), ffn1_w_gate[l], ffn1_w_up[l], ffn1_w_down[l],
                row(ln1_g[l]), row(ln1_b[l]), alpha=alpha)

    wi = w_in[l]
    p_attn = proj_attn(x1, wi, n_cols=3 * aw, n_scaled_cols=aw, scale=HEAD ** -0.5 * math.log2(math.e))
    tab, feat = alibi_tables(H, S)
    o_attn = diff_attn(p_attn, tab, feat, row(lambda_q1[l]), row(lambda_k1[l]), row(lambda_q2[l]),
                       row(lambda_k2[l]), row(attn_norm_g[l]), B=B, S=S, H=H, lambda_init=lambda_init)

    mu = jnp.pad(rwkv_mu[l], (0, RWKV_TAIL - (n_wd + n_ad + n_gd)))
    w12 = jnp.zeros((LANES, 2 * rw), F32)
    w12 = w12.at[:HEAD, :rw].set(rwkv_w2[l]).at[HEAD:, rw:].set(rwkv_a2[l])
    g2p = jnp.pad(rwkv_g2[l], ((0, 2 * LANES - n_gd), (0, 0)))
    seqs = proj_rwkv(x1, wi, row(mu), row(rwkv_w0[l]), row(rwkv_a0[l]), row(rwkv_k_k[l]), row(rwkv_k_a[l]),
                     bf(w12), bf(g2p), S=S, col0=3 * aw, width=rw)
    r, lw, k2, v, kk, b, g = [t.reshape(B, S, rw) for t in seqs]
    o_rwkv = rwkv_chunk(r, lw, k2, v, kk, b, g, row(rwkv_r_k[l]), row(rwkv_gn_g[l]), row(rwkv_gn_b[l]))

    wo = w_out[l]
    x2 = out_ln(x1, o_attn, o_rwkv.reshape(T, rw), bf(wo[:aw]), bf(wo[aw:]),
                row(ln2_g[l]), row(ln2_b[l]), alpha=alpha)
    x3 = ffn_ln(x2, ffn2_w_gate[l], ffn2_w_up[l], ffn2_w_down[l],
                row(ln3_g[l]), row(ln3_b[l]), alpha=alpha)
    return x3.reshape(B, S, D)


def kernel(x, ffn1_w_gate, ffn1_w_up, ffn1_w_down, ln1_g, ln1_b, w_in, lambda_q1, lambda_k1, lambda_q2, lambda_k2, attn_norm_g, rwkv_mu, rwkv_w0, rwkv_w2, rwkv_a0, rwkv_a2, rwkv_g2, rwkv_k_k, rwkv_k_a, rwkv_r_k, rwkv_gn_g, rwkv_gn_b, w_out, ln2_g, ln2_b, ffn2_w_gate, ffn2_w_up, ffn2_w_down, ln3_g, ln3_b):
    for l in range(DEPTH):
        x = _layer(x, l, ffn1_w_gate, ffn1_w_up, ffn1_w_down, ln1_g, ln1_b, w_in,
                   lambda_q1, lambda_k1, lambda_q2, lambda_k2, attn_norm_g,
                   rwkv_mu, rwkv_w0, rwkv_w2, rwkv_a0, rwkv_a2, rwkv_g2,
                   rwkv_k_k, rwkv_k_a, rwkv_r_k, rwkv_gn_g, rwkv_gn_b,
                   w_out, ln2_g, ln2_b, ffn2_w_gate, ffn2_w_up, ffn2_w_down, ln3_g, ln3_b)
    return x
```

```python
import functools
import math

import jax
import jax.numpy as jnp
from jax import lax
from jax.experimental import pallas as pl
from jax.experimental.pallas import tpu as pltpu

F32 = jnp.float32
BF16 = jnp.bfloat16

DEPTH = 1
LN_EPS = 1e-5
ATTN_NORM_EPS = 1e-5
GN_EPS = 64e-5
HEAD = 64
LANES = 128
CHUNK = 64
VMEM_LIMIT = 56 * 1024 * 1024


def _cparams(sem):
    return pltpu.CompilerParams(dimension_semantics=sem, vmem_limit_bytes=VMEM_LIMIT)


def _layer_norm(y, g, b):
    mu = jnp.mean(y, axis=-1, keepdims=True)
    d = y - mu
    var = jnp.mean(d * d, axis=-1, keepdims=True)
    return d * lax.rsqrt(var + LN_EPS) * g + b


def _dot(a, b):
    return jnp.dot(a, b, preferred_element_type=F32)


def _dot_nt(a, b):
    return lax.dot_general(a, b, (((1,), (1,)), ((), ())), preferred_element_type=F32)


def _dot_tn(a, b):
    return lax.dot_general(a, b, (((0,), (0,)), ((), ())), preferred_element_type=F32)


def _ffn_ln_kernel(x_ref, wg_ref, wu_ref, wd_ref, g_ref, b_ref, o_ref, xb_ref, *, alpha, ncol):
    j = pl.program_id(1)

    @pl.when(j == 0)
    def _():
        x = x_ref[...]
        xb_ref[...] = x.astype(BF16)
        o_ref[...] = alpha * x

    xb = xb_ref[...]
    hg = _dot(xb, wg_ref[...].astype(BF16))
    hu = _dot(xb, wu_ref[...].astype(BF16))
    h = (0.5 * hg * jax.nn.sigmoid(hg) * hu).astype(BF16)
    D = o_ref.shape[1]
    for n in range(D // ncol):
        sl = slice(n * ncol, (n + 1) * ncol)
        o_ref[:, sl] += _dot(h, wd_ref[:, sl].astype(BF16))

    @pl.when(j == pl.num_programs(1) - 1)
    def _():
        o_ref[...] = _layer_norm(o_ref[...], g_ref[...], b_ref[...])


def ffn_ln(x, wg, wu, wd, g, b, *, alpha, tm=1024, tf=256, ncol=512):
    T, D = x.shape
    Fd = wg.shape[1]
    tm = min(tm, T)
    tf = min(tf, Fd)
    ncol = min(ncol, D)
    return pl.pallas_call(
        functools.partial(_ffn_ln_kernel, alpha=alpha, ncol=ncol),
        grid=(T // tm, Fd // tf),
        in_specs=[
            pl.BlockSpec((tm, D), lambda i, j: (i, 0)),
            pl.BlockSpec((D, tf), lambda i, j: (0, j)),
            pl.BlockSpec((D, tf), lambda i, j: (0, j)),
            pl.BlockSpec((tf, D), lambda i, j: (j, 0)),
            pl.BlockSpec((1, D), lambda i, j: (0, 0)),
            pl.BlockSpec((1, D), lambda i, j: (0, 0)),
        ],
        out_specs=pl.BlockSpec((tm, D), lambda i, j: (i, 0)),
        out_shape=jax.ShapeDtypeStruct((T, D), F32),
        scratch_shapes=[pltpu.VMEM((tm, D), BF16)],
        compiler_params=_cparams(("parallel", "arbitrary")),
    )(x, wg, wu, wd, g, b)


def _pick_tile(limit, *sizes):
    t = limit // LANES * LANES
    while any(s % t for s in sizes):
        t -= LANES
    return t


def _stage_weight_tile(wt_ref, wb_ref, s, first_tile, n_cols):
    tile = wt_ref.shape[0]
    col = (first_tile + s) * tile + lax.broadcasted_iota(jnp.int32, wt_ref.shape, 0)
    wb_ref[s] = jnp.where(col < n_cols, wt_ref[...], 0.0).astype(BF16)


def _proj_attn_kernel(x_ref, wt_ref, o_ref, wb_ref, *, n_wt, n_cols, n_scaled, scale):
    s = pl.program_id(0)

    @pl.when(s < n_wt)
    def _():
        _stage_weight_tile(wt_ref, wb_ref, s, 0, n_cols)

    @pl.when(s >= n_wt)
    def _():
        tile = wt_ref.shape[0]
        xb = x_ref[...].astype(BF16)
        for n in range(n_wt):
            acc = _dot_nt(xb, wb_ref[n])
            if n < n_scaled:
                acc = acc * scale
            o_ref[:, n * tile:(n + 1) * tile] = acc.astype(o_ref.dtype)


def proj_attn(x, wt, *, n_cols, n_scaled_cols, scale, tm=512, tile=512):
    T, D = x.shape
    tm = min(tm, T)
    tile = _pick_tile(tile, n_cols, n_scaled_cols)
    n_wt = n_cols // tile
    tok = lambda s: (jnp.maximum(s - n_wt, 0), 0)
    return pl.pallas_call(
        functools.partial(_proj_attn_kernel, n_wt=n_wt, n_cols=wt.shape[0],
                          n_scaled=n_scaled_cols // tile, scale=scale),
        grid=(n_wt + T // tm,),
        in_specs=[
            pl.BlockSpec((tm, D), tok),
            pl.BlockSpec((tile, D), lambda s: (jnp.minimum(s, n_wt - 1), 0)),
        ],
        out_specs=pl.BlockSpec((tm, n_cols), tok),
        out_shape=jax.ShapeDtypeStruct((T, n_cols), BF16),
        scratch_shapes=[pltpu.VMEM((n_wt, tile, D), BF16)],
        compiler_params=_cparams(("arbitrary",)),
    )(x, wt)


ATT_TQ = 128
ATT_TK = 512
N_BIAS_TERMS = 3
BF16_ROWS = 16
VT_ROWS = LANES + BF16_ROWS


def alibi_tables(H, S):
    slope = jnp.exp2(-8.0 * (jnp.arange(H, dtype=F32) + 1.0) / H) * math.log2(math.e)
    terms, rest = [], slope
    for _ in range(N_BIAS_TERMS):
        t = rest.astype(BF16).astype(F32)
        terms += [64.0 * t, t]
        rest = rest - t
    tab = jnp.stack(terms + [jnp.zeros_like(slope)] * (8 - len(terms)), axis=1)
    pos = jnp.arange(S, dtype=jnp.int32)
    hi, lo = (pos // 64).astype(F32), (pos % 64).astype(F32)
    feat = jnp.stack([hi, lo] * N_BIAS_TERMS, axis=1)
    feat = jnp.pad(feat, ((0, 0), (0, LANES - feat.shape[1]))).astype(BF16)
    return tab, feat


def _diff_attn_kernel(tab_ref, q_ref, k_ref, v_ref, feat_ref, lq1_ref, lk1_ref, lq2_ref, lk2_ref, ng_ref,
                      o_ref, ka_ref, vt_ref, qa_ref, acc_ref, m_ref, *, hb, nck, lambda_init):
    tq = ATT_TQ
    tk = vt_ref.shape[2]
    g = pl.program_id(1)
    i = pl.program_id(2)
    heads = range(hb)
    hs = lambda h: slice(h * LANES, (h + 1) * LANES)

    @pl.when(i == 0)
    def _():
        for h in heads:
            ka_ref[h, :, 0:LANES] = k_ref[:, hs(h)]
            ka_ref[h, :, LANES:2 * LANES] = feat_ref[...]
            for c in range(nck):
                vt_ref[h * nck + c, 0:LANES, :] = v_ref[c * tk:(c + 1) * tk, hs(h)].astype(F32).T.astype(BF16)
                vt_ref[h * nck + c, LANES:VT_ROWS, :] = jnp.ones((BF16_ROWS, tk), BF16)

    lane = lax.broadcasted_iota(jnp.int32, (tq, LANES), 1)
    for h in heads:
        q = q_ref[:, hs(h)]
        zero = jnp.zeros_like(q)
        qf = jnp.zeros((tq, LANES), F32)
        for n in range(2 * N_BIAS_TERMS):
            qf = jnp.where(lane == n, tab_ref[g * hb + h, n], qf)
        qf = qf.astype(BF16)
        qa_ref[h, 0:tq, 0:LANES] = jnp.where(lane < HEAD, q, zero)
        qa_ref[h, tq:2 * tq, 0:LANES] = jnp.where(lane >= HEAD, q, zero)
        qa_ref[h, 0:tq, LANES:2 * LANES] = qf
        qa_ref[h, tq:2 * tq, LANES:2 * LANES] = qf
        m_ref[h] = jnp.full((1, 2 * tq), -1e30, F32)
        acc_ref[h] = jnp.zeros((VT_ROWS, 2 * tq), F32)

    def chunk(c, masked):
        start = pl.multiple_of(c * tk, tk)
        ss = [_dot_nt(ka_ref[h, pl.ds(start, tk), :], qa_ref[h]) for h in heads]
        if masked:
            key = start + lax.broadcasted_iota(jnp.int32, (tk, 2 * tq), 0)
            qcol = lax.broadcasted_iota(jnp.int32, (tk, 2 * tq), 1)
            qpos = i * tq + jnp.where(qcol >= tq, qcol - tq, qcol)
            keep = key <= qpos
            ss = [jnp.where(keep, s, -jnp.inf) for s in ss]
        m_olds = [m_ref[h] for h in heads]
        m_news = [jnp.maximum(m_olds[h], jnp.max(ss[h], axis=0, keepdims=True)) for h in heads]
        ps = [jnp.exp2(ss[h] - m_news[h]).astype(BF16) for h in heads]
        alphas = [jnp.exp2(m_olds[h] - m_news[h]) for h in heads]
        pvs = [_dot(vt_ref[h * nck + c], ps[h]) for h in heads]
        for h in heads:
            m_ref[h] = m_news[h]
            acc_ref[h] = alphas[h] * acc_ref[h] + pvs[h]

    n_full = i // (tk // tq)

    def body(c, carry):
        chunk(c, False)
        return carry

    lax.fori_loop(0, n_full, body, 0)
    chunk(n_full, True)

    lam = (jnp.exp(jnp.sum(lq1_ref[...] * lk1_ref[...], axis=-1, keepdims=True))
           - jnp.exp(jnp.sum(lq2_ref[...] * lk2_ref[...], axis=-1, keepdims=True)) + lambda_init)
    for h in heads:
        acc = acc_ref[h]
        rl = 1.0 / acc[LANES:LANES + 1, :]
        ot = (acc[0:LANES, 0:tq] * rl[:, 0:tq]
              - lam * (acc[0:LANES, tq:2 * tq] * rl[:, tq:2 * tq]))
        o = ot.T
        o = o * lax.rsqrt(jnp.mean(o * o, axis=-1, keepdims=True) + ATTN_NORM_EPS) * ng_ref[...]
        o_ref[:, hs(h)] = (o * (1.0 - lambda_init)).astype(o_ref.dtype)


def diff_attn(pa, tab, feat, lq1, lk1, lq2, lk2, norm_g, *, B, S, H, lambda_init, hb=8):
    hb = min(hb, H)
    tq, tk = ATT_TQ, min(ATT_TK, S)
    assert S % tk == 0 and H % hb == 0 and tk % tq == 0
    nq, nck, ng = S // tq, S // tk, H // hb
    wb = hb * LANES
    small = lambda n: pl.BlockSpec((1, n), lambda b, g, i: (0, 0))
    return pl.pallas_call(
        functools.partial(_diff_attn_kernel, hb=hb, nck=nck, lambda_init=lambda_init),
        grid=(B, ng, nq),
        in_specs=[
            pl.BlockSpec(memory_space=pltpu.SMEM),
            pl.BlockSpec((tq, wb), lambda b, g, i: (b * nq + i, g)),
            pl.BlockSpec((S, wb), lambda b, g, i: (b, ng + g)),
            pl.BlockSpec((S, wb), lambda b, g, i: (b, 2 * ng + g)),
            pl.BlockSpec((S, LANES), lambda b, g, i: (0, 0)),
            small(HEAD), small(HEAD), small(HEAD), small(HEAD), small(2 * HEAD),
        ],
        out_specs=pl.BlockSpec((tq, wb), lambda b, g, i: (b * nq + i, g)),
        out_shape=jax.ShapeDtypeStruct((B * S, H * LANES), BF16),
        scratch_shapes=[
            pltpu.VMEM((hb, S, 2 * LANES), BF16),
            pltpu.VMEM((hb * nck, VT_ROWS, tk), BF16),
            pltpu.VMEM((hb, 2 * tq, 2 * LANES), BF16),
            pltpu.VMEM((hb, VT_ROWS, 2 * tq), F32),
            pltpu.VMEM((hb, 1, 2 * tq), F32),
        ],
        compiler_params=_cparams(("parallel", "parallel", "arbitrary")),
    )(tab, pa, pa, pa, feat, lq1, lk1, lq2, lk2, norm_g)


def _same_head(n):
    r = lax.broadcasted_iota(jnp.int32, (n, n), 0) // HEAD
    c = lax.broadcasted_iota(jnp.int32, (n, n), 1) // HEAD
    return r == c


def _head_ones(n):
    return jnp.where(_same_head(n), 1.0, 0.0).astype(BF16)


def _head_sum(x, ones):
    hi = x.astype(BF16)
    lo = (x - hi.astype(F32)).astype(BF16)
    return _dot(hi, ones) + _dot(lo, ones)


RWKV_TAIL = 3 * LANES


def _proj_rwkv_kernel(x_ref, wt_ref, mu_ref, w0_ref, a0_ref, kk_ref, ka_ref, w12_ref, g2_ref,
                      r_ref, lw_ref, k2_ref, v_ref, kkn_ref, b_ref, g_ref, wb_ref, pr_ref, prev_ref,
                      *, n_wt, first_tile, n_cols, width, tiles_per_seq):
    s = pl.program_id(0)

    @pl.when(s == 0)
    def _():
        prev_ref[...] = jnp.zeros_like(prev_ref)

    @pl.when(s < n_wt)
    def _():
        _stage_weight_tile(wt_ref, wb_ref, s, first_tile, n_cols)

    @pl.when(s >= n_wt)
    def _():
        tile = wt_ref.shape[0]
        tm = x_ref.shape[0]
        xb = x_ref[...].astype(BF16)
        for n in range(n_wt):
            pr_ref[:, n * tile:(n + 1) * tile] = _dot_nt(xb, wb_ref[n])

        first = (s - n_wt) % tiles_per_seq == 0
        row = lax.broadcasted_iota(jnp.int32, (tm, 1), 0)

        def mixed(c0, c1):
            x = pr_ref[:, c0:c1]
            prow = jnp.where(first, 0.0, prev_ref[0:1, c0:c1])
            xs = jnp.where(row == 0, prow, pltpu.roll(x, 1, axis=0))
            return x + mu_ref[:, c0:c1] * (xs - x)

        c3 = 3 * width
        lora_in = mixed(c3, c3 + LANES)
        lane = lax.broadcasted_iota(jnp.int32, lora_in.shape, 1)
        lora_in = jnp.where(lane < HEAD, jnp.tanh(lora_in), lora_in)
        z = _dot(lora_in.astype(BF16), w12_ref[...])
        g_ref[...] = _dot(jax.nn.sigmoid(mixed(c3 + LANES, c3 + RWKV_TAIL)).astype(BF16), g2_ref[...])

        lw_ref[...] = -math.exp(-0.5) * jax.nn.sigmoid(w0_ref[...] + z[:, 0:width])
        a = jax.nn.sigmoid(a0_ref[...] + z[:, width:2 * width])

        r_ref[...] = mixed(0, width)
        v_ref[...] = mixed(2 * width, c3)
        k = mixed(width, 2 * width)
        k2_ref[...] = k * (1.0 + (a - 1.0) * ka_ref[...])

        kx = k * kk_ref[...]
        ones = _head_ones(LANES)
        ss = jnp.concatenate(
            [_head_sum(kx[:, t * LANES:(t + 1) * LANES] * kx[:, t * LANES:(t + 1) * LANES], ones)
             for t in range(width // LANES)], axis=1)
        kkn = kx / jnp.maximum(jnp.sqrt(ss), 1e-12)
        kkn_ref[...] = kkn
        b_ref[...] = kkn * a
        prev_ref[0:1, :] = pr_ref[tm - 1:tm, :]


def proj_rwkv(x, wt, mu, w0, a0, k_k, k_a, w12, g2p, *, S, col0, width, tm=256, tile=384):
    T, D = x.shape
    tm = min(tm, S)
    W = 3 * width + RWKV_TAIL
    tile = _pick_tile(tile, col0, W)
    assert S % tm == 0
    n_wt, first_tile = W // tile, col0 // tile
    tok = lambda s: (jnp.maximum(s - n_wt, 0), 0)
    row = lambda n: pl.BlockSpec((1, n), lambda s: (0, 0))
    out = pl.BlockSpec((tm, width), tok)
    return pl.pallas_call(
        functools.partial(_proj_rwkv_kernel, n_wt=n_wt, first_tile=first_tile, n_cols=wt.shape[0],
                          width=width, tiles_per_seq=S // tm),
        grid=(n_wt + T // tm,),
        in_specs=[
            pl.BlockSpec((tm, D), tok),
            pl.BlockSpec((tile, D), lambda s: (first_tile + jnp.minimum(s, n_wt - 1), 0)),
            row(W), row(width), row(width), row(width), row(width),
            pl.BlockSpec((LANES, 2 * width), lambda s: (0, 0)),
            pl.BlockSpec((2 * LANES, width), lambda s: (0, 0)),
        ],
        out_specs=[out] * 7,
        out_shape=[jax.ShapeDtypeStruct((T, width), F32)] * 7,
        scratch_shapes=[pltpu.VMEM((n_wt, tile, D), BF16), pltpu.VMEM((tm, W), F32), pltpu.VMEM((8, W), F32)],
        compiler_params=_cparams(("arbitrary",)),
    )(x, wt, mu, w0, a0, k_k, k_a, w12, g2p)


GROUP = 4


def _block_diag(x):
    head = lax.broadcasted_iota(jnp.int32, x.shape, 1) // HEAD
    zero = jnp.zeros_like(x)
    return jnp.concatenate([jnp.where(head == h, x, zero) for h in range(x.shape[1] // HEAD)], axis=0)


def _chunk_step(rs, lws, k2s, vs, kks, bs, sts):
    C = CHUNK
    GL = rs[0].shape[1]
    idx = range(len(rs))
    row = lax.broadcasted_iota(jnp.int32, (C, GL), 0)
    tcol = lax.broadcasted_iota(jnp.int32, (C, GL), 1) % HEAD
    strict = row > tcol
    incl = row >= tcol
    eye = jnp.where(row == tcol, 1.0, 0.0)
    bdb = lambda x: _block_diag(x).astype(BF16)

    cums = list(lws)
    sh = 1
    while sh < C:
        cums = [cm + jnp.where(row >= sh, pltpu.roll(cm, sh, axis=0), 0.0) for cm in cums]
        sh *= 2
    tots = [cm[C - 1:C, :] for cm in cums]
    e_negs = [jnp.exp(-cm) for cm in cums]
    ARs = [jnp.concatenate([-kks[i] * jnp.exp(cums[i] - lws[i]), rs[i] * jnp.exp(cums[i])], axis=0).astype(BF16)
           for i in idx]
    Mbs = [_dot_nt(ARs[i], bdb(bs[i] * e_negs[i])) for i in idx]
    Mks = [_dot_nt(ARs[i], bdb(k2s[i] * e_negs[i])) for i in idx]
    ARSs = [_dot_nt(ARs[i], sts[i].astype(BF16)) for i in idx]
    vbds = [bdb(v) for v in vs]

    Ls = [jnp.where(strict, Mbs[i][0:C], 0.0) for i in idx]
    Ps = [eye + L for L in Ls]
    Lps = [_dot(L.astype(BF16), bdb(L)) for L in Ls]
    Xs = [ARSs[i][0:C] + _dot(jnp.where(strict, Mks[i][0:C], 0.0).astype(BF16), vbds[i]) for i in idx]
    m = 2
    while 2 * m < C:
        both = [_dot(jnp.concatenate([Lps[i], Ps[i]], axis=0).astype(BF16), bdb(Lps[i])) for i in idx]
        Ps = [Ps[i] + both[i][C:2 * C] for i in idx]
        Lps = [bt[0:C] for bt in both]
        m *= 2
    Ps = [Ps[i] + _dot(Ps[i].astype(BF16), bdb(Lps[i])) for i in idx]

    Us = [_dot(Ps[i].astype(BF16), bdb(Xs[i])) for i in idx]
    Ys = [ARSs[i][C:2 * C]
          + _dot(jnp.concatenate([jnp.where(incl, Mbs[i][C:2 * C], 0.0),
                                  jnp.where(incl, Mks[i][C:2 * C], 0.0)], axis=1).astype(BF16),
                 jnp.concatenate([bdb(Us[i]), vbds[i]], axis=0)) for i in idx]
    same_head = _same_head(GL)
    st_news = []
    for i in idx:
        e_rem = jnp.exp(tots[i] - cums[i])
        UV = jnp.concatenate([Us[i], vs[i]], axis=0).astype(BF16)
        BK = jnp.concatenate([bs[i] * e_rem, k2s[i] * e_rem], axis=0).astype(BF16)
        st_news.append(jnp.where(same_head, sts[i] * jnp.exp(tots[i]) + _dot_tn(UV, BK), 0.0))
    return Ys, st_news


def _rwkv_chunk_kernel(r_ref, lw_ref, k2_ref, v_ref, kk_ref, b_ref, g_ref, rk_ref, gng_ref, gnb_ref,
                       o_ref, st_ref, *, nbatch, ngroups):
    c = pl.program_id(2)

    @pl.when(c == 0)
    def _():
        st_ref[...] = jnp.zeros_like(st_ref)

    GL = st_ref.shape[1]
    ones = _head_ones(GL)
    chains = [(bi, p) for bi in range(nbatch) for p in range(ngroups)]
    sl = lambda p: slice(p * GL, (p + 1) * GL)
    get = lambda ref: [ref[bi, :, sl(p)] for bi, p in chains]
    rs, k2s, vs = get(r_ref), get(k2_ref), get(v_ref)
    ys, st_news = _chunk_step(rs, get(lw_ref), k2s, vs, get(kk_ref), get(b_ref),
                              [st_ref[i] for i in range(len(chains))])
    for i in range(len(chains)):
        st_ref[i] = st_news[i]
    ds = [y - _head_sum(y, ones) * (1.0 / HEAD) for y in ys]
    vars_ = [_head_sum(d * d, ones) * (1.0 / HEAD) for d in ds]
    for i, (bi, p) in enumerate(chains):
        yn = ds[i] * lax.rsqrt(vars_[i] + GN_EPS) * gng_ref[:, sl(p)] + gnb_ref[:, sl(p)]
        bonus = _head_sum(rs[i] * k2s[i] * rk_ref[:, sl(p)], ones) * vs[i]
        o_ref[bi, :, sl(p)] = ((yn + bonus) * g_ref[bi, :, sl(p)]).astype(o_ref.dtype)


def rwkv_chunk(r, lw, k2, v, kk, b, g, r_k, gn_g, gn_b, *, nbatch=4, ngroups=4):
    B, S, W = r.shape
    GL = min(GROUP * HEAD, W)
    ngroups = min(ngroups, W // GL)
    nbatch = min(nbatch, B)
    wb = ngroups * GL
    seq = pl.BlockSpec((nbatch, CHUNK, wb), lambda bi, p, c: (bi, c, p))
    par = pl.BlockSpec((1, wb), lambda bi, p, c: (0, p))
    return pl.pallas_call(
        functools.partial(_rwkv_chunk_kernel, nbatch=nbatch, ngroups=ngroups),
        grid=(B // nbatch, W // wb, S // CHUNK),
        in_specs=[seq] * 7 + [par] * 3,
        out_specs=seq,
        out_shape=jax.ShapeDtypeStruct((B, S, W), BF16),
        scratch_shapes=[pltpu.VMEM((nbatch * ngroups, GL, GL), F32)],
        compiler_params=_cparams(("parallel", "parallel", "arbitrary")),
    )(r, lw, k2, v, kk, b, g, r_k, gn_g, gn_b)


def _out_ln_kernel(x_ref, oa_ref, or_ref, wa_ref, wr_ref, g_ref, b_ref, o_ref, *, alpha):
    mix = _dot(oa_ref[...], wa_ref[...]) + _dot(or_ref[...], wr_ref[...])
    o_ref[...] = _layer_norm(alpha * x_ref[...] + mix, g_ref[...], b_ref[...])


def out_ln(x, oa, orw, wa, wr, g, b, *, alpha, tm=512):
    T, D = x.shape
    tm = min(tm, T)
    ka, kr = oa.shape[1], orw.shape[1]
    return pl.pallas_call(
        functools.partial(_out_ln_kernel, alpha=alpha),
        grid=(T // tm,),
        in_specs=[
            pl.BlockSpec((tm, D), lambda i: (i, 0)),
            pl.BlockSpec((tm, ka), lambda i: (i, 0)),
            pl.BlockSpec((tm, kr), lambda i: (i, 0)),
            pl.BlockSpec((ka, D), lambda i: (0, 0)),
            pl.BlockSpec((kr, D), lambda i: (0, 0)),
            pl.BlockSpec((1, D), lambda i: (0, 0)),
            pl.BlockSpec((1, D), lambda i: (0, 0)),
        ],
        out_specs=pl.BlockSpec((tm, D), lambda i: (i, 0)),
        out_shape=jax.ShapeDtypeStruct((T, D), F32),
        compiler_params=_cparams(("parallel",)),
    )(x, oa, orw, wa, wr, g, b)


def _layer(x, l, ffn1_w_gate, ffn1_w_up, ffn1_w_down, ln1_g, ln1_b, w_in,
           lambda_q1, lambda_k1, lambda_q2, lambda_k2, attn_norm_g,
           rwkv_mu, rwkv_w0, rwkv_w2, rwkv_a0, rwkv_a2, rwkv_g2,
           rwkv_k_k, rwkv_k_a, rwkv_r_k, rwkv_gn_g, rwkv_gn_b,
           w_out, ln2_g, ln2_b, ffn2_w_gate, ffn2_w_up, ffn2_w_down, ln3_g, ln3_b):
    B, S, D = x.shape
    T = B * S
    alpha = (2.0 * DEPTH) ** 0.25
    lambda_init = 0.8 - 0.6 * math.exp(-0.3 * l)
    rw = rwkv_w0.shape[-1]
    aw = w_out.shape[1] - rw
    H = aw // LANES
    n_wd, n_ad, n_gd = rwkv_w2.shape[1], rwkv_a2.shape[1], rwkv_g2.shape[1]
    assert n_wd == HEAD and n_ad == HEAD and n_gd <= 2 * LANES
    row = lambda a: a.reshape(1, -1)
    bf = lambda a: a.astype(BF16)

    x1 = ffn_ln(x.reshape(T, D), ffn1_w_gate[l], ffn1_w_up[l], ffn1_w_down[l],
                row(ln1_g[l]), row(ln1_b[l]), alpha=alpha)

    wi = jnp.swapaxes(w_in[l], 0, 1)
    p_attn = proj_attn(x1, wi, n_cols=3 * aw, n_scaled_cols=aw, scale=HEAD ** -0.5 * math.log2(math.e))
    tab, feat = alibi_tables(H, S)
    o_attn = diff_attn(p_attn, tab, feat, row(lambda_q1[l]), row(lambda_k1[l]), row(lambda_q2[l]),
                       row(lambda_k2[l]), row(attn_norm_g[l]), B=B, S=S, H=H, lambda_init=lambda_init)

    mu = jnp.pad(rwkv_mu[l], (0, RWKV_TAIL - (n_wd + n_ad + n_gd)))
    w12 = jnp.zeros((LANES, 2 * rw), F32)
    w12 = w12.at[:HEAD, :rw].set(rwkv_w2[l]).at[HEAD:, rw:].set(rwkv_a2[l])
    g2p = jnp.pad(rwkv_g2[l], ((0, 2 * LANES - n_gd), (0, 0)))
    seqs = proj_rwkv(x1, wi, row(mu), row(rwkv_w0[l]), row(rwkv_a0[l]), row(rwkv_k_k[l]), row(rwkv_k_a[l]),
                     bf(w12), bf(g2p), S=S, col0=3 * aw, width=rw)
    r, lw, k2, v, kk, b, g = [t.reshape(B, S, rw) for t in seqs]
    o_rwkv = rwkv_chunk(r, lw, k2, v, kk, b, g, row(rwkv_r_k[l]), row(rwkv_gn_g[l]), row(rwkv_gn_b[l]))

    wo = w_out[l]
    x2 = out_ln(x1, o_attn, o_rwkv.reshape(T, rw), bf(wo[:aw]), bf(wo[aw:]),
                row(ln2_g[l]), row(ln2_b[l]), alpha=alpha)
    x3 = ffn_ln(x2, ffn2_w_gate[l], ffn2_w_up[l], ffn2_w_down[l],
                row(ln3_g[l]), row(ln3_b[l]), alpha=alpha)
    return x3.reshape(B, S, D)


def kernel(x, ffn1_w_gate, ffn1_w_up, ffn1_w_down, ln1_g, ln1_b, w_in, lambda_q1, lambda_k1, lambda_q2, lambda_k2, attn_norm_g, rwkv_mu, rwkv_w0, rwkv_w2, rwkv_a0, rwkv_a2, rwkv_g2, rwkv_k_k, rwkv_k_a, rwkv_r_k, rwkv_gn_g, rwkv_gn_b, w_out, ln2_g, ln2_b, ffn2_w_gate, ffn2_w_up, ffn2_w_down, ln3_g, ln3_b):
    for l in range(DEPTH):
        x = _layer(x, l, ffn1_w_gate, ffn1_w_up, ffn1_w_down, ln1_g, ln1_b, w_in,
                   lambda_q1, lambda_k1, lambda_q2, lambda_k2, attn_norm_g,
                   rwkv_mu, rwkv_w0, rwkv_w2, rwkv_a0, rwkv_a2, rwkv_g2,
                   rwkv_k_k, rwkv_k_a, rwkv_r_k, rwkv_gn_g, rwkv_gn_b,
                   w_out, ln2_g, ln2_b, ffn2_w_gate, ffn2_w_up, ffn2_w_down, ln3_g, ln3_b)
    return x
```

```python
import functools
import math

import jax
import jax.numpy as jnp
from jax import lax
from jax.experimental import pallas as pl
from jax.experimental.pallas import tpu as pltpu

F32 = jnp.float32
BF16 = jnp.bfloat16

DEPTH = 1
LN_EPS = 1e-5
ATTN_NORM_EPS = 1e-5
GN_EPS = 64e-5
HEAD = 64
LANES = 128
CHUNK = 64
VMEM_LIMIT = 56 * 1024 * 1024


def _cparams(sem):
    return pltpu.CompilerParams(dimension_semantics=sem, vmem_limit_bytes=VMEM_LIMIT)


def _layer_norm(y, g, b):
    mu = jnp.mean(y, axis=-1, keepdims=True)
    d = y - mu
    var = jnp.mean(d * d, axis=-1, keepdims=True)
    return d * lax.rsqrt(var + LN_EPS) * g + b


def _dot(a, b):
    return jnp.dot(a, b, preferred_element_type=F32)


def _dot_nt(a, b):
    return lax.dot_general(a, b, (((1,), (1,)), ((), ())), preferred_element_type=F32)


def _dot_tn(a, b):
    return lax.dot_general(a, b, (((0,), (0,)), ((), ())), preferred_element_type=F32)


def _ffn_ln_kernel(x_ref, wg_ref, wu_ref, wd_ref, g_ref, b_ref, o_ref, xb_ref, *, alpha, ncol):
    j = pl.program_id(1)

    @pl.when(j == 0)
    def _():
        x = x_ref[...]
        xb_ref[...] = x.astype(BF16)
        o_ref[...] = alpha * x

    xb = xb_ref[...]
    hg = _dot(xb, wg_ref[...].astype(BF16))
    hu = _dot(xb, wu_ref[...].astype(BF16))
    h = (0.5 * hg * jax.nn.sigmoid(hg) * hu).astype(BF16)
    D = o_ref.shape[1]
    for n in range(D // ncol):
        sl = slice(n * ncol, (n + 1) * ncol)
        o_ref[:, sl] += _dot(h, wd_ref[:, sl].astype(BF16))

    @pl.when(j == pl.num_programs(1) - 1)
    def _():
        o_ref[...] = _layer_norm(o_ref[...], g_ref[...], b_ref[...])


def ffn_ln(x, wg, wu, wd, g, b, *, alpha, tm=1024, tf=256, ncol=512):
    T, D = x.shape
    Fd = wg.shape[1]
    tm = min(tm, T)
    tf = min(tf, Fd)
    ncol = min(ncol, D)
    return pl.pallas_call(
        functools.partial(_ffn_ln_kernel, alpha=alpha, ncol=ncol),
        grid=(T // tm, Fd // tf),
        in_specs=[
            pl.BlockSpec((tm, D), lambda i, j: (i, 0)),
            pl.BlockSpec((D, tf), lambda i, j: (0, j)),
            pl.BlockSpec((D, tf), lambda i, j: (0, j)),
            pl.BlockSpec((tf, D), lambda i, j: (j, 0)),
            pl.BlockSpec((1, D), lambda i, j: (0, 0)),
            pl.BlockSpec((1, D), lambda i, j: (0, 0)),
        ],
        out_specs=pl.BlockSpec((tm, D), lambda i, j: (i, 0)),
        out_shape=jax.ShapeDtypeStruct((T, D), F32),
        scratch_shapes=[pltpu.VMEM((tm, D), BF16)],
        compiler_params=_cparams(("parallel", "arbitrary")),
    )(x, wg, wu, wd, g, b)


def _pick_tile(limit, *sizes):
    t = limit // LANES * LANES
    while any(s % t for s in sizes):
        t -= LANES
    return t


def _stage_weight_tile(wt_ref, wb_ref, s, first_tile, n_cols):
    tile = wt_ref.shape[0]
    col = (first_tile + s) * tile + lax.broadcasted_iota(jnp.int32, wt_ref.shape, 0)
    wb_ref[pl.ds(pl.multiple_of(s * tile, tile), tile), :] = jnp.where(col < n_cols, wt_ref[...], 0.0).astype(BF16)


def _proj_attn_kernel(x_ref, wt_ref, o_ref, wb_ref, *, n_wt, n_cols, n_scaled, scale):
    s = pl.program_id(0)

    @pl.when(s < n_wt)
    def _():
        _stage_weight_tile(wt_ref, wb_ref, s, 0, n_cols)

    @pl.when(s >= n_wt)
    def _():
        tile = wt_ref.shape[0]
        xb = x_ref[...].astype(BF16)
        for n in range(n_wt):
            acc = _dot_nt(xb, wb_ref[n * tile:(n + 1) * tile, :])
            if n < n_scaled:
                acc = acc * scale
            o_ref[:, n * tile:(n + 1) * tile] = acc.astype(o_ref.dtype)


def proj_attn(x, wt, *, n_cols, n_scaled_cols, scale, tm=512, tile=512):
    T, D = x.shape
    tm = min(tm, T)
    tile = _pick_tile(tile, n_cols, n_scaled_cols)
    n_wt = n_cols // tile
    tok = lambda s: (jnp.maximum(s - n_wt, 0), 0)
    return pl.pallas_call(
        functools.partial(_proj_attn_kernel, n_wt=n_wt, n_cols=wt.shape[0],
                          n_scaled=n_scaled_cols // tile, scale=scale),
        grid=(n_wt + T // tm,),
        in_specs=[
            pl.BlockSpec((tm, D), tok),
            pl.BlockSpec((tile, D), lambda s: (jnp.minimum(s, n_wt - 1), 0)),
        ],
        out_specs=pl.BlockSpec((tm, n_cols), tok),
        out_shape=jax.ShapeDtypeStruct((T, n_cols), BF16),
        scratch_shapes=[pltpu.VMEM((n_cols, D), BF16)],
        compiler_params=_cparams(("arbitrary",)),
    )(x, wt)


ATT_TQ = 128
ATT_TK = 512
N_BIAS_TERMS = 3
BF16_ROWS = 16
VT_ROWS = LANES + BF16_ROWS


def alibi_tables(H, S):
    slope = jnp.exp2(-8.0 * (jnp.arange(H, dtype=F32) + 1.0) / H) * math.log2(math.e)
    terms, rest = [], slope
    for _ in range(N_BIAS_TERMS):
        t = rest.astype(BF16).astype(F32)
        terms += [64.0 * t, t]
        rest = rest - t
    tab = jnp.stack(terms + [jnp.zeros_like(slope)] * (8 - len(terms)), axis=1)
    pos = jnp.arange(S, dtype=jnp.int32)
    hi, lo = (pos // 64).astype(F32), (pos % 64).astype(F32)
    feat = jnp.stack([hi, lo] * N_BIAS_TERMS, axis=1)
    feat = jnp.pad(feat, ((0, 0), (0, LANES - feat.shape[1]))).astype(BF16)
    return tab, feat


def _diff_attn_kernel(tab_ref, q_ref, k_ref, v_ref, feat_ref, lq1_ref, lk1_ref, lq2_ref, lk2_ref, ng_ref,
                      o_ref, ka_ref, vt_ref, qa_ref, acc_ref, m_ref, *, hb, nck, lambda_init):
    tq = ATT_TQ
    tk = vt_ref.shape[2]
    g = pl.program_id(1)
    i = pl.program_id(2)
    heads = range(hb)
    hs = lambda h: slice(h * LANES, (h + 1) * LANES)

    @pl.when(i == 0)
    def _():
        for h in heads:
            ka_ref[h, :, 0:LANES] = k_ref[:, hs(h)]
            ka_ref[h, :, LANES:2 * LANES] = feat_ref[...]
            for c in range(nck):
                vt_ref[h * nck + c, 0:LANES, :] = v_ref[c * tk:(c + 1) * tk, hs(h)].astype(F32).T.astype(BF16)
                vt_ref[h * nck + c, LANES:VT_ROWS, :] = jnp.ones((BF16_ROWS, tk), BF16)

    lane = lax.broadcasted_iota(jnp.int32, (tq, LANES), 1)
    for h in heads:
        q = q_ref[:, hs(h)]
        zero = jnp.zeros_like(q)
        qf = jnp.zeros((tq, LANES), F32)
        for n in range(2 * N_BIAS_TERMS):
            qf = jnp.where(lane == n, tab_ref[g * hb + h, n], qf)
        qf = qf.astype(BF16)
        qa_ref[h, 0:tq, 0:LANES] = jnp.where(lane < HEAD, q, zero)
        qa_ref[h, tq:2 * tq, 0:LANES] = jnp.where(lane >= HEAD, q, zero)
        qa_ref[h, 0:tq, LANES:2 * LANES] = qf
        qa_ref[h, tq:2 * tq, LANES:2 * LANES] = qf
        m_ref[h] = jnp.full((1, 2 * tq), -1e30, F32)
        acc_ref[h] = jnp.zeros((VT_ROWS, 2 * tq), F32)

    def chunk(c, masked):
        start = pl.multiple_of(c * tk, tk)
        ss = [_dot_nt(ka_ref[h, pl.ds(start, tk), :], qa_ref[h]) for h in heads]
        if masked:
            key = start + lax.broadcasted_iota(jnp.int32, (tk, 2 * tq), 0)
            qcol = lax.broadcasted_iota(jnp.int32, (tk, 2 * tq), 1)
            qpos = i * tq + jnp.where(qcol >= tq, qcol - tq, qcol)
            keep = key <= qpos
            ss = [jnp.where(keep, s, -jnp.inf) for s in ss]
        m_olds = [m_ref[h] for h in heads]
        m_news = [jnp.maximum(m_olds[h], jnp.max(ss[h], axis=0, keepdims=True)) for h in heads]
        ps = [jnp.exp2(ss[h] - m_news[h]).astype(BF16) for h in heads]
        alphas = [jnp.exp2(m_olds[h] - m_news[h]) for h in heads]
        pvs = [_dot(vt_ref[h * nck + c], ps[h]) for h in heads]
        for h in heads:
            m_ref[h] = m_news[h]
            acc_ref[h] = alphas[h] * acc_ref[h] + pvs[h]

    n_full = i // (tk // tq)

    def body(c, carry):
        chunk(c, False)
        return carry

    lax.fori_loop(0, n_full, body, 0)
    chunk(n_full, True)

    lam = (jnp.exp(jnp.sum(lq1_ref[...] * lk1_ref[...], axis=-1, keepdims=True))
           - jnp.exp(jnp.sum(lq2_ref[...] * lk2_ref[...], axis=-1, keepdims=True)) + lambda_init)
    for h in heads:
        acc = acc_ref[h]
        rl = 1.0 / acc[LANES:LANES + 1, :]
        ot = (acc[0:LANES, 0:tq] * rl[:, 0:tq]
              - lam * (acc[0:LANES, tq:2 * tq] * rl[:, tq:2 * tq]))
        o = ot.T
        o = o * lax.rsqrt(jnp.mean(o * o, axis=-1, keepdims=True) + ATTN_NORM_EPS) * ng_ref[...]
        o_ref[:, hs(h)] = (o * (1.0 - lambda_init)).astype(o_ref.dtype)


def diff_attn(pa, tab, feat, lq1, lk1, lq2, lk2, norm_g, *, B, S, H, lambda_init, hb=8):
    hb = min(hb, H)
    tq, tk = ATT_TQ, min(ATT_TK, S)
    assert S % tk == 0 and H % hb == 0 and tk % tq == 0
    nq, nck, ng = S // tq, S // tk, H // hb
    wb = hb * LANES
    small = lambda n: pl.BlockSpec((1, n), lambda b, g, i: (0, 0))
    return pl.pallas_call(
        functools.partial(_diff_attn_kernel, hb=hb, nck=nck, lambda_init=lambda_init),
        grid=(B, ng, nq),
        in_specs=[
            pl.BlockSpec(memory_space=pltpu.SMEM),
            pl.BlockSpec((tq, wb), lambda b, g, i: (b * nq + i, g)),
            pl.BlockSpec((S, wb), lambda b, g, i: (b, ng + g)),
            pl.BlockSpec((S, wb), lambda b, g, i: (b, 2 * ng + g)),
            pl.BlockSpec((S, LANES), lambda b, g, i: (0, 0)),
            small(HEAD), small(HEAD), small(HEAD), small(HEAD), small(2 * HEAD),
        ],
        out_specs=pl.BlockSpec((tq, wb), lambda b, g, i: (b * nq + i, g)),
        out_shape=jax.ShapeDtypeStruct((B * S, H * LANES), BF16),
        scratch_shapes=[
            pltpu.VMEM((hb, S, 2 * LANES), BF16),
            pltpu.VMEM((hb * nck, VT_ROWS, tk), BF16),
            pltpu.VMEM((hb, 2 * tq, 2 * LANES), BF16),
            pltpu.VMEM((hb, VT_ROWS, 2 * tq), F32),
            pltpu.VMEM((hb, 1, 2 * tq), F32),
        ],
        compiler_params=_cparams(("parallel", "parallel", "arbitrary")),
    )(tab, pa, pa, pa, feat, lq1, lk1, lq2, lk2, norm_g)


def _same_head(n):
    r = lax.broadcasted_iota(jnp.int32, (n, n), 0) // HEAD
    c = lax.broadcasted_iota(jnp.int32, (n, n), 1) // HEAD
    return r == c


def _head_ones(n):
    return jnp.where(_same_head(n), 1.0, 0.0).astype(BF16)


def _head_sum(x, ones, split=True):
    hi = x.astype(BF16)
    if not split:
        return _dot(hi, ones)
    lo = (x - hi.astype(F32)).astype(BF16)
    return _dot(hi, ones) + _dot(lo, ones)


RWKV_TAIL = 3 * LANES


def _proj_rwkv_kernel(x_ref, wt_ref, mu_ref, w0_ref, a0_ref, kk_ref, ka_ref, w12_ref, g2_ref,
                      r_ref, lw_ref, k2_ref, v_ref, kkn_ref, b_ref, g_ref, wb_ref, pr_ref, prev_ref,
                      *, n_wt, first_tile, n_cols, width, tiles_per_seq):
    s = pl.program_id(0)

    @pl.when(s == 0)
    def _():
        prev_ref[...] = jnp.zeros_like(prev_ref)

    @pl.when(s < n_wt)
    def _():
        _stage_weight_tile(wt_ref, wb_ref, s, first_tile, n_cols)

    @pl.when(s >= n_wt)
    def _():
        tm = x_ref.shape[0]
        pr_ref[...] = _dot_nt(x_ref[...].astype(BF16), wb_ref[...])

        first = (s - n_wt) % tiles_per_seq == 0
        row = lax.broadcasted_iota(jnp.int32, (tm, 1), 0)

        def mixed(c0, c1):
            x = pr_ref[:, c0:c1]
            prow = jnp.where(first, 0.0, prev_ref[0:1, c0:c1])
            xs = jnp.where(row == 0, prow, pltpu.roll(x, 1, axis=0))
            return x + mu_ref[:, c0:c1] * (xs - x)

        c3 = 3 * width
        lora_in = mixed(c3, c3 + LANES)
        lane = lax.broadcasted_iota(jnp.int32, lora_in.shape, 1)
        lora_in = jnp.where(lane < HEAD, jnp.tanh(lora_in), lora_in)
        z = _dot(lora_in.astype(BF16), w12_ref[...])
        g_ref[...] = _dot(jax.nn.sigmoid(mixed(c3 + LANES, c3 + RWKV_TAIL)).astype(BF16), g2_ref[...])

        lw_ref[...] = -math.exp(-0.5) * jax.nn.sigmoid(w0_ref[...] + z[:, 0:width])
        a = jax.nn.sigmoid(a0_ref[...] + z[:, width:2 * width])

        r_ref[...] = mixed(0, width)
        v_ref[...] = mixed(2 * width, c3)
        k = mixed(width, 2 * width)
        k2_ref[...] = k * (1.0 + (a - 1.0) * ka_ref[...])

        kx = k * kk_ref[...]
        ones = _head_ones(LANES)
        ss = jnp.concatenate(
            [_head_sum(kx[:, t * LANES:(t + 1) * LANES] * kx[:, t * LANES:(t + 1) * LANES], ones)
             for t in range(width // LANES)], axis=1)
        kkn = kx / jnp.maximum(jnp.sqrt(ss), 1e-12)
        kkn_ref[...] = kkn
        b_ref[...] = kkn * a
        prev_ref[0:1, :] = pr_ref[tm - 1:tm, :]


def proj_rwkv(x, wt, mu, w0, a0, k_k, k_a, w12, g2p, *, S, col0, width, tm=256, tile=384):
    T, D = x.shape
    tm = min(tm, S)
    W = 3 * width + RWKV_TAIL
    tile = _pick_tile(tile, col0, W)
    assert S % tm == 0
    n_wt, first_tile = W // tile, col0 // tile
    tok = lambda s: (jnp.maximum(s - n_wt, 0), 0)
    row = lambda n: pl.BlockSpec((1, n), lambda s: (0, 0))
    out = pl.BlockSpec((tm, width), tok)
    return pl.pallas_call(
        functools.partial(_proj_rwkv_kernel, n_wt=n_wt, first_tile=first_tile, n_cols=wt.shape[0],
                          width=width, tiles_per_seq=S // tm),
        grid=(n_wt + T // tm,),
        in_specs=[
            pl.BlockSpec((tm, D), tok),
            pl.BlockSpec((tile, D), lambda s: (first_tile + jnp.minimum(s, n_wt - 1), 0)),
            row(W), row(width), row(width), row(width), row(width),
            pl.BlockSpec((LANES, 2 * width), lambda s: (0, 0)),
            pl.BlockSpec((2 * LANES, width), lambda s: (0, 0)),
        ],
        out_specs=[out] * 7,
        out_shape=[jax.ShapeDtypeStruct((T, width), F32)] * 7,
        scratch_shapes=[pltpu.VMEM((W, D), BF16), pltpu.VMEM((tm, W), F32), pltpu.VMEM((8, W), F32)],
        compiler_params=_cparams(("arbitrary",)),
    )(x, wt, mu, w0, a0, k_k, k_a, w12, g2p)


GROUP = 2


def _block_diag(x):
    head = lax.broadcasted_iota(jnp.int32, x.shape, 1) // HEAD
    zero = jnp.zeros_like(x)
    return jnp.concatenate([jnp.where(head == h, x, zero) for h in range(x.shape[1] // HEAD)], axis=0)


def _chunk_step(rs, lws, k2s, vs, kks, bs, sts):
    C = CHUNK
    GL = rs[0].shape[1]
    idx = range(len(rs))
    row = lax.broadcasted_iota(jnp.int32, (C, GL), 0)
    tcol = lax.broadcasted_iota(jnp.int32, (C, GL), 1) % HEAD
    strict = row > tcol
    incl = row >= tcol
    eye = jnp.where(row == tcol, 1.0, 0.0)
    bdb = lambda x: _block_diag(x).astype(BF16)

    cums, tots, Mbs, Mks, ARSs = [], [], [], [], []
    for i in idx:
        cm = lws[i]
        sh = 1
        while sh < C:
            cm = cm + jnp.where(row >= sh, pltpu.roll(cm, sh, axis=0), 0.0)
            sh *= 2
        e_neg = jnp.exp(-cm)
        AR = jnp.concatenate([-kks[i] * jnp.exp(cm - lws[i]), rs[i] * jnp.exp(cm)], axis=0).astype(BF16)
        cums.append(cm)
        tots.append(cm[C - 1:C, :])
        Mbs.append(_dot_nt(AR, bdb(bs[i] * e_neg)))
        Mks.append(_dot_nt(AR, bdb(k2s[i] * e_neg)))
        ARSs.append(_dot_nt(AR, sts[i].astype(BF16)))
    vbds = [bdb(v) for v in vs]

    Ls = [jnp.where(strict, Mbs[i][0:C], 0.0) for i in idx]
    Ps = [eye + L for L in Ls]
    Lps = [_dot(L.astype(BF16), bdb(L)) for L in Ls]
    Xs = [ARSs[i][0:C] + _dot(jnp.where(strict, Mks[i][0:C], 0.0).astype(BF16), vbds[i]) for i in idx]
    m = 2
    while 2 * m < C:
        both = [_dot(jnp.concatenate([Lps[i], Ps[i]], axis=0).astype(BF16), bdb(Lps[i])) for i in idx]
        Ps = [Ps[i] + both[i][C:2 * C] for i in idx]
        Lps = [bt[0:C] for bt in both]
        m *= 2
    Ps = [Ps[i] + _dot(Ps[i].astype(BF16), bdb(Lps[i])) for i in idx]

    Us = [_dot(Ps[i].astype(BF16), bdb(Xs[i])) for i in idx]
    Ys = [ARSs[i][C:2 * C]
          + _dot(jnp.concatenate([jnp.where(incl, Mbs[i][C:2 * C], 0.0),
                                  jnp.where(incl, Mks[i][C:2 * C], 0.0)], axis=1).astype(BF16),
                 jnp.concatenate([bdb(Us[i]), vbds[i]], axis=0)) for i in idx]
    same_head = _same_head(GL)
    st_news = []
    for i in idx:
        e_rem = jnp.exp(tots[i] - cums[i])
        UV = jnp.concatenate([Us[i], vs[i]], axis=0).astype(BF16)
        BK = jnp.concatenate([bs[i] * e_rem, k2s[i] * e_rem], axis=0).astype(BF16)
        st_news.append(jnp.where(same_head, sts[i] * jnp.exp(tots[i]) + _dot_tn(UV, BK), 0.0))
    return Ys, st_news


def _rwkv_chunk_kernel(r_ref, lw_ref, k2_ref, v_ref, kk_ref, b_ref, g_ref, rk_ref, gng_ref, gnb_ref,
                       o_ref, st_ref, *, nbatch, ngroups):
    c = pl.program_id(2)

    @pl.when(c == 0)
    def _():
        st_ref[...] = jnp.zeros_like(st_ref)

    GL = st_ref.shape[1]
    ones = _head_ones(GL)
    chains = [(bi, p) for bi in range(nbatch) for p in range(ngroups)]
    sl = lambda p: slice(p * GL, (p + 1) * GL)
    get = lambda ref: [ref[bi, :, sl(p)] for bi, p in chains]
    rs, k2s, vs = get(r_ref), get(k2_ref), get(v_ref)
    ys, st_news = _chunk_step(rs, get(lw_ref), k2s, vs, get(kk_ref), get(b_ref),
                              [st_ref[i] for i in range(len(chains))])
    for i in range(len(chains)):
        st_ref[i] = st_news[i]
    ds = [y - _head_sum(y, ones, split=False) * (1.0 / HEAD) for y in ys]
    vars_ = [_head_sum(d * d, ones, split=False) * (1.0 / HEAD) for d in ds]
    for i, (bi, p) in enumerate(chains):
        yn = ds[i] * lax.rsqrt(vars_[i] + GN_EPS) * gng_ref[:, sl(p)] + gnb_ref[:, sl(p)]
        bonus = _head_sum(rs[i] * k2s[i] * rk_ref[:, sl(p)], ones, split=False) * vs[i]
        o_ref[bi, :, sl(p)] = ((yn + bonus) * g_ref[bi, :, sl(p)]).astype(o_ref.dtype)


def rwkv_chunk(r, lw, k2, v, kk, b, g, r_k, gn_g, gn_b, *, nbatch=2, ngroups=8):
    B, S, W = r.shape
    GL = min(GROUP * HEAD, W)
    ngroups = min(ngroups, W // GL)
    nbatch = min(nbatch, B)
    wb = ngroups * GL
    seq = pl.BlockSpec((nbatch, CHUNK, wb), lambda bi, p, c: (bi, c, p))
    par = pl.BlockSpec((1, wb), lambda bi, p, c: (0, p))
    return pl.pallas_call(
        functools.partial(_rwkv_chunk_kernel, nbatch=nbatch, ngroups=ngroups),
        grid=(B // nbatch, W // wb, S // CHUNK),
        in_specs=[seq] * 7 + [par] * 3,
        out_specs=seq,
        out_shape=jax.ShapeDtypeStruct((B, S, W), BF16),
        scratch_shapes=[pltpu.VMEM((nbatch * ngroups, GL, GL), F32)],
        compiler_params=_cparams(("parallel", "parallel", "arbitrary")),
    )(r, lw, k2, v, kk, b, g, r_k, gn_g, gn_b)


def _out_ln_kernel(x_ref, oa_ref, or_ref, wa_ref, wr_ref, g_ref, b_ref, o_ref, *, alpha):
    half = x_ref.shape[0] // 2
    rows = [slice(0, half), slice(half, 2 * half)]
    mixes = [_dot(oa_ref[r, :], wa_ref[...]) + _dot(or_ref[r, :], wr_ref[...]) for r in rows]
    for r, mix in zip(rows, mixes):
        o_ref[r, :] = _layer_norm(alpha * x_ref[r, :] + mix, g_ref[...], b_ref[...])


def out_ln(x, oa, orw, wa, wr, g, b, *, alpha, tm=512):
    T, D = x.shape
    tm = min(tm, T)
    ka, kr = oa.shape[1], orw.shape[1]
    return pl.pallas_call(
        functools.partial(_out_ln_kernel, alpha=alpha),
        grid=(T // tm,),
        in_specs=[
            pl.BlockSpec((tm, D), lambda i: (i, 0)),
            pl.BlockSpec((tm, ka), lambda i: (i, 0)),
            pl.BlockSpec((tm, kr), lambda i: (i, 0)),
            pl.BlockSpec((ka, D), lambda i: (0, 0)),
            pl.BlockSpec((kr, D), lambda i: (0, 0)),
            pl.BlockSpec((1, D), lambda i: (0, 0)),
            pl.BlockSpec((1, D), lambda i: (0, 0)),
        ],
        out_specs=pl.BlockSpec((tm, D), lambda i: (i, 0)),
        out_shape=jax.ShapeDtypeStruct((T, D), F32),
        compiler_params=_cparams(("parallel",)),
    )(x, oa, orw, wa, wr, g, b)


def _layer(x, l, ffn1_w_gate, ffn1_w_up, ffn1_w_down, ln1_g, ln1_b, w_in,
           lambda_q1, lambda_k1, lambda_q2, lambda_k2, attn_norm_g,
           rwkv_mu, rwkv_w0, rwkv_w2, rwkv_a0, rwkv_a2, rwkv_g2,
           rwkv_k_k, rwkv_k_a, rwkv_r_k, rwkv_gn_g, rwkv_gn_b,
           w_out, ln2_g, ln2_b, ffn2_w_gate, ffn2_w_up, ffn2_w_down, ln3_g, ln3_b):
    B, S, D = x.shape
    T = B * S
    alpha = (2.0 * DEPTH) ** 0.25
    lambda_init = 0.8 - 0.6 * math.exp(-0.3 * l)
    rw = rwkv_w0.shape[-1]
    aw = w_out.shape[1] - rw
    H = aw // LANES
    n_wd, n_ad, n_gd = rwkv_w2.shape[1], rwkv_a2.shape[1], rwkv_g2.shape[1]
    assert n_wd == HEAD and n_ad == HEAD and n_gd <= 2 * LANES
    row = lambda a: a.reshape(1, -1)
    bf = lambda a: a.astype(BF16)

    x1 = ffn_ln(x.reshape(T, D), ffn1_w_gate[l], ffn1_w_up[l], ffn1_w_down[l],
                row(ln1_g[l]), row(ln1_b[l]), alpha=alpha)

    wi = jnp.swapaxes(w_in[l], 0, 1)
    p_attn = proj_attn(x1, wi, n_cols=3 * aw, n_scaled_cols=aw, scale=HEAD ** -0.5 * math.log2(math.e))
    tab, feat = alibi_tables(H, S)
    o_attn = diff_attn(p_attn, tab, feat, row(lambda_q1[l]), row(lambda_k1[l]), row(lambda_q2[l]),
                       row(lambda_k2[l]), row(attn_norm_g[l]), B=B, S=S, H=H, lambda_init=lambda_init)

    mu = jnp.pad(rwkv_mu[l], (0, RWKV_TAIL - (n_wd + n_ad + n_gd)))
    w12 = jnp.zeros((LANES, 2 * rw), F32)
    w12 = w12.at[:HEAD, :rw].set(rwkv_w2[l]).at[HEAD:, rw:].set(rwkv_a2[l])
    g2p = jnp.pad(rwkv_g2[l], ((0, 2 * LANES - n_gd), (0, 0)))
    seqs = proj_rwkv(x1, wi, row(mu), row(rwkv_w0[l]), row(rwkv_a0[l]), row(rwkv_k_k[l]), row(rwkv_k_a[l]),
                     bf(w12), bf(g2p), S=S, col0=3 * aw, width=rw)
    r, lw, k2, v, kk, b, g = [t.reshape(B, S, rw) for t in seqs]
    o_rwkv = rwkv_chunk(r, lw, k2, v, kk, b, g, row(rwkv_r_k[l]), row(rwkv_gn_g[l]), row(rwkv_gn_b[l]))

    wo = w_out[l]
    x2 = out_ln(x1, o_attn, o_rwkv.reshape(T, rw), bf(wo[:aw]), bf(wo[aw:]),
                row(ln2_g[l]), row(ln2_b[l]), alpha=alpha)
    x3 = ffn_ln(x2, ffn2_w_gate[l], ffn2_w_up[l], ffn2_w_down[l],
                row(ln3_g[l]), row(ln3_b[l]), alpha=alpha)
    return x3.reshape(B, S, D)


def kernel(x, ffn1_w_gate, ffn1_w_up, ffn1_w_down, ln1_g, ln1_b, w_in, lambda_q1, lambda_k1, lambda_q2, lambda_k2, attn_norm_g, rwkv_mu, rwkv_w0, rwkv_w2, rwkv_a0, rwkv_a2, rwkv_g2, rwkv_k_k, rwkv_k_a, rwkv_r_k, rwkv_gn_g, rwkv_gn_b, w_out, ln2_g, ln2_b, ffn2_w_gate, ffn2_w_up, ffn2_w_down, ln3_g, ln3_b):
    for l in range(DEPTH):
        x = _layer(x, l, ffn1_w_gate, ffn1_w_up, ffn1_w_down, ln1_g, ln1_b, w_in,
                   lambda_q1, lambda_k1, lambda_q2, lambda_k2, attn_norm_g,
                   rwkv_mu, rwkv_w0, rwkv_w2, rwkv_a0, rwkv_a2, rwkv_g2,
                   rwkv_k_k, rwkv_k_a, rwkv_r_k, rwkv_gn_g, rwkv_gn_b,
                   w_out, ln2_g, ln2_b, ffn2_w_gate, ffn2_w_up, ffn2_w_down, ln3_g, ln3_b)
    return x
```

```python
import functools
import math

import jax
import jax.numpy as jnp
from jax import lax
from jax.experimental import pallas as pl
from jax.experimental.pallas import tpu as pltpu

F32 = jnp.float32
BF16 = jnp.bfloat16

DEPTH = 1
LN_EPS = 1e-5
ATTN_NORM_EPS = 1e-5
GN_EPS = 64e-5
HEAD = 64
LANES = 128
CHUNK = 64
VMEM_LIMIT = 56 * 1024 * 1024


def _cparams(sem):
    return pltpu.CompilerParams(dimension_semantics=sem, vmem_limit_bytes=VMEM_LIMIT)


def _layer_norm(y, g, b):
    mu = jnp.mean(y, axis=-1, keepdims=True)
    d = y - mu
    var = jnp.mean(d * d, axis=-1, keepdims=True)
    return d * lax.rsqrt(var + LN_EPS) * g + b


def _dot(a, b):
    return jnp.dot(a, b, preferred_element_type=F32)


def _dot_nt(a, b):
    return lax.dot_general(a, b, (((1,), (1,)), ((), ())), preferred_element_type=F32)


def _dot_tn(a, b):
    return lax.dot_general(a, b, (((0,), (0,)), ((), ())), preferred_element_type=F32)


def _ffn_ln_kernel(x_ref, wg_ref, wu_ref, wd_ref, g_ref, b_ref, o_ref, xb_ref, *, alpha, ncol):
    j = pl.program_id(1)
    nj = pl.num_programs(1)
    tm, D = o_ref.shape
    cols = [slice(n * ncol, (n + 1) * ncol) for n in range(D // ncol)]

    def body(first, final):
        nrow = 2 if (first or final) else 1
        rows = [slice(r * tm // nrow, (r + 1) * tm // nrow) for r in range(nrow)]
        if first:
            for r in rows:
                x = x_ref[r, :]
                xb_ref[r, :] = x.astype(BF16)
                o_ref[r, :] = alpha * x
        wg = wg_ref[...].astype(BF16)
        wu = wu_ref[...].astype(BF16)
        wd = [wd_ref[:, c].astype(BF16) for c in cols]
        hs = []
        for r in rows:
            xb = xb_ref[r, :]
            hg = _dot(xb, wg)
            hu = _dot(xb, wu)
            hs.append((0.5 * hg * jax.nn.sigmoid(hg) * hu).astype(BF16))
        for r, h in zip(rows, hs):
            for c, w in zip(cols, wd):
                o_ref[r, c] += _dot(h, w)
        if final:
            for r in rows:
                o_ref[r, :] = _layer_norm(o_ref[r, :], g_ref[...], b_ref[...])

    pl.when(j == 0)(lambda: body(True, False))
    pl.when(jnp.logical_and(j > 0, j < nj - 1))(lambda: body(False, False))
    pl.when(jnp.logical_and(j > 0, j == nj - 1))(lambda: body(False, True))


def ffn_ln(x, wg, wu, wd, g, b, *, alpha, tm=1024, tf=256, ncol=512):
    T, D = x.shape
    Fd = wg.shape[1]
    tm = min(tm, T)
    tf = min(tf, Fd)
    ncol = min(ncol, D)
    assert Fd // tf >= 2
    return pl.pallas_call(
        functools.partial(_ffn_ln_kernel, alpha=alpha, ncol=ncol),
        grid=(T // tm, Fd // tf),
        in_specs=[
            pl.BlockSpec((tm, D), lambda i, j: (i, 0)),
            pl.BlockSpec((D, tf), lambda i, j: (0, j)),
            pl.BlockSpec((D, tf), lambda i, j: (0, j)),
            pl.BlockSpec((tf, D), lambda i, j: (j, 0)),
            pl.BlockSpec((1, D), lambda i, j: (0, 0)),
            pl.BlockSpec((1, D), lambda i, j: (0, 0)),
        ],
        out_specs=pl.BlockSpec((tm, D), lambda i, j: (i, 0)),
        out_shape=jax.ShapeDtypeStruct((T, D), F32),
        scratch_shapes=[pltpu.VMEM((tm, D), BF16)],
        compiler_params=_cparams(("parallel", "arbitrary")),
    )(x, wg, wu, wd, g, b)


def _pick_tile(limit, *sizes):
    t = limit // LANES * LANES
    while any(s % t for s in sizes):
        t -= LANES
    return t


def _stage_weight_tile(wt_ref, wb_ref, s, first_tile, n_cols):
    tile = wt_ref.shape[0]
    col = (first_tile + s) * tile + lax.broadcasted_iota(jnp.int32, wt_ref.shape, 0)
    wb_ref[pl.ds(pl.multiple_of(s * tile, tile), tile), :] = jnp.where(col < n_cols, wt_ref[...], 0.0).astype(BF16)


def _proj_attn_kernel(x_ref, wt_ref, o_ref, wb_ref, *, n_wt, n_cols, n_scaled, scale):
    s = pl.program_id(0)

    @pl.when(s < n_wt)
    def _():
        _stage_weight_tile(wt_ref, wb_ref, s, 0, n_cols)

    @pl.when(s >= n_wt)
    def _():
        tile = wt_ref.shape[0]
        xb = x_ref[...].astype(BF16)
        for n in range(n_wt):
            acc = _dot_nt(xb, wb_ref[n * tile:(n + 1) * tile, :])
            if n < n_scaled:
                acc = acc * scale
            o_ref[:, n * tile:(n + 1) * tile] = acc.astype(o_ref.dtype)


def proj_attn(x, wt, *, n_cols, n_scaled_cols, scale, tm=512, tile=512):
    T, D = x.shape
    tm = min(tm, T)
    tile = _pick_tile(tile, n_cols, n_scaled_cols)
    n_wt = n_cols // tile
    tok = lambda s: (jnp.maximum(s - n_wt, 0), 0)
    return pl.pallas_call(
        functools.partial(_proj_attn_kernel, n_wt=n_wt, n_cols=wt.shape[0],
                          n_scaled=n_scaled_cols // tile, scale=scale),
        grid=(n_wt + T // tm,),
        in_specs=[
            pl.BlockSpec((tm, D), tok),
            pl.BlockSpec((tile, D), lambda s: (jnp.minimum(s, n_wt - 1), 0)),
        ],
        out_specs=pl.BlockSpec((tm, n_cols), tok),
        out_shape=jax.ShapeDtypeStruct((T, n_cols), BF16),
        scratch_shapes=[pltpu.VMEM((n_cols, D), BF16)],
        compiler_params=_cparams(("arbitrary",)),
    )(x, wt)


ATT_TQ = 128
ATT_TK = 512
N_BIAS_TERMS = 3
BF16_ROWS = 16
VT_ROWS = LANES + BF16_ROWS


def alibi_tables(H, S):
    slope = jnp.exp2(-8.0 * (jnp.arange(H, dtype=F32) + 1.0) / H) * math.log2(math.e)
    terms, rest = [], slope
    for _ in range(N_BIAS_TERMS):
        t = rest.astype(BF16).astype(F32)
        terms += [64.0 * t, t]
        rest = rest - t
    tab = jnp.stack(terms + [jnp.zeros_like(slope)] * (8 - len(terms)), axis=1)
    pos = jnp.arange(S, dtype=jnp.int32)
    hi, lo = (pos // 64).astype(F32), (pos % 64).astype(F32)
    feat = jnp.stack([hi, lo] * N_BIAS_TERMS, axis=1)
    feat = jnp.pad(feat, ((0, 0), (0, LANES - feat.shape[1]))).astype(BF16)
    return tab, feat


def _diff_attn_kernel(tab_ref, q_ref, k_ref, v_ref, feat_ref, lq1_ref, lk1_ref, lq2_ref, lk2_ref, ng_ref,
                      o_ref, ka_ref, vt_ref, qa_ref, acc_ref, m_ref, *, hb, nck, lambda_init):
    tq = ATT_TQ
    tk = vt_ref.shape[2]
    g = pl.program_id(1)
    i = pl.program_id(2)
    heads = range(hb)
    hs = lambda h: slice(h * LANES, (h + 1) * LANES)

    @pl.when(i == 0)
    def _():
        for h in heads:
            ka_ref[h, :, 0:LANES] = k_ref[:, hs(h)]
            ka_ref[h, :, LANES:2 * LANES] = feat_ref[...]
            for c in range(nck):
                vt_ref[h * nck + c, 0:LANES, :] = v_ref[c * tk:(c + 1) * tk, hs(h)].astype(F32).T.astype(BF16)
                vt_ref[h * nck + c, LANES:VT_ROWS, :] = jnp.ones((BF16_ROWS, tk), BF16)

    lane = lax.broadcasted_iota(jnp.int32, (tq, LANES), 1)
    for h in heads:
        q = q_ref[:, hs(h)]
        zero = jnp.zeros_like(q)
        qf = jnp.zeros((tq, LANES), F32)
        for n in range(2 * N_BIAS_TERMS):
            qf = jnp.where(lane == n, tab_ref[g * hb + h, n], qf)
        qf = qf.astype(BF16)
        qa_ref[h, 0:tq, 0:LANES] = jnp.where(lane < HEAD, q, zero)
        qa_ref[h, tq:2 * tq, 0:LANES] = jnp.where(lane >= HEAD, q, zero)
        qa_ref[h, 0:tq, LANES:2 * LANES] = qf
        qa_ref[h, tq:2 * tq, LANES:2 * LANES] = qf
        m_ref[h] = jnp.full((1, 2 * tq), -1e30, F32)
        acc_ref[h] = jnp.zeros((VT_ROWS, 2 * tq), F32)

    def chunk(c, masked):
        start = pl.multiple_of(c * tk, tk)
        ss = [_dot_nt(ka_ref[h, pl.ds(start, tk), :], qa_ref[h]) for h in heads]
        if masked:
            key = start + lax.broadcasted_iota(jnp.int32, (tk, 2 * tq), 0)
            qcol = lax.broadcasted_iota(jnp.int32, (tk, 2 * tq), 1)
            qpos = i * tq + jnp.where(qcol >= tq, qcol - tq, qcol)
            keep = key <= qpos
            ss = [jnp.where(keep, s, -jnp.inf) for s in ss]
        m_olds = [m_ref[h] for h in heads]
        m_news = [jnp.maximum(m_olds[h], jnp.max(ss[h], axis=0, keepdims=True)) for h in heads]
        ps = [jnp.exp2(ss[h] - m_news[h]).astype(BF16) for h in heads]
        alphas = [jnp.exp2(m_olds[h] - m_news[h]) for h in heads]
        pvs = [_dot(vt_ref[h * nck + c], ps[h]) for h in heads]
        for h in heads:
            m_ref[h] = m_news[h]
            acc_ref[h] = alphas[h] * acc_ref[h] + pvs[h]

    n_full = i // (tk // tq)

    def body(c, carry):
        chunk(c, False)
        return carry

    lax.fori_loop(0, n_full, body, 0)
    chunk(n_full, True)

    lam = (jnp.exp(jnp.sum(lq1_ref[...] * lk1_ref[...], axis=-1, keepdims=True))
           - jnp.exp(jnp.sum(lq2_ref[...] * lk2_ref[...], axis=-1, keepdims=True)) + lambda_init)
    for h in heads:
        acc = acc_ref[h]
        rl = 1.0 / acc[LANES:LANES + 1, :]
        ot = (acc[0:LANES, 0:tq] * rl[:, 0:tq]
              - lam * (acc[0:LANES, tq:2 * tq] * rl[:, tq:2 * tq]))
        o = ot.T
        o = o * lax.rsqrt(jnp.mean(o * o, axis=-1, keepdims=True) + ATTN_NORM_EPS) * ng_ref[...]
        o_ref[:, hs(h)] = (o * (1.0 - lambda_init)).astype(o_ref.dtype)


def diff_attn(pa, tab, feat, lq1, lk1, lq2, lk2, norm_g, *, B, S, H, lambda_init, hb=8):
    hb = min(hb, H)
    tq, tk = ATT_TQ, min(ATT_TK, S)
    assert S % tk == 0 and H % hb == 0 and tk % tq == 0
    nq, nck, ng = S // tq, S // tk, H // hb
    wb = hb * LANES
    small = lambda n: pl.BlockSpec((1, n), lambda b, g, i: (0, 0))
    return pl.pallas_call(
        functools.partial(_diff_attn_kernel, hb=hb, nck=nck, lambda_init=lambda_init),
        grid=(B, ng, nq),
        in_specs=[
            pl.BlockSpec(memory_space=pltpu.SMEM),
            pl.BlockSpec((tq, wb), lambda b, g, i: (b * nq + i, g)),
            pl.BlockSpec((S, wb), lambda b, g, i: (b, ng + g)),
            pl.BlockSpec((S, wb), lambda b, g, i: (b, 2 * ng + g)),
            pl.BlockSpec((S, LANES), lambda b, g, i: (0, 0)),
            small(HEAD), small(HEAD), small(HEAD), small(HEAD), small(2 * HEAD),
        ],
        out_specs=pl.BlockSpec((tq, wb), lambda b, g, i: (b * nq + i, g)),
        out_shape=jax.ShapeDtypeStruct((B * S, H * LANES), BF16),
        scratch_shapes=[
            pltpu.VMEM((hb, S, 2 * LANES), BF16),
            pltpu.VMEM((hb * nck, VT_ROWS, tk), BF16),
            pltpu.VMEM((hb, 2 * tq, 2 * LANES), BF16),
            pltpu.VMEM((hb, VT_ROWS, 2 * tq), F32),
            pltpu.VMEM((hb, 1, 2 * tq), F32),
        ],
        compiler_params=_cparams(("parallel", "parallel", "arbitrary")),
    )(tab, pa, pa, pa, feat, lq1, lk1, lq2, lk2, norm_g)


def _same_head(n):
    r = lax.broadcasted_iota(jnp.int32, (n, n), 0) // HEAD
    c = lax.broadcasted_iota(jnp.int32, (n, n), 1) // HEAD
    return r == c


def _head_ones(n):
    return jnp.where(_same_head(n), 1.0, 0.0).astype(BF16)


def _head_sum(x, ones, split=True):
    hi = x.astype(BF16)
    if not split:
        return _dot(hi, ones)
    lo = (x - hi.astype(F32)).astype(BF16)
    return _dot(hi, ones) + _dot(lo, ones)


RWKV_TAIL = 3 * LANES


def _proj_rwkv_kernel(x_ref, wt_ref, mu_ref, w0_ref, a0_ref, kk_ref, ka_ref, w12_ref, g2_ref,
                      r_ref, lw_ref, k2_ref, v_ref, kkn_ref, b_ref, g_ref, wb_ref, pr_ref, prev_ref,
                      *, n_wt, first_tile, n_cols, width, tiles_per_seq):
    s = pl.program_id(0)

    @pl.when(s == 0)
    def _():
        prev_ref[...] = jnp.zeros_like(prev_ref)

    @pl.when(s < n_wt)
    def _():
        _stage_weight_tile(wt_ref, wb_ref, s, first_tile, n_cols)

    @pl.when(s >= n_wt)
    def _():
        tm = x_ref.shape[0]
        xb = x_ref[...].astype(BF16)
        c3 = 3 * width
        for c0, c1 in ((c3, c3 + RWKV_TAIL), (0, width), (2 * width, c3), (width, 2 * width)):
            pr_ref[:, c0:c1] = _dot_nt(xb, wb_ref[c0:c1, :])

        first = (s - n_wt) % tiles_per_seq == 0
        row = lax.broadcasted_iota(jnp.int32, (tm, 1), 0)

        def mixed(c0, c1):
            x = pr_ref[:, c0:c1]
            prow = jnp.where(first, 0.0, prev_ref[0:1, c0:c1])
            xs = jnp.where(row == 0, prow, pltpu.roll(x, 1, axis=0))
            return x + mu_ref[:, c0:c1] * (xs - x)

        lora_in = mixed(c3, c3 + LANES)
        lane = lax.broadcasted_iota(jnp.int32, lora_in.shape, 1)
        lora_in = jnp.where(lane < HEAD, jnp.tanh(lora_in), lora_in)
        z = _dot(lora_in.astype(BF16), w12_ref[...])
        g_ref[...] = _dot(jax.nn.sigmoid(mixed(c3 + LANES, c3 + RWKV_TAIL)).astype(BF16), g2_ref[...])

        lw_ref[...] = -math.exp(-0.5) * jax.nn.sigmoid(w0_ref[...] + z[:, 0:width])
        a = jax.nn.sigmoid(a0_ref[...] + z[:, width:2 * width])

        r_ref[...] = mixed(0, width)
        v_ref[...] = mixed(2 * width, c3)
        k = mixed(width, 2 * width)
        k2_ref[...] = k * (1.0 + (a - 1.0) * ka_ref[...])

        kx = k * kk_ref[...]
        gw = min(2 * LANES, width)
        ones = _head_ones(gw)
        ss = jnp.concatenate(
            [_head_sum(kx[:, t * gw:(t + 1) * gw] * kx[:, t * gw:(t + 1) * gw], ones)
             for t in range(width // gw)], axis=1)
        kkn = kx / jnp.maximum(jnp.sqrt(ss), 1e-12)
        kkn_ref[...] = kkn
        b_ref[...] = kkn * a
        prev_ref[0:1, :] = pr_ref[tm - 1:tm, :]


def proj_rwkv(x, wt, mu, w0, a0, k_k, k_a, w12, g2p, *, S, col0, width, tm=256, tile=384):
    T, D = x.shape
    tm = min(tm, S)
    W = 3 * width + RWKV_TAIL
    tile = _pick_tile(tile, col0, W)
    assert S % tm == 0
    n_wt, first_tile = W // tile, col0 // tile
    tok = lambda s: (jnp.maximum(s - n_wt, 0), 0)
    row = lambda n: pl.BlockSpec((1, n), lambda s: (0, 0))
    out = pl.BlockSpec((tm, width), tok)
    return pl.pallas_call(
        functools.partial(_proj_rwkv_kernel, n_wt=n_wt, first_tile=first_tile, n_cols=wt.shape[0],
                          width=width, tiles_per_seq=S // tm),
        grid=(n_wt + T // tm,),
        in_specs=[
            pl.BlockSpec((tm, D), tok),
            pl.BlockSpec((tile, D), lambda s: (first_tile + jnp.minimum(s, n_wt - 1), 0)),
            row(W), row(width), row(width), row(width), row(width),
            pl.BlockSpec((LANES, 2 * width), lambda s: (0, 0)),
            pl.BlockSpec((2 * LANES, width), lambda s: (0, 0)),
        ],
        out_specs=[out] * 7,
        out_shape=[jax.ShapeDtypeStruct((T, width), F32)] * 7,
        scratch_shapes=[pltpu.VMEM((W, D), BF16), pltpu.VMEM((tm, W), F32), pltpu.VMEM((8, W), F32)],
        compiler_params=_cparams(("arbitrary",)),
    )(x, wt, mu, w0, a0, k_k, k_a, w12, g2p)


GROUP = 2


def _block_diag(x):
    head = lax.broadcasted_iota(jnp.int32, x.shape, 1) // HEAD
    zero = jnp.zeros_like(x)
    return jnp.concatenate([jnp.where(head == h, x, zero) for h in range(x.shape[1] // HEAD)], axis=0)


def _chunk_step(rs, lws, k2s, vs, kks, bs, sts):
    C = CHUNK
    GL = rs[0].shape[1]
    idx = range(len(rs))
    row = lax.broadcasted_iota(jnp.int32, (C, GL), 0)
    tcol = lax.broadcasted_iota(jnp.int32, (C, GL), 1) % HEAD
    strict = row > tcol
    incl = row >= tcol
    eye = jnp.where(row == tcol, 1.0, 0.0)
    bdb = lambda x: _block_diag(x).astype(BF16)

    cums, tots, Mbs, Mks, ARSs = [], [], [], [], []
    for i in idx:
        cm = lws[i]
        sh = 1
        while sh < C:
            cm = cm + jnp.where(row >= sh, pltpu.roll(cm, sh, axis=0), 0.0)
            sh *= 2
        e_neg = jnp.exp(-cm)
        AR = jnp.concatenate([-kks[i] * jnp.exp(cm - lws[i]), rs[i] * jnp.exp(cm)], axis=0).astype(BF16)
        cums.append(cm)
        tots.append(cm[C - 1:C, :])
        Mbs.append(_dot_nt(AR, bdb(bs[i] * e_neg)))
        Mks.append(_dot_nt(AR, bdb(k2s[i] * e_neg)))
        ARSs.append(_dot_nt(AR, sts[i].astype(BF16)))
    vbds = [bdb(v) for v in vs]

    Ls = [jnp.where(strict, Mbs[i][0:C], 0.0) for i in idx]
    Ps = [eye + L for L in Ls]
    Lps = [_dot(L.astype(BF16), bdb(L)) for L in Ls]
    Xs = [ARSs[i][0:C] + _dot(jnp.where(strict, Mks[i][0:C], 0.0).astype(BF16), vbds[i]) for i in idx]
    m = 2
    while 2 * m < C:
        both = [_dot(jnp.concatenate([Lps[i], Ps[i]], axis=0).astype(BF16), bdb(Lps[i])) for i in idx]
        Ps = [Ps[i] + both[i][C:2 * C] for i in idx]
        Lps = [bt[0:C] for bt in both]
        m *= 2
    Ps = [Ps[i] + _dot(Ps[i].astype(BF16), bdb(Lps[i])) for i in idx]

    Us = [_dot(Ps[i].astype(BF16), bdb(Xs[i])) for i in idx]
    Ys = [ARSs[i][C:2 * C]
          + _dot(jnp.concatenate([jnp.where(incl, Mbs[i][C:2 * C], 0.0),
                                  jnp.where(incl, Mks[i][C:2 * C], 0.0)], axis=1).astype(BF16),
                 jnp.concatenate([bdb(Us[i]), vbds[i]], axis=0)) for i in idx]
    same_head = _same_head(GL)
    st_news = []
    for i in idx:
        e_rem = jnp.exp(tots[i] - cums[i])
        UV = jnp.concatenate([Us[i], vs[i]], axis=0).astype(BF16)
        BK = jnp.concatenate([bs[i] * e_rem, k2s[i] * e_rem], axis=0).astype(BF16)
        st_news.append(jnp.where(same_head, sts[i] * jnp.exp(tots[i]) + _dot_tn(UV, BK), 0.0))
    return Ys, st_news


def _rwkv_chunk_kernel(r_ref, lw_ref, k2_ref, v_ref, kk_ref, b_ref, g_ref, rk_ref, gng_ref, gnb_ref,
                       o_ref, st_ref, *, nbatch, ngroups):
    c = pl.program_id(2)

    @pl.when(c == 0)
    def _():
        st_ref[...] = jnp.zeros_like(st_ref)

    GL = st_ref.shape[1]
    ones = _head_ones(GL)
    chains = [(bi, p) for bi in range(nbatch) for p in range(ngroups)]
    sl = lambda p: slice(p * GL, (p + 1) * GL)
    get = lambda ref: [ref[bi, :, sl(p)] for bi, p in chains]
    rs, k2s, vs = get(r_ref), get(k2_ref), get(v_ref)
    ys, st_news = _chunk_step(rs, get(lw_ref), k2s, vs, get(kk_ref), get(b_ref),
                              [st_ref[i] for i in range(len(chains))])
    for i in range(len(chains)):
        st_ref[i] = st_news[i]
    ds = [y - _head_sum(y, ones, split=False) * (1.0 / HEAD) for y in ys]
    vars_ = [_head_sum(d * d, ones, split=False) * (1.0 / HEAD) for d in ds]
    for i, (bi, p) in enumerate(chains):
        yn = ds[i] * lax.rsqrt(vars_[i] + GN_EPS) * gng_ref[:, sl(p)] + gnb_ref[:, sl(p)]
        bonus = _head_sum(rs[i] * k2s[i] * rk_ref[:, sl(p)], ones, split=False) * vs[i]
        o_ref[bi, :, sl(p)] = ((yn + bonus) * g_ref[bi, :, sl(p)]).astype(o_ref.dtype)


def rwkv_chunk(r, lw, k2, v, kk, b, g, r_k, gn_g, gn_b, *, nbatch=2, ngroups=8):
    B, S, W = r.shape
    GL = min(GROUP * HEAD, W)
    ngroups = min(ngroups, W // GL)
    nbatch = min(nbatch, B)
    wb = ngroups * GL
    seq = pl.BlockSpec((nbatch, CHUNK, wb), lambda bi, p, c: (bi, c, p))
    par = pl.BlockSpec((1, wb), lambda bi, p, c: (0, p))
    return pl.pallas_call(
        functools.partial(_rwkv_chunk_kernel, nbatch=nbatch, ngroups=ngroups),
        grid=(B // nbatch, W // wb, S // CHUNK),
        in_specs=[seq] * 7 + [par] * 3,
        out_specs=seq,
        out_shape=jax.ShapeDtypeStruct((B, S, W), BF16),
        scratch_shapes=[pltpu.VMEM((nbatch * ngroups, GL, GL), F32)],
        compiler_params=_cparams(("parallel", "parallel", "arbitrary")),
    )(r, lw, k2, v, kk, b, g, r_k, gn_g, gn_b)


def _out_ln_kernel(x_ref, oa_ref, or_ref, wa_ref, wr_ref, g_ref, b_ref, o_ref, *, alpha):
    half = x_ref.shape[0] // 2
    rows = [slice(0, half), slice(half, 2 * half)]
    mixes = [_dot(oa_ref[r, :], wa_ref[...]) + _dot(or_ref[r, :], wr_ref[...]) for r in rows]
    for r, mix in zip(rows, mixes):
        o_ref[r, :] = _layer_norm(alpha * x_ref[r, :] + mix, g_ref[...], b_ref[...])


def out_ln(x, oa, orw, wa, wr, g, b, *, alpha, tm=512):
    T, D = x.shape
    tm = min(tm, T)
    ka, kr = oa.shape[1], orw.shape[1]
    return pl.pallas_call(
        functools.partial(_out_ln_kernel, alpha=alpha),
        grid=(T // tm,),
        in_specs=[
            pl.BlockSpec((tm, D), lambda i: (i, 0)),
            pl.BlockSpec((tm, ka), lambda i: (i, 0)),
            pl.BlockSpec((tm, kr), lambda i: (i, 0)),
            pl.BlockSpec((ka, D), lambda i: (0, 0)),
            pl.BlockSpec((kr, D), lambda i: (0, 0)),
            pl.BlockSpec((1, D), lambda i: (0, 0)),
            pl.BlockSpec((1, D), lambda i: (0, 0)),
        ],
        out_specs=pl.BlockSpec((tm, D), lambda i: (i, 0)),
        out_shape=jax.ShapeDtypeStruct((T, D), F32),
        compiler_params=_cparams(("parallel",)),
    )(x, oa, orw, wa, wr, g, b)


def _layer(x, l, ffn1_w_gate, ffn1_w_up, ffn1_w_down, ln1_g, ln1_b, w_in,
           lambda_q1, lambda_k1, lambda_q2, lambda_k2, attn_norm_g,
           rwkv_mu, rwkv_w0, rwkv_w2, rwkv_a0, rwkv_a2, rwkv_g2,
           rwkv_k_k, rwkv_k_a, rwkv_r_k, rwkv_gn_g, rwkv_gn_b,
           w_out, ln2_g, ln2_b, ffn2_w_gate, ffn2_w_up, ffn2_w_down, ln3_g, ln3_b):
    B, S, D = x.shape
    T = B * S
    alpha = (2.0 * DEPTH) ** 0.25
    lambda_init = 0.8 - 0.6 * math.exp(-0.3 * l)
    rw = rwkv_w0.shape[-1]
    aw = w_out.shape[1] - rw
    H = aw // LANES
    n_wd, n_ad, n_gd = rwkv_w2.shape[1], rwkv_a2.shape[1], rwkv_g2.shape[1]
    assert n_wd == HEAD and n_ad == HEAD and n_gd <= 2 * LANES
    row = lambda a: a.reshape(1, -1)
    bf = lambda a: a.astype(BF16)

    x1 = ffn_ln(x.reshape(T, D), ffn1_w_gate[l], ffn1_w_up[l], ffn1_w_down[l],
                row(ln1_g[l]), row(ln1_b[l]), alpha=alpha)

    wi = jnp.swapaxes(w_in[l], 0, 1)
    p_attn = proj_attn(x1, wi, n_cols=3 * aw, n_scaled_cols=aw, scale=HEAD ** -0.5 * math.log2(math.e))
    tab, feat = alibi_tables(H, S)
    o_attn = diff_attn(p_attn, tab, feat, row(lambda_q1[l]), row(lambda_k1[l]), row(lambda_q2[l]),
                       row(lambda_k2[l]), row(attn_norm_g[l]), B=B, S=S, H=H, lambda_init=lambda_init)

    mu = jnp.pad(rwkv_mu[l], (0, RWKV_TAIL - (n_wd + n_ad + n_gd)))
    w12 = jnp.zeros((LANES, 2 * rw), F32)
    w12 = w12.at[:HEAD, :rw].set(rwkv_w2[l]).at[HEAD:, rw:].set(rwkv_a2[l])
    g2p = jnp.pad(rwkv_g2[l], ((0, 2 * LANES - n_gd), (0, 0)))
    seqs = proj_rwkv(x1, wi, row(mu), row(rwkv_w0[l]), row(rwkv_a0[l]), row(rwkv_k_k[l]), row(rwkv_k_a[l]),
                     bf(w12), bf(g2p), S=S, col0=3 * aw, width=rw)
    r, lw, k2, v, kk, b, g = [t.reshape(B, S, rw) for t in seqs]
    o_rwkv = rwkv_chunk(r, lw, k2, v, kk, b, g, row(rwkv_r_k[l]), row(rwkv_gn_g[l]), row(rwkv_gn_b[l]))

    wo = w_out[l]
    x2 = out_ln(x1, o_attn, o_rwkv.reshape(T, rw), bf(wo[:aw]), bf(wo[aw:]),
                row(ln2_g[l]), row(ln2_b[l]), alpha=alpha)
    x3 = ffn_ln(x2, ffn2_w_gate[l], ffn2_w_up[l], ffn2_w_down[l],
                row(ln3_g[l]), row(ln3_b[l]), alpha=alpha)
    return x3.reshape(B, S, D)


def kernel(x, ffn1_w_gate, ffn1_w_up, ffn1_w_down, ln1_g, ln1_b, w_in, lambda_q1, lambda_k1, lambda_q2, lambda_k2, attn_norm_g, rwkv_mu, rwkv_w0, rwkv_w2, rwkv_a0, rwkv_a2, rwkv_g2, rwkv_k_k, rwkv_k_a, rwkv_r_k, rwkv_gn_g, rwkv_gn_b, w_out, ln2_g, ln2_b, ffn2_w_gate, ffn2_w_up, ffn2_w_down, ln3_g, ln3_b):
    for l in range(DEPTH):
        x = _layer(x, l, ffn1_w_gate, ffn1_w_up, ffn1_w_down, ln1_g, ln1_b, w_in,
                   lambda_q1, lambda_k1, lambda_q2, lambda_k2, attn_norm_g,
                   rwkv_mu, rwkv_w0, rwkv_w2, rwkv_a0, rwkv_a2, rwkv_g2,
                   rwkv_k_k, rwkv_k_a, rwkv_r_k, rwkv_gn_g, rwkv_gn_b,
                   w_out, ln2_g, ln2_b, ffn2_w_gate, ffn2_w_up, ffn2_w_down, ln3_g, ln3_b)
    return x
```

```python
import functools
import math

import jax
import jax.numpy as jnp
from jax import lax
from jax.experimental import pallas as pl
from jax.experimental.pallas import tpu as pltpu

F32 = jnp.float32
BF16 = jnp.bfloat16

DEPTH = 1
LN_EPS = 1e-5
ATTN_NORM_EPS = 1e-5
GN_EPS = 64e-5
HEAD = 64
LANES = 128
CHUNK = 64
VMEM_LIMIT = 56 * 1024 * 1024


def _cparams(sem):
    return pltpu.CompilerParams(dimension_semantics=sem, vmem_limit_bytes=VMEM_LIMIT)


def _layer_norm(y, g, b):
    mu = jnp.mean(y, axis=-1, keepdims=True)
    d = y - mu
    var = jnp.mean(d * d, axis=-1, keepdims=True)
    return d * lax.rsqrt(var + LN_EPS) * g + b


def _dot(a, b):
    return jnp.dot(a, b, preferred_element_type=F32)


def _dot_nt(a, b):
    return lax.dot_general(a, b, (((1,), (1,)), ((), ())), preferred_element_type=F32)


def _dot_tn(a, b):
    return lax.dot_general(a, b, (((0,), (0,)), ((), ())), preferred_element_type=F32)


def _ffn_ln_kernel(x_ref, wg_ref, wu_ref, wd_ref, g_ref, b_ref, o_ref, xb_ref, *, alpha, ncol):
    j = pl.program_id(1)
    nj = pl.num_programs(1)
    tm, D = o_ref.shape
    cols = [slice(n * ncol, (n + 1) * ncol) for n in range(D // ncol)]

    def body(first, final):
        nrow = 2 if (first or final) else 1
        rows = [slice(r * tm // nrow, (r + 1) * tm // nrow) for r in range(nrow)]
        if first:
            for r in rows:
                x = x_ref[r, :]
                xb_ref[r, :] = x.astype(BF16)
                o_ref[r, :] = alpha * x
        wg = wg_ref[...].astype(BF16)
        wu = wu_ref[...].astype(BF16)
        wd = [wd_ref[:, c].astype(BF16) for c in cols]
        hs = []
        for r in rows:
            xb = xb_ref[r, :]
            hg = _dot(xb, wg)
            hu = _dot(xb, wu)
            hs.append((0.5 * hg * jax.nn.sigmoid(hg) * hu).astype(BF16))
        for r, h in zip(rows, hs):
            for c, w in zip(cols, wd):
                o_ref[r, c] += _dot(h, w)
        if final:
            for r in rows:
                o_ref[r, :] = _layer_norm(o_ref[r, :], g_ref[...], b_ref[...])

    pl.when(j == 0)(lambda: body(True, False))
    pl.when(jnp.logical_and(j > 0, j < nj - 1))(lambda: body(False, False))
    pl.when(jnp.logical_and(j > 0, j == nj - 1))(lambda: body(False, True))


def ffn_ln(x, wg, wu, wd, g, b, *, alpha, tm=1024, tf=256, ncol=512):
    T, D = x.shape
    Fd = wg.shape[1]
    tm = min(tm, T)
    tf = min(tf, Fd)
    ncol = min(ncol, D)
    assert Fd // tf >= 2
    return pl.pallas_call(
        functools.partial(_ffn_ln_kernel, alpha=alpha, ncol=ncol),
        grid=(T // tm, Fd // tf),
        in_specs=[
            pl.BlockSpec((tm, D), lambda i, j: (i, 0)),
            pl.BlockSpec((D, tf), lambda i, j: (0, j)),
            pl.BlockSpec((D, tf), lambda i, j: (0, j)),
            pl.BlockSpec((tf, D), lambda i, j: (j, 0)),
            pl.BlockSpec((1, D), lambda i, j: (0, 0)),
            pl.BlockSpec((1, D), lambda i, j: (0, 0)),
        ],
        out_specs=pl.BlockSpec((tm, D), lambda i, j: (i, 0)),
        out_shape=jax.ShapeDtypeStruct((T, D), F32),
        scratch_shapes=[pltpu.VMEM((tm, D), BF16)],
        compiler_params=_cparams(("parallel", "arbitrary")),
    )(x, wg, wu, wd, g, b)


def _pick_tile(limit, *sizes):
    t = limit // LANES * LANES
    while any(s % t for s in sizes):
        t -= LANES
    return t


def _stage_weight_tile(wt_ref, wb_ref, s, first_tile, n_cols):
    tile = wt_ref.shape[0]
    col = (first_tile + s) * tile + lax.broadcasted_iota(jnp.int32, wt_ref.shape, 0)
    wb_ref[pl.ds(pl.multiple_of(s * tile, tile), tile), :] = jnp.where(col < n_cols, wt_ref[...], 0.0).astype(BF16)


def _proj_attn_kernel(x_ref, wt_ref, o_ref, wb_ref, *, n_wt, n_cols, n_scaled, scale):
    s = pl.program_id(0)

    @pl.when(s < n_wt)
    def _():
        _stage_weight_tile(wt_ref, wb_ref, s, 0, n_cols)

    @pl.when(s >= n_wt)
    def _():
        tile = wt_ref.shape[0]
        xb = x_ref[...].astype(BF16)
        for n in range(n_wt):
            acc = _dot_nt(xb, wb_ref[n * tile:(n + 1) * tile, :])
            if n < n_scaled:
                acc = acc * scale
            o_ref[:, n * tile:(n + 1) * tile] = acc.astype(o_ref.dtype)


def proj_attn(x, wt, *, n_cols, n_scaled_cols, scale, tm=512, tile=512):
    T, D = x.shape
    tm = min(tm, T)
    tile = _pick_tile(tile, n_cols, n_scaled_cols)
    n_wt = n_cols // tile
    tok = lambda s: (jnp.maximum(s - n_wt, 0), 0)
    return pl.pallas_call(
        functools.partial(_proj_attn_kernel, n_wt=n_wt, n_cols=wt.shape[0],
                          n_scaled=n_scaled_cols // tile, scale=scale),
        grid=(n_wt + T // tm,),
        in_specs=[
            pl.BlockSpec((tm, D), tok),
            pl.BlockSpec((tile, D), lambda s: (jnp.minimum(s, n_wt - 1), 0)),
        ],
        out_specs=pl.BlockSpec((tm, n_cols), tok),
        out_shape=jax.ShapeDtypeStruct((T, n_cols), BF16),
        scratch_shapes=[pltpu.VMEM((n_cols, D), BF16)],
        compiler_params=_cparams(("arbitrary",)),
    )(x, wt)


ATT_TQ = 256
ATT_TK = 512
N_BIAS_TERMS = 3
BF16_ROWS = 16
VT_ROWS = LANES + BF16_ROWS


def alibi_tables(H, S):
    slope = jnp.exp2(-8.0 * (jnp.arange(H, dtype=F32) + 1.0) / H) * math.log2(math.e)
    terms, rest = [], slope
    for _ in range(N_BIAS_TERMS):
        t = rest.astype(BF16).astype(F32)
        terms += [64.0 * t, t]
        rest = rest - t
    tab = jnp.stack(terms + [jnp.zeros_like(slope)] * (8 - len(terms)), axis=1)
    pos = jnp.arange(S, dtype=jnp.int32)
    hi, lo = (pos // 64).astype(F32), (pos % 64).astype(F32)
    feat = jnp.stack([hi, lo] * N_BIAS_TERMS, axis=1)
    feat = jnp.pad(feat, ((0, 0), (0, LANES - feat.shape[1]))).astype(BF16)
    return tab, feat


def _diff_attn_kernel(tab_ref, q_ref, k_ref, v_ref, feat_ref, lq1_ref, lk1_ref, lq2_ref, lk2_ref, ng_ref,
                      o_ref, ka_ref, vt_ref, qa_ref, acc_ref, m_ref, *, hb, nck, lambda_init):
    tq = ATT_TQ
    tk = vt_ref.shape[2]
    g = pl.program_id(1)
    i = pl.program_id(2)
    heads = range(hb)
    hs = lambda h: slice(h * LANES, (h + 1) * LANES)

    @pl.when(i == 0)
    def _():
        for h in heads:
            ka_ref[h, :, 0:LANES] = k_ref[:, hs(h)]
            ka_ref[h, :, LANES:2 * LANES] = feat_ref[...]
            for c in range(nck):
                vt_ref[h * nck + c, 0:LANES, :] = v_ref[c * tk:(c + 1) * tk, hs(h)].astype(F32).T.astype(BF16)
                vt_ref[h * nck + c, LANES:VT_ROWS, :] = jnp.ones((BF16_ROWS, tk), BF16)

    lane = lax.broadcasted_iota(jnp.int32, (tq, LANES), 1)
    for h in heads:
        q = q_ref[:, hs(h)]
        zero = jnp.zeros_like(q)
        qf = jnp.zeros((tq, LANES), F32)
        for n in range(2 * N_BIAS_TERMS):
            qf = jnp.where(lane == n, tab_ref[g * hb + h, n], qf)
        qf = qf.astype(BF16)
        qa_ref[h, 0:tq, 0:LANES] = jnp.where(lane < HEAD, q, zero)
        qa_ref[h, tq:2 * tq, 0:LANES] = jnp.where(lane >= HEAD, q, zero)
        qa_ref[h, 0:tq, LANES:2 * LANES] = qf
        qa_ref[h, tq:2 * tq, LANES:2 * LANES] = qf
        m_ref[h] = jnp.full((1, 2 * tq), -1e30, F32)
        acc_ref[h] = jnp.zeros((VT_ROWS, 2 * tq), F32)

    def chunk(c, masked):
        start = pl.multiple_of(c * tk, tk)
        ss = [_dot_nt(ka_ref[h, pl.ds(start, tk), :], qa_ref[h]) for h in heads]
        if masked:
            key = start + lax.broadcasted_iota(jnp.int32, (tk, 2 * tq), 0)
            qcol = lax.broadcasted_iota(jnp.int32, (tk, 2 * tq), 1)
            qpos = i * tq + jnp.where(qcol >= tq, qcol - tq, qcol)
            keep = key <= qpos
            ss = [jnp.where(keep, s, -jnp.inf) for s in ss]
        m_olds = [m_ref[h] for h in heads]
        m_news = [jnp.maximum(m_olds[h], jnp.max(ss[h], axis=0, keepdims=True)) for h in heads]
        ps = [jnp.exp2(ss[h] - m_news[h]).astype(BF16) for h in heads]
        alphas = [jnp.exp2(m_olds[h] - m_news[h]) for h in heads]
        pvs = [_dot(vt_ref[h * nck + c], ps[h]) for h in heads]
        for h in heads:
            m_ref[h] = m_news[h]
            acc_ref[h] = alphas[h] * acc_ref[h] + pvs[h]

    n_full = i // (tk // tq)

    def body(c, carry):
        chunk(c, False)
        return carry

    lax.fori_loop(0, n_full, body, 0)
    chunk(n_full, True)

    lam = (jnp.exp(jnp.sum(lq1_ref[...] * lk1_ref[...], axis=-1, keepdims=True))
           - jnp.exp(jnp.sum(lq2_ref[...] * lk2_ref[...], axis=-1, keepdims=True)) + lambda_init)
    for h in heads:
        acc = acc_ref[h]
        rl = 1.0 / acc[LANES:LANES + 1, :]
        ot = (acc[0:LANES, 0:tq] * rl[:, 0:tq]
              - lam * (acc[0:LANES, tq:2 * tq] * rl[:, tq:2 * tq]))
        o = ot.T
        o = o * lax.rsqrt(jnp.mean(o * o, axis=-1, keepdims=True) + ATTN_NORM_EPS) * ng_ref[...]
        o_ref[:, hs(h)] = (o * (1.0 - lambda_init)).astype(o_ref.dtype)


def diff_attn(pa, tab, feat, lq1, lk1, lq2, lk2, norm_g, *, B, S, H, lambda_init, hb=8):
    hb = min(hb, H)
    tq, tk = ATT_TQ, min(ATT_TK, S)
    assert S % tk == 0 and H % hb == 0 and tk % tq == 0
    nq, nck, ng = S // tq, S // tk, H // hb
    wb = hb * LANES
    small = lambda n: pl.BlockSpec((1, n), lambda b, g, i: (0, 0))
    return pl.pallas_call(
        functools.partial(_diff_attn_kernel, hb=hb, nck=nck, lambda_init=lambda_init),
        grid=(B, ng, nq),
        in_specs=[
            pl.BlockSpec(memory_space=pltpu.SMEM),
            pl.BlockSpec((tq, wb), lambda b, g, i: (b * nq + i, g)),
            pl.BlockSpec((S, wb), lambda b, g, i: (b, ng + g)),
            pl.BlockSpec((S, wb), lambda b, g, i: (b, 2 * ng + g)),
            pl.BlockSpec((S, LANES), lambda b, g, i: (0, 0)),
            small(HEAD), small(HEAD), small(HEAD), small(HEAD), small(2 * HEAD),
        ],
        out_specs=pl.BlockSpec((tq, wb), lambda b, g, i: (b * nq + i, g)),
        out_shape=jax.ShapeDtypeStruct((B * S, H * LANES), BF16),
        scratch_shapes=[
            pltpu.VMEM((hb, S, 2 * LANES), BF16),
            pltpu.VMEM((hb * nck, VT_ROWS, tk), BF16),
            pltpu.VMEM((hb, 2 * tq, 2 * LANES), BF16),
            pltpu.VMEM((hb, VT_ROWS, 2 * tq), F32),
            pltpu.VMEM((hb, 1, 2 * tq), F32),
        ],
        compiler_params=_cparams(("parallel", "parallel", "arbitrary")),
    )(tab, pa, pa, pa, feat, lq1, lk1, lq2, lk2, norm_g)


def _same_head(n):
    r = lax.broadcasted_iota(jnp.int32, (n, n), 0) // HEAD
    c = lax.broadcasted_iota(jnp.int32, (n, n), 1) // HEAD
    return r == c


def _head_ones(n):
    return jnp.where(_same_head(n), 1.0, 0.0).astype(BF16)


def _head_sum(x, ones, split=True):
    hi = x.astype(BF16)
    if not split:
        return _dot(hi, ones)
    lo = (x - hi.astype(F32)).astype(BF16)
    return _dot(hi, ones) + _dot(lo, ones)


RWKV_TAIL = 3 * LANES


def _proj_rwkv_kernel(x_ref, wt_ref, mu_ref, w0_ref, a0_ref, kk_ref, ka_ref, w12_ref, g2_ref,
                      r_ref, lw_ref, k2_ref, v_ref, kkn_ref, b_ref, g_ref, wb_ref, pr_ref, prev_ref,
                      *, n_wt, first_tile, n_cols, width, tiles_per_seq):
    s = pl.program_id(0)

    @pl.when(s == 0)
    def _():
        prev_ref[...] = jnp.zeros_like(prev_ref)

    @pl.when(s < n_wt)
    def _():
        _stage_weight_tile(wt_ref, wb_ref, s, first_tile, n_cols)

    @pl.when(s >= n_wt)
    def _():
        tm = x_ref.shape[0]
        xb = x_ref[...].astype(BF16)
        c3 = 3 * width
        for c0, c1 in ((c3, c3 + RWKV_TAIL), (0, width), (2 * width, c3), (width, 2 * width)):
            pr_ref[:, c0:c1] = _dot_nt(xb, wb_ref[c0:c1, :])

        first = (s - n_wt) % tiles_per_seq == 0
        row = lax.broadcasted_iota(jnp.int32, (tm, 1), 0)

        def mixed(c0, c1):
            x = pr_ref[:, c0:c1]
            prow = jnp.where(first, 0.0, prev_ref[0:1, c0:c1])
            xs = jnp.where(row == 0, prow, pltpu.roll(x, 1, axis=0))
            return x + mu_ref[:, c0:c1] * (xs - x)

        lora_in = mixed(c3, c3 + LANES)
        lane = lax.broadcasted_iota(jnp.int32, lora_in.shape, 1)
        lora_in = jnp.where(lane < HEAD, jnp.tanh(lora_in), lora_in)
        z = _dot(lora_in.astype(BF16), w12_ref[...])
        g_ref[...] = _dot(jax.nn.sigmoid(mixed(c3 + LANES, c3 + RWKV_TAIL)).astype(BF16), g2_ref[...])

        lw_ref[...] = -math.exp(-0.5) * jax.nn.sigmoid(w0_ref[...] + z[:, 0:width])
        a = jax.nn.sigmoid(a0_ref[...] + z[:, width:2 * width])

        r_ref[...] = mixed(0, width)
        v_ref[...] = mixed(2 * width, c3)
        k = mixed(width, 2 * width)
        k2_ref[...] = k * (1.0 + (a - 1.0) * ka_ref[...])

        kx = k * kk_ref[...]
        gw = min(2 * LANES, width)
        ones = _head_ones(gw)
        ss = jnp.concatenate(
            [_head_sum(kx[:, t * gw:(t + 1) * gw] * kx[:, t * gw:(t + 1) * gw], ones)
             for t in range(width // gw)], axis=1)
        kkn = kx / jnp.maximum(jnp.sqrt(ss), 1e-12)
        kkn_ref[...] = kkn
        b_ref[...] = kkn * a
        prev_ref[0:1, :] = pr_ref[tm - 1:tm, :]


def proj_rwkv(x, wt, mu, w0, a0, k_k, k_a, w12, g2p, *, S, col0, width, tm=256, tile=384):
    T, D = x.shape
    tm = min(tm, S)
    W = 3 * width + RWKV_TAIL
    tile = _pick_tile(tile, col0, W)
    assert S % tm == 0
    n_wt, first_tile = W // tile, col0 // tile
    tok = lambda s: (jnp.maximum(s - n_wt, 0), 0)
    row = lambda n: pl.BlockSpec((1, n), lambda s: (0, 0))
    out = pl.BlockSpec((tm, width), tok)
    return pl.pallas_call(
        functools.partial(_proj_rwkv_kernel, n_wt=n_wt, first_tile=first_tile, n_cols=wt.shape[0],
                          width=width, tiles_per_seq=S // tm),
        grid=(n_wt + T // tm,),
        in_specs=[
            pl.BlockSpec((tm, D), tok),
            pl.BlockSpec((tile, D), lambda s: (first_tile + jnp.minimum(s, n_wt - 1), 0)),
            row(W), row(width), row(width), row(width), row(width),
            pl.BlockSpec((LANES, 2 * width), lambda s: (0, 0)),
            pl.BlockSpec((2 * LANES, width), lambda s: (0, 0)),
        ],
        out_specs=[out] * 7,
        out_shape=[jax.ShapeDtypeStruct((T, width), F32)] * 7,
        scratch_shapes=[pltpu.VMEM((W, D), BF16), pltpu.VMEM((tm, W), F32), pltpu.VMEM((8, W), F32)],
        compiler_params=_cparams(("arbitrary",)),
    )(x, wt, mu, w0, a0, k_k, k_a, w12, g2p)


GROUP = 2


def _block_diag(x):
    head = lax.broadcasted_iota(jnp.int32, x.shape, 1) // HEAD
    zero = jnp.zeros_like(x)
    return jnp.concatenate([jnp.where(head == h, x, zero) for h in range(x.shape[1] // HEAD)], axis=0)


def _chunk_step(rs, lws, k2s, vs, kks, bs, sts):
    C = CHUNK
    GL = rs[0].shape[1]
    idx = range(len(rs))
    row = lax.broadcasted_iota(jnp.int32, (C, GL), 0)
    tcol = lax.broadcasted_iota(jnp.int32, (C, GL), 1) % HEAD
    strict = row > tcol
    incl = row >= tcol
    eye = jnp.where(row == tcol, 1.0, 0.0)
    bdb = lambda x: _block_diag(x).astype(BF16)

    cums, tots, Mbs, Mks, ARSs = [], [], [], [], []
    for i in idx:
        cm = lws[i]
        sh = 1
        while sh < C:
            cm = cm + jnp.where(row >= sh, pltpu.roll(cm, sh, axis=0), 0.0)
            sh *= 2
        e_neg = jnp.exp(-cm)
        AR = jnp.concatenate([-kks[i] * jnp.exp(cm - lws[i]), rs[i] * jnp.exp(cm)], axis=0).astype(BF16)
        cums.append(cm)
        tots.append(cm[C - 1:C, :])
        Mbk = _dot_nt(AR, jnp.concatenate([bdb(bs[i] * e_neg), bdb(k2s[i] * e_neg)], axis=0))
        Mbs.append(Mbk[:, 0:GL])
        Mks.append(Mbk[:, GL:2 * GL])
        ARSs.append(_dot_nt(AR, sts[i].astype(BF16)))
    vbds = [bdb(v) for v in vs]

    Ls = [jnp.where(strict, Mbs[i][0:C], 0.0) for i in idx]
    Ps = [eye + L for L in Ls]
    Lps = [_dot(L.astype(BF16), bdb(L)) for L in Ls]
    Xs = [ARSs[i][0:C] + _dot(jnp.where(strict, Mks[i][0:C], 0.0).astype(BF16), vbds[i]) for i in idx]
    m = 2
    while 2 * m < C:
        both = [_dot(jnp.concatenate([Lps[i], Ps[i]], axis=0).astype(BF16), bdb(Lps[i])) for i in idx]
        Ps = [Ps[i] + both[i][C:2 * C] for i in idx]
        Lps = [bt[0:C] for bt in both]
        m *= 2
    Ps = [Ps[i] + _dot(Ps[i].astype(BF16), bdb(Lps[i])) for i in idx]

    Us = [_dot(Ps[i].astype(BF16), bdb(Xs[i])) for i in idx]
    Ys = [ARSs[i][C:2 * C]
          + _dot(jnp.concatenate([jnp.where(incl, Mbs[i][C:2 * C], 0.0),
                                  jnp.where(incl, Mks[i][C:2 * C], 0.0)], axis=1).astype(BF16),
                 jnp.concatenate([bdb(Us[i]), vbds[i]], axis=0)) for i in idx]
    same_head = _same_head(GL)
    st_news = []
    for i in idx:
        e_rem = jnp.exp(tots[i] - cums[i])
        UV = jnp.concatenate([Us[i], vs[i]], axis=0).astype(BF16)
        BK = jnp.concatenate([bs[i] * e_rem, k2s[i] * e_rem], axis=0).astype(BF16)
        st_news.append(jnp.where(same_head, sts[i] * jnp.exp(tots[i]) + _dot_tn(UV, BK), 0.0))
    return Ys, st_news


def _rwkv_chunk_kernel(r_ref, lw_ref, k2_ref, v_ref, kk_ref, b_ref, g_ref, rk_ref, gng_ref, gnb_ref,
                       o_ref, st_ref, *, nbatch, ngroups):
    c = pl.program_id(2)

    @pl.when(c == 0)
    def _():
        st_ref[...] = jnp.zeros_like(st_ref)

    GL = st_ref.shape[1]
    ones = _head_ones(GL)
    chains = [(bi, p) for bi in range(nbatch) for p in range(ngroups)]
    sl = lambda p: slice(p * GL, (p + 1) * GL)
    get = lambda ref: [ref[bi, :, sl(p)] for bi, p in chains]
    rs, k2s, vs = get(r_ref), get(k2_ref), get(v_ref)
    ys, st_news = _chunk_step(rs, get(lw_ref), k2s, vs, get(kk_ref), get(b_ref),
                              [st_ref[i] for i in range(len(chains))])
    for i in range(len(chains)):
        st_ref[i] = st_news[i]
    ds = [y - _head_sum(y, ones, split=False) * (1.0 / HEAD) for y in ys]
    vars_ = [_head_sum(d * d, ones, split=False) * (1.0 / HEAD) for d in ds]
    for i, (bi, p) in enumerate(chains):
        yn = ds[i] * lax.rsqrt(vars_[i] + GN_EPS) * gng_ref[:, sl(p)] + gnb_ref[:, sl(p)]
        bonus = _head_sum(rs[i] * k2s[i] * rk_ref[:, sl(p)], ones, split=False) * vs[i]
        o_ref[bi, :, sl(p)] = ((yn + bonus) * g_ref[bi, :, sl(p)]).astype(o_ref.dtype)


def rwkv_chunk(r, lw, k2, v, kk, b, g, r_k, gn_g, gn_b, *, nbatch=2, ngroups=8):
    B, S, W = r.shape
    GL = min(GROUP * HEAD, W)
    ngroups = min(ngroups, W // GL)
    nbatch = min(nbatch, B)
    wb = ngroups * GL
    seq = pl.BlockSpec((nbatch, CHUNK, wb), lambda bi, p, c: (bi, c, p))
    par = pl.BlockSpec((1, wb), lambda bi, p, c: (0, p))
    return pl.pallas_call(
        functools.partial(_rwkv_chunk_kernel, nbatch=nbatch, ngroups=ngroups),
        grid=(B // nbatch, W // wb, S // CHUNK),
        in_specs=[seq] * 7 + [par] * 3,
        out_specs=seq,
        out_shape=jax.ShapeDtypeStruct((B, S, W), BF16),
        scratch_shapes=[pltpu.VMEM((nbatch * ngroups, GL, GL), F32)],
        compiler_params=_cparams(("parallel", "parallel", "arbitrary")),
    )(r, lw, k2, v, kk, b, g, r_k, gn_g, gn_b)


def _out_ln_kernel(x_ref, oa_ref, or_ref, wa_ref, wr_ref, g_ref, b_ref, o_ref, *, alpha):
    half = x_ref.shape[0] // 2
    rows = [slice(0, half), slice(half, 2 * half)]
    mixes = [_dot(oa_ref[r, :], wa_ref[...]) + _dot(or_ref[r, :], wr_ref[...]) for r in rows]
    for r, mix in zip(rows, mixes):
        o_ref[r, :] = _layer_norm(alpha * x_ref[r, :] + mix, g_ref[...], b_ref[...])


def out_ln(x, oa, orw, wa, wr, g, b, *, alpha, tm=512):
    T, D = x.shape
    tm = min(tm, T)
    ka, kr = oa.shape[1], orw.shape[1]
    return pl.pallas_call(
        functools.partial(_out_ln_kernel, alpha=alpha),
        grid=(T // tm,),
        in_specs=[
            pl.BlockSpec((tm, D), lambda i: (i, 0)),
            pl.BlockSpec((tm, ka), lambda i: (i, 0)),
            pl.BlockSpec((tm, kr), lambda i: (i, 0)),
            pl.BlockSpec((ka, D), lambda i: (0, 0)),
            pl.BlockSpec((kr, D), lambda i: (0, 0)),
            pl.BlockSpec((1, D), lambda i: (0, 0)),
            pl.BlockSpec((1, D), lambda i: (0, 0)),
        ],
        out_specs=pl.BlockSpec((tm, D), lambda i: (i, 0)),
        out_shape=jax.ShapeDtypeStruct((T, D), F32),
        compiler_params=_cparams(("parallel",)),
    )(x, oa, orw, wa, wr, g, b)


def _layer(x, l, ffn1_w_gate, ffn1_w_up, ffn1_w_down, ln1_g, ln1_b, w_in,
           lambda_q1, lambda_k1, lambda_q2, lambda_k2, attn_norm_g,
           rwkv_mu, rwkv_w0, rwkv_w2, rwkv_a0, rwkv_a2, rwkv_g2,
           rwkv_k_k, rwkv_k_a, rwkv_r_k, rwkv_gn_g, rwkv_gn_b,
           w_out, ln2_g, ln2_b, ffn2_w_gate, ffn2_w_up, ffn2_w_down, ln3_g, ln3_b):
    B, S, D = x.shape
    T = B * S
    alpha = (2.0 * DEPTH) ** 0.25
    lambda_init = 0.8 - 0.6 * math.exp(-0.3 * l)
    rw = rwkv_w0.shape[-1]
    aw = w_out.shape[1] - rw
    H = aw // LANES
    n_wd, n_ad, n_gd = rwkv_w2.shape[1], rwkv_a2.shape[1], rwkv_g2.shape[1]
    assert n_wd == HEAD and n_ad == HEAD and n_gd <= 2 * LANES
    row = lambda a: a.reshape(1, -1)
    bf = lambda a: a.astype(BF16)

    x1 = ffn_ln(x.reshape(T, D), ffn1_w_gate[l], ffn1_w_up[l], ffn1_w_down[l],
                row(ln1_g[l]), row(ln1_b[l]), alpha=alpha)

    wi = jnp.swapaxes(w_in[l], 0, 1)
    p_attn = proj_attn(x1, wi, n_cols=3 * aw, n_scaled_cols=aw, scale=HEAD ** -0.5 * math.log2(math.e))
    tab, feat = alibi_tables(H, S)
    o_attn = diff_attn(p_attn, tab, feat, row(lambda_q1[l]), row(lambda_k1[l]), row(lambda_q2[l]),
                       row(lambda_k2[l]), row(attn_norm_g[l]), B=B, S=S, H=H, lambda_init=lambda_init)

    mu = jnp.pad(rwkv_mu[l], (0, RWKV_TAIL - (n_wd + n_ad + n_gd)))
    w12 = jnp.zeros((LANES, 2 * rw), F32)
    w12 = w12.at[:HEAD, :rw].set(rwkv_w2[l]).at[HEAD:, rw:].set(rwkv_a2[l])
    g2p = jnp.pad(rwkv_g2[l], ((0, 2 * LANES - n_gd), (0, 0)))
    seqs = proj_rwkv(x1, wi, row(mu), row(rwkv_w0[l]), row(rwkv_a0[l]), row(rwkv_k_k[l]), row(rwkv_k_a[l]),
                     bf(w12), bf(g2p), S=S, col0=3 * aw, width=rw)
    r, lw, k2, v, kk, b, g = [t.reshape(B, S, rw) for t in seqs]
    o_rwkv = rwkv_chunk(r, lw, k2, v, kk, b, g, row(rwkv_r_k[l]), row(rwkv_gn_g[l]), row(rwkv_gn_b[l]))

    wo = w_out[l]
    x2 = out_ln(x1, o_attn, o_rwkv.reshape(T, rw), bf(wo[:aw]), bf(wo[aw:]),
                row(ln2_g[l]), row(ln2_b[l]), alpha=alpha)
    x3 = ffn_ln(x2, ffn2_w_gate[l], ffn2_w_up[l], ffn2_w_down[l],
                row(ln3_g[l]), row(ln3_b[l]), alpha=alpha)
    return x3.reshape(B, S, D)


def kernel(x, ffn1_w_gate, ffn1_w_up, ffn1_w_down, ln1_g, ln1_b, w_in, lambda_q1, lambda_k1, lambda_q2, lambda_k2, attn_norm_g, rwkv_mu, rwkv_w0, rwkv_w2, rwkv_a0, rwkv_a2, rwkv_g2, rwkv_k_k, rwkv_k_a, rwkv_r_k, rwkv_gn_g, rwkv_gn_b, w_out, ln2_g, ln2_b, ffn2_w_gate, ffn2_w_up, ffn2_w_down, ln3_g, ln3_b):
    for l in range(DEPTH):
        x = _layer(x, l, ffn1_w_gate, ffn1_w_up, ffn1_w_down, ln1_g, ln1_b, w_in,
                   lambda_q1, lambda_k1, lambda_q2, lambda_k2, attn_norm_g,
                   rwkv_mu, rwkv_w0, rwkv_w2, rwkv_a0, rwkv_a2, rwkv_g2,
                   rwkv_k_k, rwkv_k_a, rwkv_r_k, rwkv_gn_g, rwkv_gn_b,
                   w_out, ln2_g, ln2_b, ffn2_w_gate, ffn2_w_up, ffn2_w_down, ln3_g, ln3_b)
    return x
```

```python
import functools
import math

import jax
import jax.numpy as jnp
from jax import lax
from jax.experimental import pallas as pl
from jax.experimental.pallas import tpu as pltpu

F32 = jnp.float32
BF16 = jnp.bfloat16

DEPTH = 1
LN_EPS = 1e-5
ATTN_NORM_EPS = 1e-5
GN_EPS = 64e-5
HEAD = 64
LANES = 128
CHUNK = 64
VMEM_LIMIT = 56 * 1024 * 1024


def _cparams(sem):
    return pltpu.CompilerParams(dimension_semantics=sem, vmem_limit_bytes=VMEM_LIMIT)


def _layer_norm(y, g, b):
    mu = jnp.mean(y, axis=-1, keepdims=True)
    d = y - mu
    var = jnp.mean(d * d, axis=-1, keepdims=True)
    return d * lax.rsqrt(var + LN_EPS) * g + b


def _dot(a, b):
    return jnp.dot(a, b, preferred_element_type=F32)


def _dot_nt(a, b):
    return lax.dot_general(a, b, (((1,), (1,)), ((), ())), preferred_element_type=F32)


def _dot_tn(a, b):
    return lax.dot_general(a, b, (((0,), (0,)), ((), ())), preferred_element_type=F32)


def _ffn_ln_kernel(x_ref, wg_ref, wu_ref, wd_ref, g_ref, b_ref, o_ref, xb_ref, *, alpha, ncol):
    j = pl.program_id(1)
    nj = pl.num_programs(1)
    tm, D = o_ref.shape
    cols = [slice(n * ncol, (n + 1) * ncol) for n in range(D // ncol)]

    def body(first, final):
        nrow = 2 if (first or final) else 1
        rows = [slice(r * tm // nrow, (r + 1) * tm // nrow) for r in range(nrow)]
        if first:
            for r in rows:
                x = x_ref[r, :]
                xb_ref[r, :] = x.astype(BF16)
                o_ref[r, :] = alpha * x
        wg = wg_ref[...].astype(BF16)
        wu = wu_ref[...].astype(BF16)
        wd = [wd_ref[:, c].astype(BF16) for c in cols]
        hs = []
        for r in rows:
            xb = xb_ref[r, :]
            hg = _dot(xb, wg)
            hu = _dot(xb, wu)
            hs.append((0.5 * hg * jax.nn.sigmoid(hg) * hu).astype(BF16))
        for r, h in zip(rows, hs):
            for c, w in zip(cols, wd):
                o_ref[r, c] += _dot(h, w)
        if final:
            for r in rows:
                o_ref[r, :] = _layer_norm(o_ref[r, :], g_ref[...], b_ref[...])

    pl.when(j == 0)(lambda: body(True, False))
    pl.when(jnp.logical_and(j > 0, j < nj - 1))(lambda: body(False, False))
    pl.when(jnp.logical_and(j > 0, j == nj - 1))(lambda: body(False, True))


def ffn_ln(x, wg, wu, wd, g, b, *, alpha, tm=1024, tf=256, ncol=512):
    T, D = x.shape
    Fd = wg.shape[1]
    tm = min(tm, T)
    tf = min(tf, Fd)
    ncol = min(ncol, D)
    assert Fd // tf >= 2
    return pl.pallas_call(
        functools.partial(_ffn_ln_kernel, alpha=alpha, ncol=ncol),
        grid=(T // tm, Fd // tf),
        in_specs=[
            pl.BlockSpec((tm, D), lambda i, j: (i, 0)),
            pl.BlockSpec((D, tf), lambda i, j: (0, j)),
            pl.BlockSpec((D, tf), lambda i, j: (0, j)),
            pl.BlockSpec((tf, D), lambda i, j: (j, 0)),
            pl.BlockSpec((1, D), lambda i, j: (0, 0)),
            pl.BlockSpec((1, D), lambda i, j: (0, 0)),
        ],
        out_specs=pl.BlockSpec((tm, D), lambda i, j: (i, 0)),
        out_shape=jax.ShapeDtypeStruct((T, D), F32),
        scratch_shapes=[pltpu.VMEM((tm, D), BF16)],
        compiler_params=_cparams(("parallel", "arbitrary")),
    )(x, wg, wu, wd, g, b)


def _pick_tile(limit, *sizes):
    t = limit // LANES * LANES
    while any(s % t for s in sizes):
        t -= LANES
    return t


def _stage_weight_tile(wt_ref, wb_ref, s, first_tile, n_cols):
    tile = wt_ref.shape[0]
    col = (first_tile + s) * tile + lax.broadcasted_iota(jnp.int32, wt_ref.shape, 0)
    wb_ref[pl.ds(pl.multiple_of(s * tile, tile), tile), :] = jnp.where(col < n_cols, wt_ref[...], 0.0).astype(BF16)


def _proj_attn_kernel(x_ref, wt_ref, o_ref, wb_ref, *, n_wt, n_cols, n_scaled, scale):
    s = pl.program_id(0)

    @pl.when(s < n_wt)
    def _():
        _stage_weight_tile(wt_ref, wb_ref, s, 0, n_cols)

    @pl.when(s >= n_wt)
    def _():
        tile = wt_ref.shape[0]
        xb = x_ref[...].astype(BF16)
        for n in range(n_wt):
            acc = _dot_nt(xb, wb_ref[n * tile:(n + 1) * tile, :])
            if n < n_scaled:
                acc = acc * scale
            o_ref[:, n * tile:(n + 1) * tile] = acc.astype(o_ref.dtype)


def proj_attn(x, wt, *, n_cols, n_scaled_cols, scale, tm=512, tile=512):
    T, D = x.shape
    tm = min(tm, T)
    tile = _pick_tile(tile, n_cols, n_scaled_cols)
    n_wt = n_cols // tile
    tok = lambda s: (jnp.maximum(s - n_wt, 0), 0)
    return pl.pallas_call(
        functools.partial(_proj_attn_kernel, n_wt=n_wt, n_cols=wt.shape[0],
                          n_scaled=n_scaled_cols // tile, scale=scale),
        grid=(n_wt + T // tm,),
        in_specs=[
            pl.BlockSpec((tm, D), tok),
            pl.BlockSpec((tile, D), lambda s: (jnp.minimum(s, n_wt - 1), 0)),
        ],
        out_specs=pl.BlockSpec((tm, n_cols), tok),
        out_shape=jax.ShapeDtypeStruct((T, n_cols), BF16),
        scratch_shapes=[pltpu.VMEM((n_cols, D), BF16)],
        compiler_params=_cparams(("arbitrary",)),
    )(x, wt)


ATT_TQ = 256
ATT_TK = 512
N_BIAS_TERMS = 3
BF16_ROWS = 16
VT_ROWS = LANES + BF16_ROWS


def alibi_tables(H, S):
    slope = jnp.exp2(-8.0 * (jnp.arange(H, dtype=F32) + 1.0) / H) * math.log2(math.e)
    terms, rest = [], slope
    for _ in range(N_BIAS_TERMS):
        t = rest.astype(BF16).astype(F32)
        terms += [64.0 * t, t]
        rest = rest - t
    tab = jnp.stack(terms + [jnp.zeros_like(slope)] * (8 - len(terms)), axis=1)
    pos = jnp.arange(S, dtype=jnp.int32)
    hi, lo = (pos // 64).astype(F32), (pos % 64).astype(F32)
    feat = jnp.stack([hi, lo] * N_BIAS_TERMS, axis=1)
    feat = jnp.pad(feat, ((0, 0), (0, LANES - feat.shape[1]))).astype(BF16)
    return tab, feat


def _diff_attn_kernel(tab_ref, q_ref, k_ref, v_ref, feat_ref, lq1_ref, lk1_ref, lq2_ref, lk2_ref, ng_ref,
                      o_ref, ka_ref, vt_ref, qa_ref, acc_ref, m_ref, *, hb, nck, lambda_init):
    tq = ATT_TQ
    tk = vt_ref.shape[2]
    g = pl.program_id(1)
    i = pl.program_id(2)
    heads = range(hb)
    hs = lambda h: slice(h * LANES, (h + 1) * LANES)

    @pl.when(i == 0)
    def _():
        for h in heads:
            ka_ref[h, :, 0:LANES] = k_ref[:, hs(h)]
            ka_ref[h, :, LANES:2 * LANES] = feat_ref[...]
            for c in range(nck):
                vt_ref[h * nck + c, 0:LANES, :] = v_ref[c * tk:(c + 1) * tk, hs(h)].astype(F32).T.astype(BF16)
                vt_ref[h * nck + c, LANES:VT_ROWS, :] = jnp.ones((BF16_ROWS, tk), BF16)

    lane = lax.broadcasted_iota(jnp.int32, (tq, LANES), 1)
    for h in heads:
        q = q_ref[:, hs(h)]
        zero = jnp.zeros_like(q)
        qf = jnp.zeros((tq, LANES), F32)
        for n in range(2 * N_BIAS_TERMS):
            qf = jnp.where(lane == n, tab_ref[g * hb + h, n], qf)
        qf = qf.astype(BF16)
        qa_ref[h, 0:tq, 0:LANES] = jnp.where(lane < HEAD, q, zero)
        qa_ref[h, tq:2 * tq, 0:LANES] = jnp.where(lane >= HEAD, q, zero)
        qa_ref[h, 0:tq, LANES:2 * LANES] = qf
        qa_ref[h, tq:2 * tq, LANES:2 * LANES] = qf
        m_ref[h] = jnp.full((1, 2 * tq), -1e30, F32)
        acc_ref[h] = jnp.zeros((VT_ROWS, 2 * tq), F32)

    def chunk(c, part=None):
        start = pl.multiple_of(c * tk, tk)
        nk, lanes = tk, slice(None)
        if part is not None:
            start, nk, lanes = start + part * tq, tq, slice(part * tq, (part + 1) * tq)
        ss = [_dot_nt(ka_ref[h, pl.ds(start, nk), :], qa_ref[h]) for h in heads]
        if part is not None:
            key = start + lax.broadcasted_iota(jnp.int32, (nk, 2 * tq), 0)
            qcol = lax.broadcasted_iota(jnp.int32, (nk, 2 * tq), 1)
            qpos = i * tq + jnp.where(qcol >= tq, qcol - tq, qcol)
            keep = key <= qpos
            ss = [jnp.where(keep, s, -jnp.inf) for s in ss]
        m_olds = [m_ref[h] for h in heads]
        m_news = [jnp.maximum(m_olds[h], jnp.max(ss[h], axis=0, keepdims=True)) for h in heads]
        ps = [jnp.exp2(ss[h] - m_news[h]).astype(BF16) for h in heads]
        alphas = [jnp.exp2(m_olds[h] - m_news[h]) for h in heads]
        pvs = [_dot(vt_ref[h * nck + c, :, lanes], ps[h]) for h in heads]
        for h in heads:
            m_ref[h] = m_news[h]
            acc_ref[h] = alphas[h] * acc_ref[h] + pvs[h]

    ratio = tk // tq
    n_full = i // ratio

    def body(c, carry):
        chunk(c)
        return carry

    lax.fori_loop(0, n_full, body, 0)
    chunk(n_full, 0)
    for part in range(1, ratio):
        pl.when(i % ratio >= part)(functools.partial(chunk, n_full, part))

    lam = (jnp.exp(jnp.sum(lq1_ref[...] * lk1_ref[...], axis=-1, keepdims=True))
           - jnp.exp(jnp.sum(lq2_ref[...] * lk2_ref[...], axis=-1, keepdims=True)) + lambda_init)
    for h in heads:
        acc = acc_ref[h]
        rl = 1.0 / acc[LANES:LANES + 1, :]
        ot = (acc[0:LANES, 0:tq] * rl[:, 0:tq]
              - lam * (acc[0:LANES, tq:2 * tq] * rl[:, tq:2 * tq]))
        o = ot.T
        o = o * lax.rsqrt(jnp.mean(o * o, axis=-1, keepdims=True) + ATTN_NORM_EPS) * ng_ref[...]
        o_ref[:, hs(h)] = (o * (1.0 - lambda_init)).astype(o_ref.dtype)


def diff_attn(pa, tab, feat, lq1, lk1, lq2, lk2, norm_g, *, B, S, H, lambda_init, hb=8):
    hb = min(hb, H)
    tq, tk = ATT_TQ, min(ATT_TK, S)
    assert S % tk == 0 and H % hb == 0 and tk % tq == 0
    nq, nck, ng = S // tq, S // tk, H // hb
    wb = hb * LANES
    small = lambda n: pl.BlockSpec((1, n), lambda b, g, i: (0, 0))
    return pl.pallas_call(
        functools.partial(_diff_attn_kernel, hb=hb, nck=nck, lambda_init=lambda_init),
        grid=(B, ng, nq),
        in_specs=[
            pl.BlockSpec(memory_space=pltpu.SMEM),
            pl.BlockSpec((tq, wb), lambda b, g, i: (b * nq + i, g)),
            pl.BlockSpec((S, wb), lambda b, g, i: (b, ng + g)),
            pl.BlockSpec((S, wb), lambda b, g, i: (b, 2 * ng + g)),
            pl.BlockSpec((S, LANES), lambda b, g, i: (0, 0)),
            small(HEAD), small(HEAD), small(HEAD), small(HEAD), small(2 * HEAD),
        ],
        out_specs=pl.BlockSpec((tq, wb), lambda b, g, i: (b * nq + i, g)),
        out_shape=jax.ShapeDtypeStruct((B * S, H * LANES), BF16),
        scratch_shapes=[
            pltpu.VMEM((hb, S, 2 * LANES), BF16),
            pltpu.VMEM((hb * nck, VT_ROWS, tk), BF16),
            pltpu.VMEM((hb, 2 * tq, 2 * LANES), BF16),
            pltpu.VMEM((hb, VT_ROWS, 2 * tq), F32),
            pltpu.VMEM((hb, 1, 2 * tq), F32),
        ],
        compiler_params=_cparams(("parallel", "parallel", "arbitrary")),
    )(tab, pa, pa, pa, feat, lq1, lk1, lq2, lk2, norm_g)


def _same_head(n):
    r = lax.broadcasted_iota(jnp.int32, (n, n), 0) // HEAD
    c = lax.broadcasted_iota(jnp.int32, (n, n), 1) // HEAD
    return r == c


def _head_ones(n):
    return jnp.where(_same_head(n), 1.0, 0.0).astype(BF16)


def _head_sum(x, ones, split=True):
    hi = x.astype(BF16)
    if not split:
        return _dot(hi, ones)
    lo = (x - hi.astype(F32)).astype(BF16)
    return _dot(hi, ones) + _dot(lo, ones)


RWKV_TAIL = 3 * LANES


def _proj_rwkv_kernel(x_ref, wt_ref, mu_ref, w0_ref, a0_ref, kk_ref, ka_ref, w12_ref, g2_ref,
                      r_ref, lw_ref, k2_ref, v_ref, kkn_ref, b_ref, g_ref, wb_ref, pr_ref, prev_ref,
                      *, n_wt, first_tile, n_cols, width, tiles_per_seq):
    s = pl.program_id(0)

    @pl.when(s == 0)
    def _():
        prev_ref[...] = jnp.zeros_like(prev_ref)

    @pl.when(s < n_wt)
    def _():
        _stage_weight_tile(wt_ref, wb_ref, s, first_tile, n_cols)

    @pl.when(s >= n_wt)
    def _():
        tm = x_ref.shape[0]
        xb = x_ref[...].astype(BF16)
        c3 = 3 * width
        first = (s - n_wt) % tiles_per_seq == 0
        row = lax.broadcasted_iota(jnp.int32, (tm, 1), 0)

        def project(c0, c1):
            pr_ref[:, c0:c1] = _dot_nt(xb, wb_ref[c0:c1, :])

        def mixed(c0, c1):
            x = pr_ref[:, c0:c1]
            prow = jnp.where(first, 0.0, prev_ref[0:1, c0:c1])
            xs = jnp.where(row == 0, prow, pltpu.roll(x, 1, axis=0))
            return x + mu_ref[:, c0:c1] * (xs - x)

        project(c3, c3 + RWKV_TAIL)
        project(width, 2 * width)

        lora_in = mixed(c3, c3 + LANES)
        lane = lax.broadcasted_iota(jnp.int32, lora_in.shape, 1)
        lora_in = jnp.where(lane < HEAD, jnp.tanh(lora_in), lora_in)
        z = _dot(lora_in.astype(BF16), w12_ref[...])
        g_ref[...] = _dot(jax.nn.sigmoid(mixed(c3 + LANES, c3 + RWKV_TAIL)).astype(BF16), g2_ref[...])
        lw_ref[...] = -math.exp(-0.5) * jax.nn.sigmoid(w0_ref[...] + z[:, 0:width])
        a = jax.nn.sigmoid(a0_ref[...] + z[:, width:2 * width])

        project(0, width)

        k = mixed(width, 2 * width)
        k2_ref[...] = k * (1.0 + (a - 1.0) * ka_ref[...])
        kx = k * kk_ref[...]
        gw = min(2 * LANES, width)
        ones = _head_ones(gw)
        ss = jnp.concatenate(
            [_head_sum(kx[:, t * gw:(t + 1) * gw] * kx[:, t * gw:(t + 1) * gw], ones)
             for t in range(width // gw)], axis=1)
        kkn = kx / jnp.maximum(jnp.sqrt(ss), 1e-12)
        kkn_ref[...] = kkn
        b_ref[...] = kkn * a

        project(2 * width, c3)

        r_ref[...] = mixed(0, width)
        v_ref[...] = mixed(2 * width, c3)
        prev_ref[0:1, :] = pr_ref[tm - 1:tm, :]


def proj_rwkv(x, wt, mu, w0, a0, k_k, k_a, w12, g2p, *, S, col0, width, tm=256, tile=384):
    T, D = x.shape
    tm = min(tm, S)
    W = 3 * width + RWKV_TAIL
    tile = _pick_tile(tile, col0, W)
    assert S % tm == 0
    n_wt, first_tile = W // tile, col0 // tile
    tok = lambda s: (jnp.maximum(s - n_wt, 0), 0)
    row = lambda n: pl.BlockSpec((1, n), lambda s: (0, 0))
    out = pl.BlockSpec((tm, width), tok)
    return pl.pallas_call(
        functools.partial(_proj_rwkv_kernel, n_wt=n_wt, first_tile=first_tile, n_cols=wt.shape[0],
                          width=width, tiles_per_seq=S // tm),
        grid=(n_wt + T // tm,),
        in_specs=[
            pl.BlockSpec((tm, D), tok),
            pl.BlockSpec((tile, D), lambda s: (first_tile + jnp.minimum(s, n_wt - 1), 0)),
            row(W), row(width), row(width), row(width), row(width),
            pl.BlockSpec((LANES, 2 * width), lambda s: (0, 0)),
            pl.BlockSpec((2 * LANES, width), lambda s: (0, 0)),
        ],
        out_specs=[out] * 7,
        out_shape=[jax.ShapeDtypeStruct((T, width), F32)] * 7,
        scratch_shapes=[pltpu.VMEM((W, D), BF16), pltpu.VMEM((tm, W), F32), pltpu.VMEM((8, W), F32)],
        compiler_params=_cparams(("arbitrary",)),
    )(x, wt, mu, w0, a0, k_k, k_a, w12, g2p)


GROUP = 2


def _block_diag(x):
    head = lax.broadcasted_iota(jnp.int32, x.shape, 1) // HEAD
    zero = jnp.zeros_like(x)
    return jnp.concatenate([jnp.where(head == h, x, zero) for h in range(x.shape[1] // HEAD)], axis=0)


def _chunk_step(rs, lws, k2s, vs, kks, bs, sts):
    C = CHUNK
    GL = rs[0].shape[1]
    idx = range(len(rs))
    row = lax.broadcasted_iota(jnp.int32, (C, GL), 0)
    tcol = lax.broadcasted_iota(jnp.int32, (C, GL), 1) % HEAD
    strict = row > tcol
    incl = row >= tcol
    eye = jnp.where(row == tcol, 1.0, 0.0)
    bdb = lambda x: _block_diag(x).astype(BF16)

    cums, tots, Mbs, Mks, ARSs = [], [], [], [], []
    for i in idx:
        cm = lws[i]
        sh = 1
        while sh < C:
            cm = cm + jnp.where(row >= sh, pltpu.roll(cm, sh, axis=0), 0.0)
            sh *= 2
        e_neg = jnp.exp(-cm)
        AR = jnp.concatenate([-kks[i] * jnp.exp(cm - lws[i]), rs[i] * jnp.exp(cm)], axis=0).astype(BF16)
        cums.append(cm)
        tots.append(cm[C - 1:C, :])
        Mbk = _dot_nt(AR, jnp.concatenate([bdb(bs[i] * e_neg), bdb(k2s[i] * e_neg)], axis=0))
        Mbs.append(Mbk[:, 0:GL])
        Mks.append(Mbk[:, GL:2 * GL])
        ARSs.append(_dot_nt(AR, sts[i].astype(BF16)))
    vbds = [bdb(v) for v in vs]

    Ls = [jnp.where(strict, Mbs[i][0:C], 0.0) for i in idx]
    Ps = [eye + L for L in Ls]
    Lps = [_dot(L.astype(BF16), bdb(L)) for L in Ls]
    Xs = [ARSs[i][0:C] + _dot(jnp.where(strict, Mks[i][0:C], 0.0).astype(BF16), vbds[i]) for i in idx]
    m = 2
    while 2 * m < C:
        both = [_dot(jnp.concatenate([Lps[i], Ps[i]], axis=0).astype(BF16), bdb(Lps[i])) for i in idx]
        Ps = [Ps[i] + both[i][C:2 * C] for i in idx]
        Lps = [bt[0:C] for bt in both]
        m *= 2
    Ps = [Ps[i] + _dot(Ps[i].astype(BF16), bdb(Lps[i])) for i in idx]

    Us = [_dot(Ps[i].astype(BF16), bdb(Xs[i])) for i in idx]
    Ys = [ARSs[i][C:2 * C]
          + _dot(jnp.concatenate([jnp.where(incl, Mbs[i][C:2 * C], 0.0),
                                  jnp.where(incl, Mks[i][C:2 * C], 0.0)], axis=1).astype(BF16),
                 jnp.concatenate([bdb(Us[i]), vbds[i]], axis=0)) for i in idx]
    same_head = _same_head(GL)
    st_news = []
    for i in idx:
        e_rem = jnp.exp(tots[i] - cums[i])
        UV = jnp.concatenate([Us[i], vs[i]], axis=0).astype(BF16)
        BK = jnp.concatenate([bs[i] * e_rem, k2s[i] * e_rem], axis=0).astype(BF16)
        st_news.append(jnp.where(same_head, sts[i] * jnp.exp(tots[i]) + _dot_tn(UV, BK), 0.0))
    return Ys, st_news


def _rwkv_chunk_kernel(r_ref, lw_ref, k2_ref, v_ref, kk_ref, b_ref, g_ref, rk_ref, gng_ref, gnb_ref,
                       o_ref, st_ref, *, nbatch, ngroups):
    c = pl.program_id(2)

    @pl.when(c == 0)
    def _():
        st_ref[...] = jnp.zeros_like(st_ref)

    GL = st_ref.shape[1]
    ones = _head_ones(GL)
    chains = [(bi, p) for bi in range(nbatch) for p in range(ngroups)]
    sl = lambda p: slice(p * GL, (p + 1) * GL)
    get = lambda ref: [ref[bi, :, sl(p)] for bi, p in chains]
    rs, k2s, vs = get(r_ref), get(k2_ref), get(v_ref)
    ys, st_news = _chunk_step(rs, get(lw_ref), k2s, vs, get(kk_ref), get(b_ref),
                              [st_ref[i] for i in range(len(chains))])
    for i in range(len(chains)):
        st_ref[i] = st_news[i]
    ds = [y - _head_sum(y, ones, split=False) * (1.0 / HEAD) for y in ys]
    vars_ = [_head_sum(d * d, ones, split=False) * (1.0 / HEAD) for d in ds]
    for i, (bi, p) in enumerate(chains):
        yn = ds[i] * lax.rsqrt(vars_[i] + GN_EPS) * gng_ref[:, sl(p)] + gnb_ref[:, sl(p)]
        bonus = _head_sum(rs[i] * k2s[i] * rk_ref[:, sl(p)], ones, split=False) * vs[i]
        o_ref[bi, :, sl(p)] = ((yn + bonus) * g_ref[bi, :, sl(p)]).astype(o_ref.dtype)


def rwkv_chunk(r, lw, k2, v, kk, b, g, r_k, gn_g, gn_b, *, nbatch=4, ngroups=8):
    B, S, W = r.shape
    GL = min(GROUP * HEAD, W)
    ngroups = min(ngroups, W // GL)
    nbatch = min(nbatch, B)
    wb = ngroups * GL
    seq = pl.BlockSpec((nbatch, CHUNK, wb), lambda bi, p, c: (bi, c, p))
    par = pl.BlockSpec((1, wb), lambda bi, p, c: (0, p))
    return pl.pallas_call(
        functools.partial(_rwkv_chunk_kernel, nbatch=nbatch, ngroups=ngroups),
        grid=(B // nbatch, W // wb, S // CHUNK),
        in_specs=[seq] * 7 + [par] * 3,
        out_specs=seq,
        out_shape=jax.ShapeDtypeStruct((B, S, W), BF16),
        scratch_shapes=[pltpu.VMEM((nbatch * ngroups, GL, GL), F32)],
        compiler_params=_cparams(("parallel", "parallel", "arbitrary")),
    )(r, lw, k2, v, kk, b, g, r_k, gn_g, gn_b)


def _out_ln_kernel(x_ref, oa_ref, or_ref, wa_ref, wr_ref, g_ref, b_ref, o_ref, *, alpha):
    half = x_ref.shape[0] // 2
    rows = [slice(0, half), slice(half, 2 * half)]
    mixes = [_dot(oa_ref[r, :], wa_ref[...]) + _dot(or_ref[r, :], wr_ref[...]) for r in rows]
    for r, mix in zip(rows, mixes):
        o_ref[r, :] = _layer_norm(alpha * x_ref[r, :] + mix, g_ref[...], b_ref[...])


def out_ln(x, oa, orw, wa, wr, g, b, *, alpha, tm=512):
    T, D = x.shape
    tm = min(tm, T)
    ka, kr = oa.shape[1], orw.shape[1]
    return pl.pallas_call(
        functools.partial(_out_ln_kernel, alpha=alpha),
        grid=(T // tm,),
        in_specs=[
            pl.BlockSpec((tm, D), lambda i: (i, 0)),
            pl.BlockSpec((tm, ka), lambda i: (i, 0)),
            pl.BlockSpec((tm, kr), lambda i: (i, 0)),
            pl.BlockSpec((ka, D), lambda i: (0, 0)),
            pl.BlockSpec((kr, D), lambda i: (0, 0)),
            pl.BlockSpec((1, D), lambda i: (0, 0)),
            pl.BlockSpec((1, D), lambda i: (0, 0)),
        ],
        out_specs=pl.BlockSpec((tm, D), lambda i: (i, 0)),
        out_shape=jax.ShapeDtypeStruct((T, D), F32),
        compiler_params=_cparams(("parallel",)),
    )(x, oa, orw, wa, wr, g, b)


def _layer(x, l, ffn1_w_gate, ffn1_w_up, ffn1_w_down, ln1_g, ln1_b, w_in,
           lambda_q1, lambda_k1, lambda_q2, lambda_k2, attn_norm_g,
           rwkv_mu, rwkv_w0, rwkv_w2, rwkv_a0, rwkv_a2, rwkv_g2,
           rwkv_k_k, rwkv_k_a, rwkv_r_k, rwkv_gn_g, rwkv_gn_b,
           w_out, ln2_g, ln2_b, ffn2_w_gate, ffn2_w_up, ffn2_w_down, ln3_g, ln3_b):
    B, S, D = x.shape
    T = B * S
    alpha = (2.0 * DEPTH) ** 0.25
    lambda_init = 0.8 - 0.6 * math.exp(-0.3 * l)
    rw = rwkv_w0.shape[-1]
    aw = w_out.shape[1] - rw
    H = aw // LANES
    n_wd, n_ad, n_gd = rwkv_w2.shape[1], rwkv_a2.shape[1], rwkv_g2.shape[1]
    assert n_wd == HEAD and n_ad == HEAD and n_gd <= 2 * LANES
    row = lambda a: a.reshape(1, -1)
    bf = lambda a: a.astype(BF16)

    x1 = ffn_ln(x.reshape(T, D), ffn1_w_gate[l], ffn1_w_up[l], ffn1_w_down[l],
                row(ln1_g[l]), row(ln1_b[l]), alpha=alpha)

    wi = jnp.swapaxes(w_in[l], 0, 1)
    p_attn = proj_attn(x1, wi, n_cols=3 * aw, n_scaled_cols=aw, scale=HEAD ** -0.5 * math.log2(math.e))
    tab, feat = alibi_tables(H, S)
    o_attn = diff_attn(p_attn, tab, feat, row(lambda_q1[l]), row(lambda_k1[l]), row(lambda_q2[l]),
                       row(lambda_k2[l]), row(attn_norm_g[l]), B=B, S=S, H=H, lambda_init=lambda_init)

    mu = jnp.pad(rwkv_mu[l], (0, RWKV_TAIL - (n_wd + n_ad + n_gd)))
    w12 = jnp.zeros((LANES, 2 * rw), F32)
    w12 = w12.at[:HEAD, :rw].set(rwkv_w2[l]).at[HEAD:, rw:].set(rwkv_a2[l])
    g2p = jnp.pad(rwkv_g2[l], ((0, 2 * LANES - n_gd), (0, 0)))
    seqs = proj_rwkv(x1, wi, row(mu), row(rwkv_w0[l]), row(rwkv_a0[l]), row(rwkv_k_k[l]), row(rwkv_k_a[l]),
                     bf(w12), bf(g2p), S=S, col0=3 * aw, width=rw)
    r, lw, k2, v, kk, b, g = [t.reshape(B, S, rw) for t in seqs]
    o_rwkv = rwkv_chunk(r, lw, k2, v, kk, b, g, row(rwkv_r_k[l]), row(rwkv_gn_g[l]), row(rwkv_gn_b[l]))

    wo = w_out[l]
    x2 = out_ln(x1, o_attn, o_rwkv.reshape(T, rw), bf(wo[:aw]), bf(wo[aw:]),
                row(ln2_g[l]), row(ln2_b[l]), alpha=alpha)
    x3 = ffn_ln(x2, ffn2_w_gate[l], ffn2_w_up[l], ffn2_w_down[l],
                row(ln3_g[l]), row(ln3_b[l]), alpha=alpha)
    return x3.reshape(B, S, D)


def kernel(x, ffn1_w_gate, ffn1_w_up, ffn1_w_down, ln1_g, ln1_b, w_in, lambda_q1, lambda_k1, lambda_q2, lambda_k2, attn_norm_g, rwkv_mu, rwkv_w0, rwkv_w2, rwkv_a0, rwkv_a2, rwkv_g2, rwkv_k_k, rwkv_k_a, rwkv_r_k, rwkv_gn_g, rwkv_gn_b, w_out, ln2_g, ln2_b, ffn2_w_gate, ffn2_w_up, ffn2_w_down, ln3_g, ln3_b):
    for l in range(DEPTH):
        x = _layer(x, l, ffn1_w_gate, ffn1_w_up, ffn1_w_down, ln1_g, ln1_b, w_in,
                   lambda_q1, lambda_k1, lambda_q2, lambda_k2, attn_norm_g,
                   rwkv_mu, rwkv_w0, rwkv_w2, rwkv_a0, rwkv_a2, rwkv_g2,
                   rwkv_k_k, rwkv_k_a, rwkv_r_k, rwkv_gn_g, rwkv_gn_b,
                   w_out, ln2_g, ln2_b, ffn2_w_gate, ffn2_w_up, ffn2_w_down, ln3_g, ln3_b)
    return x
```

```python
import functools
import math

import jax
import jax.numpy as jnp
from jax import lax
from jax.experimental import pallas as pl
from jax.experimental.pallas import tpu as pltpu

F32 = jnp.float32
BF16 = jnp.bfloat16

DEPTH = 1
LN_EPS = 1e-5
ATTN_NORM_EPS = 1e-5
GN_EPS = 64e-5
HEAD = 64
LANES = 128
CHUNK = 64
VMEM_LIMIT = 56 * 1024 * 1024


def _cparams(sem):
    return pltpu.CompilerParams(dimension_semantics=sem, vmem_limit_bytes=VMEM_LIMIT)


def _layer_norm(y, g, b):
    mu = jnp.mean(y, axis=-1, keepdims=True)
    d = y - mu
    var = jnp.mean(d * d, axis=-1, keepdims=True)
    return d * lax.rsqrt(var + LN_EPS) * g + b


def _dot(a, b):
    return jnp.dot(a, b, preferred_element_type=F32)


def _dot_nt(a, b):
    return lax.dot_general(a, b, (((1,), (1,)), ((), ())), preferred_element_type=F32)


def _dot_tn(a, b):
    return lax.dot_general(a, b, (((0,), (0,)), ((), ())), preferred_element_type=F32)


def _ffn_ln_kernel(x_ref, wg_ref, wu_ref, wd_ref, g_ref, b_ref, o_ref, xb_ref, *, alpha, ncol):
    j = pl.program_id(1)
    nj = pl.num_programs(1)
    tm, D = o_ref.shape
    cols = [slice(n * ncol, (n + 1) * ncol) for n in range(D // ncol)]

    def body(first, final):
        nrow = 2 if (first or final) else 1
        rows = [slice(r * tm // nrow, (r + 1) * tm // nrow) for r in range(nrow)]
        if first:
            for r in rows:
                x = x_ref[r, :]
                xb_ref[r, :] = x.astype(BF16)
                o_ref[r, :] = alpha * x
        wg = wg_ref[...].astype(BF16)
        wu = wu_ref[...].astype(BF16)
        wd = [wd_ref[:, c].astype(BF16) for c in cols]
        hs = []
        for r in rows:
            xb = xb_ref[r, :]
            hg = _dot(xb, wg)
            hu = _dot(xb, wu)
            hs.append((0.5 * hg * jax.nn.sigmoid(hg) * hu).astype(BF16))
        for r, h in zip(rows, hs):
            for c, w in zip(cols, wd):
                o_ref[r, c] += _dot(h, w)
        if final:
            for r in rows:
                o_ref[r, :] = _layer_norm(o_ref[r, :], g_ref[...], b_ref[...])

    pl.when(j == 0)(lambda: body(True, False))
    pl.when(jnp.logical_and(j > 0, j < nj - 1))(lambda: body(False, False))
    pl.when(jnp.logical_and(j > 0, j == nj - 1))(lambda: body(False, True))


FFN_TF = 256


def ffn_ln(x, wg, wu, wd, g, b, *, alpha, tm=1024, tf=FFN_TF, ncol=512):
    T, D = x.shape
    Fd = wg.shape[1]
    tm = min(tm, T)
    tf = min(tf, Fd // 2)
    ncol = min(ncol, D)
    return pl.pallas_call(
        functools.partial(_ffn_ln_kernel, alpha=alpha, ncol=ncol),
        grid=(T // tm, Fd // tf),
        in_specs=[
            pl.BlockSpec((tm, D), lambda i, j: (i, 0)),
            pl.BlockSpec((D, tf), lambda i, j: (0, j)),
            pl.BlockSpec((D, tf), lambda i, j: (0, j)),
            pl.BlockSpec((tf, D), lambda i, j: (j, 0)),
            pl.BlockSpec((1, D), lambda i, j: (0, 0)),
            pl.BlockSpec((1, D), lambda i, j: (0, 0)),
        ],
        out_specs=pl.BlockSpec((tm, D), lambda i, j: (i, 0)),
        out_shape=jax.ShapeDtypeStruct((T, D), F32),
        scratch_shapes=[pltpu.VMEM((tm, D), BF16)],
        compiler_params=_cparams(("parallel", "arbitrary")),
    )(x, wg, wu, wd, g, b)


def _pick_tile(limit, *sizes):
    t = limit // LANES * LANES
    while any(s % t for s in sizes):
        t -= LANES
    return t


def _stage_weight_tile(wt_ref, wb_ref, s, first_tile, n_cols):
    tile = wt_ref.shape[0]
    col = (first_tile + s) * tile + lax.broadcasted_iota(jnp.int32, wt_ref.shape, 0)
    wb_ref[pl.ds(pl.multiple_of(s * tile, tile), tile), :] = jnp.where(col < n_cols, wt_ref[...], 0.0).astype(BF16)


def _proj_attn_kernel(x_ref, wt_ref, o_ref, wb_ref, *, n_wt, n_cols, n_scaled, scale):
    s = pl.program_id(0)

    @pl.when(s < n_wt)
    def _():
        _stage_weight_tile(wt_ref, wb_ref, s, 0, n_cols)

    @pl.when(s >= n_wt)
    def _():
        tile = wt_ref.shape[0]
        xb = x_ref[...].astype(BF16)
        for n in range(n_wt):
            acc = _dot_nt(xb, wb_ref[n * tile:(n + 1) * tile, :])
            if n < n_scaled:
                acc = acc * scale
            o_ref[:, n * tile:(n + 1) * tile] = acc.astype(o_ref.dtype)


def proj_attn(x, wt, *, n_cols, n_scaled_cols, scale, tm=512, tile=512):
    T, D = x.shape
    tm = min(tm, T)
    tile = _pick_tile(tile, n_cols, n_scaled_cols)
    n_wt = n_cols // tile
    tok = lambda s: (jnp.maximum(s - n_wt, 0), 0)
    return pl.pallas_call(
        functools.partial(_proj_attn_kernel, n_wt=n_wt, n_cols=wt.shape[0],
                          n_scaled=n_scaled_cols // tile, scale=scale),
        grid=(n_wt + T // tm,),
        in_specs=[
            pl.BlockSpec((tm, D), tok),
            pl.BlockSpec((tile, D), lambda s: (jnp.minimum(s, n_wt - 1), 0)),
        ],
        out_specs=pl.BlockSpec((tm, n_cols), tok),
        out_shape=jax.ShapeDtypeStruct((T, n_cols), BF16),
        scratch_shapes=[pltpu.VMEM((n_cols, D), BF16)],
        compiler_params=_cparams(("arbitrary",)),
    )(x, wt)


ATT_TQ = 256
ATT_TK = 512
N_BIAS_TERMS = 3
BF16_ROWS = 16
VT_ROWS = LANES + BF16_ROWS


def alibi_tables(H, S):
    slope = jnp.exp2(-8.0 * (jnp.arange(H, dtype=F32) + 1.0) / H) * math.log2(math.e)
    terms, rest = [], slope
    for _ in range(N_BIAS_TERMS):
        t = rest.astype(BF16).astype(F32)
        terms += [64.0 * t, t]
        rest = rest - t
    tab = jnp.stack(terms + [jnp.zeros_like(slope)] * (8 - len(terms)), axis=1)
    pos = jnp.arange(S, dtype=jnp.int32)
    hi, lo = (pos // 64).astype(F32), (pos % 64).astype(F32)
    feat = jnp.stack([hi, lo] * N_BIAS_TERMS, axis=1)
    feat = jnp.pad(feat, ((0, 0), (0, LANES - feat.shape[1]))).astype(BF16)
    return tab, feat


def _diff_attn_kernel(tab_ref, q_ref, k_ref, v_ref, feat_ref, lq1_ref, lk1_ref, lq2_ref, lk2_ref, ng_ref,
                      o_ref, ka_ref, vt_ref, qa_ref, acc_ref, m_ref, *, hb, nck, lambda_init):
    tq = ATT_TQ
    tk = vt_ref.shape[2]
    g = pl.program_id(1)
    i = pl.program_id(2)
    heads = range(hb)
    hs = lambda h: slice(h * LANES, (h + 1) * LANES)

    @pl.when(i == 0)
    def _():
        for h in heads:
            ka_ref[h, :, 0:LANES] = k_ref[:, hs(h)]
            ka_ref[h, :, LANES:2 * LANES] = feat_ref[...]
            for c in range(nck):
                vt_ref[h * nck + c, 0:LANES, :] = v_ref[c * tk:(c + 1) * tk, hs(h)].astype(F32).T.astype(BF16)
                vt_ref[h * nck + c, LANES:VT_ROWS, :] = jnp.ones((BF16_ROWS, tk), BF16)

    lane = lax.broadcasted_iota(jnp.int32, (tq, LANES), 1)
    for h in heads:
        q = q_ref[:, hs(h)]
        zero = jnp.zeros_like(q)
        qf = jnp.zeros((tq, LANES), F32)
        for n in range(2 * N_BIAS_TERMS):
            qf = jnp.where(lane == n, tab_ref[g * hb + h, n], qf)
        qf = qf.astype(BF16)
        qa_ref[h, 0:tq, 0:LANES] = jnp.where(lane < HEAD, q, zero)
        qa_ref[h, tq:2 * tq, 0:LANES] = jnp.where(lane >= HEAD, q, zero)
        qa_ref[h, 0:tq, LANES:2 * LANES] = qf
        qa_ref[h, tq:2 * tq, LANES:2 * LANES] = qf
        m_ref[h] = jnp.full((1, 2 * tq), -1e30, F32)
        acc_ref[h] = jnp.zeros((VT_ROWS, 2 * tq), F32)

    def chunk(c, part=None):
        start = pl.multiple_of(c * tk, tk)
        nk, lanes = tk, slice(None)
        if part is not None:
            start, nk, lanes = start + part * tq, tq, slice(part * tq, (part + 1) * tq)
        ss = [_dot_nt(ka_ref[h, pl.ds(start, nk), :], qa_ref[h]) for h in heads]
        if part is not None:
            key = start + lax.broadcasted_iota(jnp.int32, (nk, 2 * tq), 0)
            qcol = lax.broadcasted_iota(jnp.int32, (nk, 2 * tq), 1)
            qpos = i * tq + jnp.where(qcol >= tq, qcol - tq, qcol)
            keep = key <= qpos
            ss = [jnp.where(keep, s, -jnp.inf) for s in ss]
        m_olds = [m_ref[h] for h in heads]
        m_news = [jnp.maximum(m_olds[h], jnp.max(ss[h], axis=0, keepdims=True)) for h in heads]
        ps = [jnp.exp2(ss[h] - m_news[h]).astype(BF16) for h in heads]
        alphas = [jnp.exp2(m_olds[h] - m_news[h]) for h in heads]
        pvs = [_dot(vt_ref[h * nck + c, :, lanes], ps[h]) for h in heads]
        for h in heads:
            m_ref[h] = m_news[h]
            acc_ref[h] = alphas[h] * acc_ref[h] + pvs[h]

    ratio = tk // tq
    n_full = i // ratio

    def body(c, carry):
        chunk(c)
        return carry

    lax.fori_loop(0, n_full, body, 0)
    chunk(n_full, 0)
    for part in range(1, ratio):
        pl.when(i % ratio >= part)(functools.partial(chunk, n_full, part))

    lam = (jnp.exp(jnp.sum(lq1_ref[...] * lk1_ref[...], axis=-1, keepdims=True))
           - jnp.exp(jnp.sum(lq2_ref[...] * lk2_ref[...], axis=-1, keepdims=True)) + lambda_init)
    for h in heads:
        acc = acc_ref[h]
        rl = 1.0 / acc[LANES:LANES + 1, :]
        ot = (acc[0:LANES, 0:tq] * rl[:, 0:tq]
              - lam * (acc[0:LANES, tq:2 * tq] * rl[:, tq:2 * tq]))
        o = ot.T
        o = o * lax.rsqrt(jnp.mean(o * o, axis=-1, keepdims=True) + ATTN_NORM_EPS) * ng_ref[...]
        o_ref[:, hs(h)] = (o * (1.0 - lambda_init)).astype(o_ref.dtype)


def diff_attn(pa, tab, feat, lq1, lk1, lq2, lk2, norm_g, *, B, S, H, lambda_init, hb=8):
    hb = min(hb, H)
    tq, tk = ATT_TQ, min(ATT_TK, S)
    assert S % tk == 0 and H % hb == 0 and tk % tq == 0
    nq, nck, ng = S // tq, S // tk, H // hb
    wb = hb * LANES
    small = lambda n: pl.BlockSpec((1, n), lambda b, g, i: (0, 0))
    return pl.pallas_call(
        functools.partial(_diff_attn_kernel, hb=hb, nck=nck, lambda_init=lambda_init),
        grid=(B, ng, nq),
        in_specs=[
            pl.BlockSpec(memory_space=pltpu.SMEM),
            pl.BlockSpec((tq, wb), lambda b, g, i: (b * nq + i, g)),
            pl.BlockSpec((S, wb), lambda b, g, i: (b, ng + g)),
            pl.BlockSpec((S, wb), lambda b, g, i: (b, 2 * ng + g)),
            pl.BlockSpec((S, LANES), lambda b, g, i: (0, 0)),
            small(HEAD), small(HEAD), small(HEAD), small(HEAD), small(2 * HEAD),
        ],
        out_specs=pl.BlockSpec((tq, wb), lambda b, g, i: (b * nq + i, g)),
        out_shape=jax.ShapeDtypeStruct((B * S, H * LANES), BF16),
        scratch_shapes=[
            pltpu.VMEM((hb, S, 2 * LANES), BF16),
            pltpu.VMEM((hb * nck, VT_ROWS, tk), BF16),
            pltpu.VMEM((hb, 2 * tq, 2 * LANES), BF16),
            pltpu.VMEM((hb, VT_ROWS, 2 * tq), F32),
            pltpu.VMEM((hb, 1, 2 * tq), F32),
        ],
        compiler_params=_cparams(("parallel", "parallel", "arbitrary")),
    )(tab, pa, pa, pa, feat, lq1, lk1, lq2, lk2, norm_g)


def _same_head(n):
    r = lax.broadcasted_iota(jnp.int32, (n, n), 0) // HEAD
    c = lax.broadcasted_iota(jnp.int32, (n, n), 1) // HEAD
    return r == c


def _head_ones(n):
    return jnp.where(_same_head(n), 1.0, 0.0).astype(BF16)


def _head_sum(x, ones, split=True):
    hi = x.astype(BF16)
    if not split:
        return _dot(hi, ones)
    lo = (x - hi.astype(F32)).astype(BF16)
    return _dot(hi, ones) + _dot(lo, ones)


RWKV_TAIL = 3 * LANES


def _proj_rwkv_kernel(x_ref, wt_ref, mu_ref, w0_ref, a0_ref, kk_ref, ka_ref, w12_ref, g2_ref,
                      r_ref, lw_ref, k2_ref, v_ref, kkn_ref, b_ref, g_ref, wb_ref, pr_ref, prev_ref,
                      *, n_wt, first_tile, n_cols, width, tiles_per_seq):
    s = pl.program_id(0)

    @pl.when(s == 0)
    def _():
        prev_ref[...] = jnp.zeros_like(prev_ref)

    @pl.when(s < n_wt)
    def _():
        _stage_weight_tile(wt_ref, wb_ref, s, first_tile, n_cols)

    @pl.when(s >= n_wt)
    def _():
        tm = x_ref.shape[0]
        xb = x_ref[...].astype(BF16)
        c3 = 3 * width
        first = (s - n_wt) % tiles_per_seq == 0
        row = lax.broadcasted_iota(jnp.int32, (tm, 1), 0)

        def project(c0, c1):
            pr_ref[:, c0:c1] = _dot_nt(xb, wb_ref[c0:c1, :])

        def mixed(c0, c1):
            x = pr_ref[:, c0:c1]
            prow = jnp.where(first, 0.0, prev_ref[0:1, c0:c1])
            xs = jnp.where(row == 0, prow, pltpu.roll(x, 1, axis=0))
            return x + mu_ref[:, c0:c1] * (xs - x)

        project(c3, c3 + RWKV_TAIL)
        project(width, 2 * width)

        lora_in = mixed(c3, c3 + LANES)
        lane = lax.broadcasted_iota(jnp.int32, lora_in.shape, 1)
        lora_in = jnp.where(lane < HEAD, jnp.tanh(lora_in), lora_in)
        z = _dot(lora_in.astype(BF16), w12_ref[...])
        g_ref[...] = _dot(jax.nn.sigmoid(mixed(c3 + LANES, c3 + RWKV_TAIL)).astype(BF16), g2_ref[...])
        lw_ref[...] = -math.exp(-0.5) * jax.nn.sigmoid(w0_ref[...] + z[:, 0:width])
        a = jax.nn.sigmoid(a0_ref[...] + z[:, width:2 * width])

        project(0, width)

        k = mixed(width, 2 * width)
        k2_ref[...] = k * (1.0 + (a - 1.0) * ka_ref[...])
        kx = k * kk_ref[...]
        gw = min(2 * LANES, width)
        ones = _head_ones(gw)
        ss = jnp.concatenate(
            [_head_sum(kx[:, t * gw:(t + 1) * gw] * kx[:, t * gw:(t + 1) * gw], ones)
             for t in range(width // gw)], axis=1)
        kkn = kx / jnp.maximum(jnp.sqrt(ss), 1e-12)
        kkn_ref[...] = kkn
        b_ref[...] = kkn * a

        project(2 * width, c3)

        r_ref[...] = mixed(0, width)
        v_ref[...] = mixed(2 * width, c3)
        prev_ref[0:1, :] = pr_ref[tm - 1:tm, :]


def proj_rwkv(x, wt, mu, w0, a0, k_k, k_a, w12, g2p, *, S, col0, width, tm=256, tile=384):
    T, D = x.shape
    tm = min(tm, S)
    W = 3 * width + RWKV_TAIL
    tile = _pick_tile(tile, col0, W)
    assert S % tm == 0
    n_wt, first_tile = W // tile, col0 // tile
    tok = lambda s: (jnp.maximum(s - n_wt, 0), 0)
    row = lambda n: pl.BlockSpec((1, n), lambda s: (0, 0))
    out = pl.BlockSpec((tm, width), tok)
    return pl.pallas_call(
        functools.partial(_proj_rwkv_kernel, n_wt=n_wt, first_tile=first_tile, n_cols=wt.shape[0],
                          width=width, tiles_per_seq=S // tm),
        grid=(n_wt + T // tm,),
        in_specs=[
            pl.BlockSpec((tm, D), tok),
            pl.BlockSpec((tile, D), lambda s: (first_tile + jnp.minimum(s, n_wt - 1), 0)),
            row(W), row(width), row(width), row(width), row(width),
            pl.BlockSpec((LANES, 2 * width), lambda s: (0, 0)),
            pl.BlockSpec((2 * LANES, width), lambda s: (0, 0)),
        ],
        out_specs=[out] * 7,
        out_shape=[jax.ShapeDtypeStruct((T, width), F32)] * 7,
        scratch_shapes=[pltpu.VMEM((W, D), BF16), pltpu.VMEM((tm, W), F32), pltpu.VMEM((8, W), F32)],
        compiler_params=_cparams(("arbitrary",)),
    )(x, wt, mu, w0, a0, k_k, k_a, w12, g2p)


GROUP = 2


def _block_diag(x):
    head = lax.broadcasted_iota(jnp.int32, x.shape, 1) // HEAD
    zero = jnp.zeros_like(x)
    return jnp.concatenate([jnp.where(head == h, x, zero) for h in range(x.shape[1] // HEAD)], axis=0)


def _chunk_step(rs, lws, k2s, vs, kks, bs, sts):
    C = CHUNK
    GL = rs[0].shape[1]
    idx = range(len(rs))
    row = lax.broadcasted_iota(jnp.int32, (C, GL), 0)
    tcol = lax.broadcasted_iota(jnp.int32, (C, GL), 1) % HEAD
    strict = row > tcol
    incl = row >= tcol
    eye = jnp.where(row == tcol, 1.0, 0.0)
    bdb = lambda x: _block_diag(x).astype(BF16)

    cums, tots, Mbs, Mks, ARSs = [], [], [], [], []
    for i in idx:
        cm = lws[i]
        sh = 1
        while sh < C:
            cm = cm + jnp.where(row >= sh, pltpu.roll(cm, sh, axis=0), 0.0)
            sh *= 2
        e_neg = jnp.exp(-cm)
        AR = jnp.concatenate([-kks[i] * jnp.exp(cm - lws[i]), rs[i] * jnp.exp(cm)], axis=0).astype(BF16)
        cums.append(cm)
        tots.append(cm[C - 1:C, :])
        Mbk = _dot_nt(AR, jnp.concatenate([bdb(bs[i] * e_neg), bdb(k2s[i] * e_neg)], axis=0))
        Mbs.append(Mbk[:, 0:GL])
        Mks.append(Mbk[:, GL:2 * GL])
        ARSs.append(_dot_nt(AR, sts[i].astype(BF16)))
    vbds = [bdb(v) for v in vs]

    Ls = [jnp.where(strict, Mbs[i][0:C], 0.0) for i in idx]
    Ps = [eye + L for L in Ls]
    Lps = [_dot(L.astype(BF16), bdb(L)) for L in Ls]
    Xs = [ARSs[i][0:C] + _dot(jnp.where(strict, Mks[i][0:C], 0.0).astype(BF16), vbds[i]) for i in idx]
    m = 2
    while 2 * m < C:
        both = [_dot(jnp.concatenate([Lps[i], Ps[i]], axis=0).astype(BF16), bdb(Lps[i])) for i in idx]
        Ps = [Ps[i] + both[i][C:2 * C] for i in idx]
        Lps = [bt[0:C] for bt in both]
        m *= 2
    Ps = [Ps[i] + _dot(Ps[i].astype(BF16), bdb(Lps[i])) for i in idx]

    Us = [_dot(Ps[i].astype(BF16), bdb(Xs[i])) for i in idx]
    Ys = [ARSs[i][C:2 * C]
          + _dot(jnp.concatenate([jnp.where(incl, Mbs[i][C:2 * C], 0.0),
                                  jnp.where(incl, Mks[i][C:2 * C], 0.0)], axis=1).astype(BF16),
                 jnp.concatenate([bdb(Us[i]), vbds[i]], axis=0)) for i in idx]
    same_head = _same_head(GL)
    st_news = []
    for i in idx:
        e_rem = jnp.exp(tots[i] - cums[i])
        UV = jnp.concatenate([Us[i], vs[i]], axis=0).astype(BF16)
        BK = jnp.concatenate([bs[i] * e_rem, k2s[i] * e_rem], axis=0).astype(BF16)
        st_news.append(jnp.where(same_head, sts[i] * jnp.exp(tots[i]) + _dot_tn(UV, BK), 0.0))
    return Ys, st_news


def _rwkv_chunk_kernel(r_ref, lw_ref, k2_ref, v_ref, kk_ref, b_ref, g_ref, rk_ref, gng_ref, gnb_ref,
                       *rest, nbatch, ngroups, n_ride):
    ride_in, o_ref, ride_out, st_ref = rest[:n_ride], rest[n_ride], rest[n_ride + 1:-1], rest[-1]
    for w_ref, wb_ref in zip(ride_in, ride_out):
        wb_ref[...] = w_ref[...].astype(BF16)
    c = pl.program_id(2)

    @pl.when(c == 0)
    def _():
        st_ref[...] = jnp.zeros_like(st_ref)

    GL = st_ref.shape[1]
    ones = _head_ones(GL)
    chains = [(bi, p) for bi in range(nbatch) for p in range(ngroups)]
    sl = lambda p: slice(p * GL, (p + 1) * GL)
    get = lambda ref: [ref[bi, :, sl(p)] for bi, p in chains]
    rs, k2s, vs = get(r_ref), get(k2_ref), get(v_ref)
    ys, st_news = _chunk_step(rs, get(lw_ref), k2s, vs, get(kk_ref), get(b_ref),
                              [st_ref[i] for i in range(len(chains))])
    for i in range(len(chains)):
        st_ref[i] = st_news[i]
    ds = [y - _head_sum(y, ones, split=False) * (1.0 / HEAD) for y in ys]
    vars_ = [_head_sum(d * d, ones, split=False) * (1.0 / HEAD) for d in ds]
    for i, (bi, p) in enumerate(chains):
        yn = ds[i] * lax.rsqrt(vars_[i] + GN_EPS) * gng_ref[:, sl(p)] + gnb_ref[:, sl(p)]
        bonus = _head_sum(rs[i] * k2s[i] * rk_ref[:, sl(p)], ones, split=False) * vs[i]
        o_ref[bi, :, sl(p)] = ((yn + bonus) * g_ref[bi, :, sl(p)]).astype(o_ref.dtype)


def rwkv_chunk(r, lw, k2, v, kk, b, g, r_k, gn_g, gn_b, *, ride_along=(), nbatch=4, ngroups=8):
    B, S, W = r.shape
    GL = min(GROUP * HEAD, W)
    ngroups = min(ngroups, W // GL)
    nbatch = min(nbatch, B)
    wb = ngroups * GL
    grid = (B // nbatch, W // wb, S // CHUNK)
    n_steps = grid[0] * grid[1] * grid[2]
    seq = pl.BlockSpec((nbatch, CHUNK, wb), lambda bi, p, c: (bi, c, p))
    par = pl.BlockSpec((1, wb), lambda bi, p, c: (0, p))
    rides = [w for w in ride_along if w.shape[0] % (n_steps * BF16_ROWS) == 0]
    slab = lambda w: pl.BlockSpec((w.shape[0] // n_steps, w.shape[1]),
                                  lambda bi, p, c: ((bi * grid[1] + p) * grid[2] + c, 0))
    outs = pl.pallas_call(
        functools.partial(_rwkv_chunk_kernel, nbatch=nbatch, ngroups=ngroups, n_ride=len(rides)),
        grid=grid,
        in_specs=[seq] * 7 + [par] * 3 + [slab(w) for w in rides],
        out_specs=[seq] + [slab(w) for w in rides],
        out_shape=[jax.ShapeDtypeStruct((B, S, W), BF16)] + [jax.ShapeDtypeStruct(w.shape, BF16) for w in rides],
        scratch_shapes=[pltpu.VMEM((nbatch * ngroups, GL, GL), F32)],
        compiler_params=_cparams(("parallel", "parallel", "arbitrary")),
    )(r, lw, k2, v, kk, b, g, r_k, gn_g, gn_b, *rides)
    cast = iter(outs[1:])
    return outs[0], tuple(next(cast) if any(w is q for q in rides) else w.astype(BF16) for w in ride_along)


def _out_ln_kernel(x_ref, oa_ref, or_ref, wa_ref, wr_ref, g_ref, b_ref, o_ref, *, alpha):
    half = x_ref.shape[0] // 2
    rows = [slice(0, half), slice(half, 2 * half)]
    mixes = [_dot(oa_ref[r, :], wa_ref[...]) + _dot(or_ref[r, :], wr_ref[...]) for r in rows]
    for r, mix in zip(rows, mixes):
        o_ref[r, :] = _layer_norm(alpha * x_ref[r, :] + mix, g_ref[...], b_ref[...])


def out_ln(x, oa, orw, wa, wr, g, b, *, alpha, tm=512):
    T, D = x.shape
    tm = min(tm, T)
    ka, kr = oa.shape[1], orw.shape[1]
    return pl.pallas_call(
        functools.partial(_out_ln_kernel, alpha=alpha),
        grid=(T // tm,),
        in_specs=[
            pl.BlockSpec((tm, D), lambda i: (i, 0)),
            pl.BlockSpec((tm, ka), lambda i: (i, 0)),
            pl.BlockSpec((tm, kr), lambda i: (i, 0)),
            pl.BlockSpec((ka, D), lambda i: (0, 0)),
            pl.BlockSpec((kr, D), lambda i: (0, 0)),
            pl.BlockSpec((1, D), lambda i: (0, 0)),
            pl.BlockSpec((1, D), lambda i: (0, 0)),
        ],
        out_specs=pl.BlockSpec((tm, D), lambda i: (i, 0)),
        out_shape=jax.ShapeDtypeStruct((T, D), F32),
        compiler_params=_cparams(("parallel",)),
    )(x, oa, orw, wa, wr, g, b)


def _layer(x, l, ffn1_w_gate, ffn1_w_up, ffn1_w_down, ln1_g, ln1_b, w_in,
           lambda_q1, lambda_k1, lambda_q2, lambda_k2, attn_norm_g,
           rwkv_mu, rwkv_w0, rwkv_w2, rwkv_a0, rwkv_a2, rwkv_g2,
           rwkv_k_k, rwkv_k_a, rwkv_r_k, rwkv_gn_g, rwkv_gn_b,
           w_out, ln2_g, ln2_b, ffn2_w_gate, ffn2_w_up, ffn2_w_down, ln3_g, ln3_b):
    B, S, D = x.shape
    T = B * S
    alpha = (2.0 * DEPTH) ** 0.25
    lambda_init = 0.8 - 0.6 * math.exp(-0.3 * l)
    rw = rwkv_w0.shape[-1]
    aw = w_out.shape[1] - rw
    H = aw // LANES
    n_wd, n_ad, n_gd = rwkv_w2.shape[1], rwkv_a2.shape[1], rwkv_g2.shape[1]
    assert n_wd == HEAD and n_ad == HEAD and n_gd <= 2 * LANES
    row = lambda a: a.reshape(1, -1)
    bf = lambda a: a.astype(BF16)

    x1 = ffn_ln(x.reshape(T, D), ffn1_w_gate[l], ffn1_w_up[l], ffn1_w_down[l],
                row(ln1_g[l]), row(ln1_b[l]), alpha=alpha)

    wi = jnp.swapaxes(w_in[l], 0, 1)
    p_attn = proj_attn(x1, wi, n_cols=3 * aw, n_scaled_cols=aw, scale=HEAD ** -0.5 * math.log2(math.e))
    tab, feat = alibi_tables(H, S)
    o_attn = diff_attn(p_attn, tab, feat, row(lambda_q1[l]), row(lambda_k1[l]), row(lambda_q2[l]),
                       row(lambda_k2[l]), row(attn_norm_g[l]), B=B, S=S, H=H, lambda_init=lambda_init)

    mu = jnp.pad(rwkv_mu[l], (0, RWKV_TAIL - (n_wd + n_ad + n_gd)))
    w12 = jnp.zeros((LANES, 2 * rw), F32)
    w12 = w12.at[:HEAD, :rw].set(rwkv_w2[l]).at[HEAD:, rw:].set(rwkv_a2[l])
    g2p = jnp.pad(rwkv_g2[l], ((0, 2 * LANES - n_gd), (0, 0)))
    seqs = proj_rwkv(x1, wi, row(mu), row(rwkv_w0[l]), row(rwkv_a0[l]), row(rwkv_k_k[l]), row(rwkv_k_a[l]),
                     bf(w12), bf(g2p), S=S, col0=3 * aw, width=rw)
    r, lw, k2, v, kk, b, g = [t.reshape(B, S, rw) for t in seqs]
    o_rwkv, ffn2_w = rwkv_chunk(r, lw, k2, v, kk, b, g, row(rwkv_r_k[l]), row(rwkv_gn_g[l]), row(rwkv_gn_b[l]),
                                ride_along=(ffn2_w_gate[l], ffn2_w_up[l], ffn2_w_down[l]))

    wo = w_out[l]
    x2 = out_ln(x1, o_attn, o_rwkv.reshape(T, rw), bf(wo[:aw]), bf(wo[aw:]),
                row(ln2_g[l]), row(ln2_b[l]), alpha=alpha)
    x3 = ffn_ln(x2, *ffn2_w, row(ln3_g[l]), row(ln3_b[l]), alpha=alpha, tf=2 * FFN_TF)
    return x3.reshape(B, S, D)


def kernel(x, ffn1_w_gate, ffn1_w_up, ffn1_w_down, ln1_g, ln1_b, w_in, lambda_q1, lambda_k1, lambda_q2, lambda_k2, attn_norm_g, rwkv_mu, rwkv_w0, rwkv_w2, rwkv_a0, rwkv_a2, rwkv_g2, rwkv_k_k, rwkv_k_a, rwkv_r_k, rwkv_gn_g, rwkv_gn_b, w_out, ln2_g, ln2_b, ffn2_w_gate, ffn2_w_up, ffn2_w_down, ln3_g, ln3_b):
    for l in range(DEPTH):
        x = _layer(x, l, ffn1_w_gate, ffn1_w_up, ffn1_w_down, ln1_g, ln1_b, w_in,
                   lambda_q1, lambda_k1, lambda_q2, lambda_k2, attn_norm_g,
                   rwkv_mu, rwkv_w0, rwkv_w2, rwkv_a0, rwkv_a2, rwkv_g2,
                   rwkv_k_k, rwkv_k_a, rwkv_r_k, rwkv_gn_g, rwkv_gn_b,
                   w_out, ln2_g, ln2_b, ffn2_w_gate, ffn2_w_up, ffn2_w_down, ln3_g, ln3_b)
    return x
```

```python
import functools
import math

import jax
import jax.numpy as jnp
from jax import lax
from jax.experimental import pallas as pl
from jax.experimental.pallas import tpu as pltpu

F32 = jnp.float32
BF16 = jnp.bfloat16

DEPTH = 1
LN_EPS = 1e-5
ATTN_NORM_EPS = 1e-5
GN_EPS = 64e-5
HEAD = 64
LANES = 128
CHUNK = 64
VMEM_LIMIT = 56 * 1024 * 1024


def _cparams(sem):
    return pltpu.CompilerParams(dimension_semantics=sem, vmem_limit_bytes=VMEM_LIMIT)


def _layer_norm(y, g, b):
    mu = jnp.mean(y, axis=-1, keepdims=True)
    d = y - mu
    var = jnp.mean(d * d, axis=-1, keepdims=True)
    return d * lax.rsqrt(var + LN_EPS) * g + b


def _dot(a, b):
    return jnp.dot(a, b, preferred_element_type=F32)


def _dot_nt(a, b):
    return lax.dot_general(a, b, (((1,), (1,)), ((), ())), preferred_element_type=F32)


def _dot_tn(a, b):
    return lax.dot_general(a, b, (((0,), (0,)), ((), ())), preferred_element_type=F32)


def _ffn_ln_kernel(x_ref, wg_ref, wu_ref, wd_ref, g_ref, b_ref, o_ref, xb_ref, *, alpha, ncol):
    j = pl.program_id(1)
    nj = pl.num_programs(1)
    tm, D = o_ref.shape
    cols = [slice(n * ncol, (n + 1) * ncol) for n in range(D // ncol)]

    def body(first, final):
        nrow = 2 if (first or final) else 1
        rows = [slice(r * tm // nrow, (r + 1) * tm // nrow) for r in range(nrow)]
        if first:
            for r in rows:
                x = x_ref[r, :]
                xb_ref[r, :] = x.astype(BF16)
                o_ref[r, :] = alpha * x
        wg = wg_ref[...].astype(BF16)
        wu = wu_ref[...].astype(BF16)
        wd = [wd_ref[:, c].astype(BF16) for c in cols]
        hs = []
        for r in rows:
            xb = xb_ref[r, :]
            hg = _dot(xb, wg)
            hu = _dot(xb, wu)
            hs.append((0.5 * hg * jax.nn.sigmoid(hg) * hu).astype(BF16))
        for r, h in zip(rows, hs):
            for c, w in zip(cols, wd):
                o_ref[r, c] += _dot(h, w)
        if final:
            for r in rows:
                o_ref[r, :] = _layer_norm(o_ref[r, :], g_ref[...], b_ref[...])

    pl.when(j == 0)(lambda: body(True, False))
    pl.when(jnp.logical_and(j > 0, j < nj - 1))(lambda: body(False, False))
    pl.when(jnp.logical_and(j > 0, j == nj - 1))(lambda: body(False, True))


FFN_TF = 256


def ffn_ln(x, wg, wu, wd, g, b, *, alpha, tm=1024, tf=FFN_TF, ncol=512):
    T, D = x.shape
    Fd = wg.shape[1]
    tm = min(tm, T)
    tf = min(tf, Fd // 2)
    ncol = min(ncol, D)
    return pl.pallas_call(
        functools.partial(_ffn_ln_kernel, alpha=alpha, ncol=ncol),
        grid=(T // tm, Fd // tf),
        in_specs=[
            pl.BlockSpec((tm, D), lambda i, j: (i, 0)),
            pl.BlockSpec((D, tf), lambda i, j: (0, j)),
            pl.BlockSpec((D, tf), lambda i, j: (0, j)),
            pl.BlockSpec((tf, D), lambda i, j: (j, 0)),
            pl.BlockSpec((1, D), lambda i, j: (0, 0)),
            pl.BlockSpec((1, D), lambda i, j: (0, 0)),
        ],
        out_specs=pl.BlockSpec((tm, D), lambda i, j: (i, 0)),
        out_shape=jax.ShapeDtypeStruct((T, D), F32),
        scratch_shapes=[pltpu.VMEM((tm, D), BF16)],
        compiler_params=_cparams(("parallel", "arbitrary")),
    )(x, wg, wu, wd, g, b)


def _pick_tile(limit, *sizes):
    t = limit // LANES * LANES
    while any(s % t for s in sizes):
        t -= LANES
    return t


def _stage_weight_tile(wt_ref, wb_ref, s, first_tile, n_cols):
    tile = wt_ref.shape[0]
    col = (first_tile + s) * tile + lax.broadcasted_iota(jnp.int32, wt_ref.shape, 0)
    wb_ref[pl.ds(pl.multiple_of(s * tile, tile), tile), :] = jnp.where(col < n_cols, wt_ref[...], 0.0).astype(BF16)


def _proj_attn_kernel(x_ref, wt_ref, o_ref, wb_ref, *, n_wt, n_cols, n_scaled, scale):
    s = pl.program_id(0)

    @pl.when(s < n_wt)
    def _():
        _stage_weight_tile(wt_ref, wb_ref, s, 0, n_cols)

    @pl.when(s >= n_wt)
    def _():
        tile = wt_ref.shape[0]
        xb = x_ref[...].astype(BF16)
        for n in range(n_wt):
            acc = _dot_nt(xb, wb_ref[n * tile:(n + 1) * tile, :])
            if n < n_scaled:
                acc = acc * scale
            o_ref[:, n * tile:(n + 1) * tile] = acc.astype(o_ref.dtype)


def proj_attn(x, wt, *, n_cols, n_scaled_cols, scale, tm=512, tile=512):
    T, D = x.shape
    tm = min(tm, T)
    tile = _pick_tile(tile, n_cols, n_scaled_cols)
    n_wt = n_cols // tile
    tok = lambda s: (jnp.maximum(s - n_wt, 0), 0)
    return pl.pallas_call(
        functools.partial(_proj_attn_kernel, n_wt=n_wt, n_cols=wt.shape[0],
                          n_scaled=n_scaled_cols // tile, scale=scale),
        grid=(n_wt + T // tm,),
        in_specs=[
            pl.BlockSpec((tm, D), tok),
            pl.BlockSpec((tile, D), lambda s: (jnp.minimum(s, n_wt - 1), 0)),
        ],
        out_specs=pl.BlockSpec((tm, n_cols), tok),
        out_shape=jax.ShapeDtypeStruct((T, n_cols), BF16),
        scratch_shapes=[pltpu.VMEM((n_cols, D), BF16)],
        compiler_params=_cparams(("arbitrary",)),
    )(x, wt)


ATT_TQ = 256
ATT_TK = 512
N_BIAS_TERMS = 3
BF16_ROWS = 16
VT_ROWS = LANES + BF16_ROWS


def alibi_tables(H, S):
    slope = jnp.exp2(-8.0 * (jnp.arange(H, dtype=F32) + 1.0) / H) * math.log2(math.e)
    terms, rest = [], slope
    for _ in range(N_BIAS_TERMS):
        t = rest.astype(BF16).astype(F32)
        terms += [64.0 * t, t]
        rest = rest - t
    tab = jnp.stack(terms + [jnp.zeros_like(slope)] * (8 - len(terms)), axis=1)
    pos = jnp.arange(S, dtype=jnp.int32)
    hi, lo = (pos // 64).astype(F32), (pos % 64).astype(F32)
    feat = jnp.stack([hi, lo] * N_BIAS_TERMS, axis=1)
    feat = jnp.pad(feat, ((0, 0), (0, LANES - feat.shape[1]))).astype(BF16)
    return tab, feat


def _diff_attn_kernel(tab_ref, q_ref, k_ref, v_ref, feat_ref, lq1_ref, lk1_ref, lq2_ref, lk2_ref, ng_ref,
                      o_ref, ka_ref, vt_ref, qa_ref, acc_ref, m_ref, *, hb, nck, lambda_init):
    tq = ATT_TQ
    tk = vt_ref.shape[2]
    g = pl.program_id(1)
    i = pl.program_id(2)
    heads = range(hb)
    hs = lambda h: slice(h * LANES, (h + 1) * LANES)

    @pl.when(i == 0)
    def _():
        for h in heads:
            ka_ref[h, :, 0:LANES] = k_ref[:, hs(h)]
            ka_ref[h, :, LANES:2 * LANES] = feat_ref[...]
            for c in range(nck):
                vt_ref[h * nck + c, 0:LANES, :] = v_ref[c * tk:(c + 1) * tk, hs(h)].astype(F32).T.astype(BF16)
                vt_ref[h * nck + c, LANES:VT_ROWS, :] = jnp.ones((BF16_ROWS, tk), BF16)

    lane = lax.broadcasted_iota(jnp.int32, (tq, LANES), 1)
    for h in heads:
        q = q_ref[:, hs(h)]
        zero = jnp.zeros_like(q)
        qf = jnp.zeros((tq, LANES), F32)
        for n in range(2 * N_BIAS_TERMS):
            qf = jnp.where(lane == n, tab_ref[g * hb + h, n], qf)
        qf = qf.astype(BF16)
        qa_ref[h, 0:tq, 0:LANES] = jnp.where(lane < HEAD, q, zero)
        qa_ref[h, tq:2 * tq, 0:LANES] = jnp.where(lane >= HEAD, q, zero)
        qa_ref[h, 0:tq, LANES:2 * LANES] = qf
        qa_ref[h, tq:2 * tq, LANES:2 * LANES] = qf
        m_ref[h] = jnp.full((1, 2 * tq), -1e30, F32)
        acc_ref[h] = jnp.zeros((VT_ROWS, 2 * tq), F32)

    def chunk(c, part=None):
        start = pl.multiple_of(c * tk, tk)
        nk, lanes = tk, slice(None)
        if part is not None:
            start, nk, lanes = start + part * tq, tq, slice(part * tq, (part + 1) * tq)
        ss = [_dot_nt(ka_ref[h, pl.ds(start, nk), :], qa_ref[h]) for h in heads]
        if part is not None:
            key = start + lax.broadcasted_iota(jnp.int32, (nk, 2 * tq), 0)
            qcol = lax.broadcasted_iota(jnp.int32, (nk, 2 * tq), 1)
            qpos = i * tq + jnp.where(qcol >= tq, qcol - tq, qcol)
            keep = key <= qpos
            ss = [jnp.where(keep, s, -jnp.inf) for s in ss]
        m_olds = [m_ref[h] for h in heads]
        m_news = [jnp.maximum(m_olds[h], jnp.max(ss[h], axis=0, keepdims=True)) for h in heads]
        ps = [jnp.exp2(ss[h] - m_news[h]).astype(BF16) for h in heads]
        alphas = [jnp.exp2(m_olds[h] - m_news[h]) for h in heads]
        pvs = [_dot(vt_ref[h * nck + c, :, lanes], ps[h]) for h in heads]
        for h in heads:
            m_ref[h] = m_news[h]
            acc_ref[h] = alphas[h] * acc_ref[h] + pvs[h]

    ratio = tk // tq
    n_full = i // ratio

    def body(c, carry):
        chunk(c)
        return carry

    lax.fori_loop(0, n_full, body, 0)
    chunk(n_full, 0)
    for part in range(1, ratio):
        pl.when(i % ratio >= part)(functools.partial(chunk, n_full, part))

    lam = (jnp.exp(jnp.sum(lq1_ref[...] * lk1_ref[...], axis=-1, keepdims=True))
           - jnp.exp(jnp.sum(lq2_ref[...] * lk2_ref[...], axis=-1, keepdims=True)) + lambda_init)
    for h in heads:
        acc = acc_ref[h]
        rl = 1.0 / acc[LANES:LANES + 1, :]
        ot = (acc[0:LANES, 0:tq] * rl[:, 0:tq]
              - lam * (acc[0:LANES, tq:2 * tq] * rl[:, tq:2 * tq]))
        o = ot.T
        o = o * lax.rsqrt(jnp.mean(o * o, axis=-1, keepdims=True) + ATTN_NORM_EPS) * ng_ref[...]
        o_ref[:, hs(h)] = (o * (1.0 - lambda_init)).astype(o_ref.dtype)


def diff_attn(pa, tab, feat, lq1, lk1, lq2, lk2, norm_g, *, B, S, H, lambda_init, hb=8):
    hb = min(hb, H)
    tq, tk = ATT_TQ, min(ATT_TK, S)
    assert S % tk == 0 and H % hb == 0 and tk % tq == 0
    nq, nck, ng = S // tq, S // tk, H // hb
    wb = hb * LANES
    small = lambda n: pl.BlockSpec((1, n), lambda b, g, i: (0, 0))
    return pl.pallas_call(
        functools.partial(_diff_attn_kernel, hb=hb, nck=nck, lambda_init=lambda_init),
        grid=(B, ng, nq),
        in_specs=[
            pl.BlockSpec(memory_space=pltpu.SMEM),
            pl.BlockSpec((tq, wb), lambda b, g, i: (b * nq + i, g)),
            pl.BlockSpec((S, wb), lambda b, g, i: (b, ng + g)),
            pl.BlockSpec((S, wb), lambda b, g, i: (b, 2 * ng + g)),
            pl.BlockSpec((S, LANES), lambda b, g, i: (0, 0)),
            small(HEAD), small(HEAD), small(HEAD), small(HEAD), small(2 * HEAD),
        ],
        out_specs=pl.BlockSpec((tq, wb), lambda b, g, i: (b * nq + i, g)),
        out_shape=jax.ShapeDtypeStruct((B * S, H * LANES), BF16),
        scratch_shapes=[
            pltpu.VMEM((hb, S, 2 * LANES), BF16),
            pltpu.VMEM((hb * nck, VT_ROWS, tk), BF16),
            pltpu.VMEM((hb, 2 * tq, 2 * LANES), BF16),
            pltpu.VMEM((hb, VT_ROWS, 2 * tq), F32),
            pltpu.VMEM((hb, 1, 2 * tq), F32),
        ],
        compiler_params=_cparams(("parallel", "parallel", "arbitrary")),
    )(tab, pa, pa, pa, feat, lq1, lk1, lq2, lk2, norm_g)


def _same_head(n):
    r = lax.broadcasted_iota(jnp.int32, (n, n), 0) // HEAD
    c = lax.broadcasted_iota(jnp.int32, (n, n), 1) // HEAD
    return r == c


def _head_ones(n):
    return jnp.where(_same_head(n), 1.0, 0.0).astype(BF16)


def _head_sum(x, ones, split=True):
    hi = x.astype(BF16)
    if not split:
        return _dot(hi, ones)
    lo = (x - hi.astype(F32)).astype(BF16)
    return _dot(hi, ones) + _dot(lo, ones)


RWKV_TAIL = 3 * LANES


def _proj_rwkv_kernel(x_ref, wt_ref, mu_ref, w0_ref, a0_ref, kk_ref, ka_ref, w12_ref, g2_ref,
                      r_ref, lw_ref, k2_ref, v_ref, kkn_ref, b_ref, g_ref, wb_ref, pr_ref, prev_ref,
                      *, n_wt, first_tile, n_cols, width, tiles_per_seq):
    s = pl.program_id(0)

    @pl.when(s == 0)
    def _():
        prev_ref[...] = jnp.zeros_like(prev_ref)

    @pl.when(s < n_wt)
    def _():
        _stage_weight_tile(wt_ref, wb_ref, s, first_tile, n_cols)

    @pl.when(s >= n_wt)
    def _():
        tm = x_ref.shape[0]
        xb = x_ref[...].astype(BF16)
        c3 = 3 * width
        first = (s - n_wt) % tiles_per_seq == 0
        row = lax.broadcasted_iota(jnp.int32, (tm, 1), 0)

        def project(c0, c1):
            pr_ref[:, c0:c1] = _dot_nt(xb, wb_ref[c0:c1, :])

        def mixed(c0, c1):
            x = pr_ref[:, c0:c1]
            prow = jnp.where(first, 0.0, prev_ref[0:1, c0:c1])
            xs = jnp.where(row == 0, prow, pltpu.roll(x, 1, axis=0))
            return x + mu_ref[:, c0:c1] * (xs - x)

        project(c3, c3 + RWKV_TAIL)
        project(width, 2 * width)

        lora_in = mixed(c3, c3 + LANES)
        lane = lax.broadcasted_iota(jnp.int32, lora_in.shape, 1)
        lora_in = jnp.where(lane < HEAD, jnp.tanh(lora_in), lora_in)
        z = _dot(lora_in.astype(BF16), w12_ref[...])
        g_ref[...] = _dot(jax.nn.sigmoid(mixed(c3 + LANES, c3 + RWKV_TAIL)).astype(BF16), g2_ref[...])
        lw_ref[...] = -math.exp(-0.5) * jax.nn.sigmoid(w0_ref[...] + z[:, 0:width])
        a = jax.nn.sigmoid(a0_ref[...] + z[:, width:2 * width])

        project(0, width)

        k = mixed(width, 2 * width)
        k2_ref[...] = k * (1.0 + (a - 1.0) * ka_ref[...])
        kx = k * kk_ref[...]
        gw = min(2 * LANES, width)
        ones = _head_ones(gw)
        ss = jnp.concatenate(
            [_head_sum(kx[:, t * gw:(t + 1) * gw] * kx[:, t * gw:(t + 1) * gw], ones)
             for t in range(width // gw)], axis=1)
        kkn = kx / jnp.maximum(jnp.sqrt(ss), 1e-12)
        kkn_ref[...] = kkn
        b_ref[...] = kkn * a

        project(2 * width, c3)

        r_ref[...] = mixed(0, width)
        v_ref[...] = mixed(2 * width, c3)
        prev_ref[0:1, :] = pr_ref[tm - 1:tm, :]


def proj_rwkv(x, wt, mu, w0, a0, k_k, k_a, w12, g2p, *, S, col0, width, tm=256, tile=384):
    T, D = x.shape
    tm = min(tm, S)
    W = 3 * width + RWKV_TAIL
    tile = _pick_tile(tile, col0, W)
    assert S % tm == 0
    n_wt, first_tile = W // tile, col0 // tile
    tok = lambda s: (jnp.maximum(s - n_wt, 0), 0)
    row = lambda n: pl.BlockSpec((1, n), lambda s: (0, 0))
    out = pl.BlockSpec((tm, width), tok)
    return pl.pallas_call(
        functools.partial(_proj_rwkv_kernel, n_wt=n_wt, first_tile=first_tile, n_cols=wt.shape[0],
                          width=width, tiles_per_seq=S // tm),
        grid=(n_wt + T // tm,),
        in_specs=[
            pl.BlockSpec((tm, D), tok),
            pl.BlockSpec((tile, D), lambda s: (first_tile + jnp.minimum(s, n_wt - 1), 0)),
            row(W), row(width), row(width), row(width), row(width),
            pl.BlockSpec((LANES, 2 * width), lambda s: (0, 0)),
            pl.BlockSpec((2 * LANES, width), lambda s: (0, 0)),
        ],
        out_specs=[out] * 7,
        out_shape=[jax.ShapeDtypeStruct((T, width), F32)] * 7,
        scratch_shapes=[pltpu.VMEM((W, D), BF16), pltpu.VMEM((tm, W), F32), pltpu.VMEM((8, W), F32)],
        compiler_params=_cparams(("arbitrary",)),
    )(x, wt, mu, w0, a0, k_k, k_a, w12, g2p)


GROUP = 2


def _block_diag(x):
    head = lax.broadcasted_iota(jnp.int32, x.shape, 1) // HEAD
    zero = jnp.zeros_like(x)
    return jnp.concatenate([jnp.where(head == h, x, zero) for h in range(x.shape[1] // HEAD)], axis=0)


def _chunk_step(rs, lws, k2s, vs, kks, bs, sts):
    C = CHUNK
    GL = rs[0].shape[1]
    idx = range(len(rs))
    row = lax.broadcasted_iota(jnp.int32, (C, GL), 0)
    tcol = lax.broadcasted_iota(jnp.int32, (C, GL), 1) % HEAD
    strict = row > tcol
    incl = row >= tcol
    eye = jnp.where(row == tcol, 1.0, 0.0)
    bdb = lambda x: _block_diag(x).astype(BF16)

    cums, tots, Mbs, Mks, ARSs = [], [], [], [], []
    for i in idx:
        cm = lws[i]
        sh = 1
        while sh < C:
            cm = cm + jnp.where(row >= sh, pltpu.roll(cm, sh, axis=0), 0.0)
            sh *= 2
        e_neg = jnp.exp(-cm)
        AR = jnp.concatenate([-kks[i] * jnp.exp(cm - lws[i]), rs[i] * jnp.exp(cm)], axis=0).astype(BF16)
        cums.append(cm)
        tots.append(cm[C - 1:C, :])
        Mbk = _dot_nt(AR, jnp.concatenate([bdb(bs[i] * e_neg), bdb(k2s[i] * e_neg)], axis=0))
        Mbs.append(Mbk[:, 0:GL])
        Mks.append(Mbk[:, GL:2 * GL])
        ARSs.append(_dot_nt(AR, sts[i].astype(BF16)))
    vbds = [bdb(v) for v in vs]

    Ls = [jnp.where(strict, Mbs[i][0:C], 0.0) for i in idx]
    Ps = [eye + L for L in Ls]
    Lps = [_dot(L.astype(BF16), bdb(L)) for L in Ls]
    Xs = [ARSs[i][0:C] + _dot(jnp.where(strict, Mks[i][0:C], 0.0).astype(BF16), vbds[i]) for i in idx]
    m = 2
    while 2 * m < C:
        both = [_dot(jnp.concatenate([Lps[i], Ps[i]], axis=0).astype(BF16), bdb(Lps[i])) for i in idx]
        Ps = [Ps[i] + both[i][C:2 * C] for i in idx]
        Lps = [bt[0:C] for bt in both]
        m *= 2
    Ps = [Ps[i] + _dot(Ps[i].astype(BF16), bdb(Lps[i])) for i in idx]

    Us = [_dot(Ps[i].astype(BF16), bdb(Xs[i])) for i in idx]
    Ys = [ARSs[i][C:2 * C]
          + _dot(jnp.concatenate([jnp.where(incl, Mbs[i][C:2 * C], 0.0),
                                  jnp.where(incl, Mks[i][C:2 * C], 0.0)], axis=1).astype(BF16),
                 jnp.concatenate([bdb(Us[i]), vbds[i]], axis=0)) for i in idx]
    same_head = _same_head(GL)
    st_news = []
    for i in idx:
        e_rem = jnp.exp(tots[i] - cums[i])
        UV = jnp.concatenate([Us[i], vs[i]], axis=0).astype(BF16)
        BK = jnp.concatenate([bs[i] * e_rem, k2s[i] * e_rem], axis=0).astype(BF16)
        st_news.append(jnp.where(same_head, sts[i] * jnp.exp(tots[i]) + _dot_tn(UV, BK), 0.0))
    return Ys, st_news


def _rwkv_chunk_kernel(r_ref, lw_ref, k2_ref, v_ref, kk_ref, b_ref, g_ref, rk_ref, gng_ref, gnb_ref,
                       *rest, nbatch, ngroups, n_ride):
    ride_in, o_ref, ride_out, st_ref = rest[:n_ride], rest[n_ride], rest[n_ride + 1:-1], rest[-1]
    for w_ref, wb_ref in zip(ride_in, ride_out):
        wb_ref[...] = w_ref[...].astype(BF16)
    c = pl.program_id(2)

    @pl.when(c == 0)
    def _():
        st_ref[...] = jnp.zeros_like(st_ref)

    GL = st_ref.shape[1]
    ones = _head_ones(GL)
    chains = [(bi, p) for bi in range(nbatch) for p in range(ngroups)]
    sl = lambda p: slice(p * GL, (p + 1) * GL)
    get = lambda ref: [ref[bi, :, sl(p)] for bi, p in chains]
    rs, k2s, vs = get(r_ref), get(k2_ref), get(v_ref)
    ys, st_news = _chunk_step(rs, get(lw_ref), k2s, vs, get(kk_ref), get(b_ref),
                              [st_ref[i] for i in range(len(chains))])
    for i in range(len(chains)):
        st_ref[i] = st_news[i]
    ds = [y - _head_sum(y, ones, split=False) * (1.0 / HEAD) for y in ys]
    vars_ = [_head_sum(d * d, ones, split=False) * (1.0 / HEAD) for d in ds]
    for i, (bi, p) in enumerate(chains):
        yn = ds[i] * lax.rsqrt(vars_[i] + GN_EPS) * gng_ref[:, sl(p)] + gnb_ref[:, sl(p)]
        bonus = _head_sum(rs[i] * k2s[i] * rk_ref[:, sl(p)], ones, split=False) * vs[i]
        o_ref[bi, :, sl(p)] = ((yn + bonus) * g_ref[bi, :, sl(p)]).astype(o_ref.dtype)


def rwkv_chunk(r, lw, k2, v, kk, b, g, r_k, gn_g, gn_b, *, ride_along=(), nbatch=4, ngroups=8):
    B, S, W = r.shape
    GL = min(GROUP * HEAD, W)
    ngroups = min(ngroups, W // GL)
    nbatch = min(nbatch, B)
    wb = ngroups * GL
    grid = (B // nbatch, W // wb, S // CHUNK)
    n_steps = grid[0] * grid[1] * grid[2]
    seq = pl.BlockSpec((nbatch, CHUNK, wb), lambda bi, p, c: (bi, c, p))
    par = pl.BlockSpec((1, wb), lambda bi, p, c: (0, p))
    rides = [w for w in ride_along if w.shape[0] % (n_steps * BF16_ROWS) == 0]
    slab = lambda w: pl.BlockSpec((w.shape[0] // n_steps, w.shape[1]),
                                  lambda bi, p, c: ((bi * grid[1] + p) * grid[2] + c, 0))
    outs = pl.pallas_call(
        functools.partial(_rwkv_chunk_kernel, nbatch=nbatch, ngroups=ngroups, n_ride=len(rides)),
        grid=grid,
        in_specs=[seq] * 7 + [par] * 3 + [slab(w) for w in rides],
        out_specs=[seq] + [slab(w) for w in rides],
        out_shape=[jax.ShapeDtypeStruct((B, S, W), BF16)] + [jax.ShapeDtypeStruct(w.shape, BF16) for w in rides],
        scratch_shapes=[pltpu.VMEM((nbatch * ngroups, GL, GL), F32)],
        compiler_params=_cparams(("parallel", "parallel", "arbitrary")),
    )(r, lw, k2, v, kk, b, g, r_k, gn_g, gn_b, *rides)
    cast = iter(outs[1:])
    return outs[0], tuple(next(cast) if any(w is q for q in rides) else w.astype(BF16) for w in ride_along)


def _out_ln_kernel(x_ref, oa_ref, or_ref, wa_ref, wr_ref, g_ref, b_ref, o_ref, *, alpha):
    half = x_ref.shape[0] // 2
    rows = [slice(0, half), slice(half, 2 * half)]
    mixes = [_dot(oa_ref[r, :], wa_ref[...]) + _dot(or_ref[r, :], wr_ref[...]) for r in rows]
    for r, mix in zip(rows, mixes):
        o_ref[r, :] = _layer_norm(alpha * x_ref[r, :] + mix, g_ref[...], b_ref[...])


def out_ln(x, oa, orw, w, g, b, *, alpha, tm=512):
    T, D = x.shape
    tm = min(tm, T)
    ka, kr = oa.shape[1], orw.shape[1]
    assert ka % kr == 0 and w.shape[0] == ka + kr
    return pl.pallas_call(
        functools.partial(_out_ln_kernel, alpha=alpha),
        grid=(T // tm,),
        in_specs=[
            pl.BlockSpec((tm, D), lambda i: (i, 0)),
            pl.BlockSpec((tm, ka), lambda i: (i, 0)),
            pl.BlockSpec((tm, kr), lambda i: (i, 0)),
            pl.BlockSpec((ka, D), lambda i: (0, 0)),
            pl.BlockSpec((kr, D), lambda i: (ka // kr, 0)),
            pl.BlockSpec((1, D), lambda i: (0, 0)),
            pl.BlockSpec((1, D), lambda i: (0, 0)),
        ],
        out_specs=pl.BlockSpec((tm, D), lambda i: (i, 0)),
        out_shape=jax.ShapeDtypeStruct((T, D), F32),
        compiler_params=_cparams(("parallel",)),
    )(x, oa, orw, w, w, g, b)


def _layer(x, l, ffn1_w_gate, ffn1_w_up, ffn1_w_down, ln1_g, ln1_b, w_in,
           lambda_q1, lambda_k1, lambda_q2, lambda_k2, attn_norm_g,
           rwkv_mu, rwkv_w0, rwkv_w2, rwkv_a0, rwkv_a2, rwkv_g2,
           rwkv_k_k, rwkv_k_a, rwkv_r_k, rwkv_gn_g, rwkv_gn_b,
           w_out, ln2_g, ln2_b, ffn2_w_gate, ffn2_w_up, ffn2_w_down, ln3_g, ln3_b):
    B, S, D = x.shape
    T = B * S
    alpha = (2.0 * DEPTH) ** 0.25
    lambda_init = 0.8 - 0.6 * math.exp(-0.3 * l)
    rw = rwkv_w0.shape[-1]
    aw = w_out.shape[1] - rw
    H = aw // LANES
    n_wd, n_ad, n_gd = rwkv_w2.shape[1], rwkv_a2.shape[1], rwkv_g2.shape[1]
    assert n_wd == HEAD and n_ad == HEAD and n_gd <= 2 * LANES
    row = lambda a: a.reshape(1, -1)
    bf = lambda a: a.astype(BF16)

    x1 = ffn_ln(x.reshape(T, D), ffn1_w_gate[l], ffn1_w_up[l], ffn1_w_down[l],
                row(ln1_g[l]), row(ln1_b[l]), alpha=alpha)

    wi = jnp.swapaxes(w_in[l], 0, 1)
    p_attn = proj_attn(x1, wi, n_cols=3 * aw, n_scaled_cols=aw, scale=HEAD ** -0.5 * math.log2(math.e))
    tab, feat = alibi_tables(H, S)
    o_attn = diff_attn(p_attn, tab, feat, row(lambda_q1[l]), row(lambda_k1[l]), row(lambda_q2[l]),
                       row(lambda_k2[l]), row(attn_norm_g[l]), B=B, S=S, H=H, lambda_init=lambda_init)

    mu = jnp.pad(rwkv_mu[l], (0, RWKV_TAIL - (n_wd + n_ad + n_gd)))
    w12 = jnp.zeros((LANES, 2 * rw), F32)
    w12 = w12.at[:HEAD, :rw].set(rwkv_w2[l]).at[HEAD:, rw:].set(rwkv_a2[l])
    g2p = jnp.pad(rwkv_g2[l], ((0, 2 * LANES - n_gd), (0, 0)))
    seqs = proj_rwkv(x1, wi, row(mu), row(rwkv_w0[l]), row(rwkv_a0[l]), row(rwkv_k_k[l]), row(rwkv_k_a[l]),
                     bf(w12), bf(g2p), S=S, col0=3 * aw, width=rw)
    r, lw, k2, v, kk, b, g = [t.reshape(B, S, rw) for t in seqs]
    o_rwkv, later_w = rwkv_chunk(r, lw, k2, v, kk, b, g, row(rwkv_r_k[l]), row(rwkv_gn_g[l]), row(rwkv_gn_b[l]),
                                 ride_along=(w_out[l], ffn2_w_gate[l], ffn2_w_up[l], ffn2_w_down[l]))

    x2 = out_ln(x1, o_attn, o_rwkv.reshape(T, rw), later_w[0], row(ln2_g[l]), row(ln2_b[l]), alpha=alpha)
    x3 = ffn_ln(x2, *later_w[1:], row(ln3_g[l]), row(ln3_b[l]), alpha=alpha, tf=2 * FFN_TF)
    return x3.reshape(B, S, D)


def kernel(x, ffn1_w_gate, ffn1_w_up, ffn1_w_down, ln1_g, ln1_b, w_in, lambda_q1, lambda_k1, lambda_q2, lambda_k2, attn_norm_g, rwkv_mu, rwkv_w0, rwkv_w2, rwkv_a0, rwkv_a2, rwkv_g2, rwkv_k_k, rwkv_k_a, rwkv_r_k, rwkv_gn_g, rwkv_gn_b, w_out, ln2_g, ln2_b, ffn2_w_gate, ffn2_w_up, ffn2_w_down, ln3_g, ln3_b):
    for l in range(DEPTH):
        x = _layer(x, l, ffn1_w_gate, ffn1_w_up, ffn1_w_down, ln1_g, ln1_b, w_in,
                   lambda_q1, lambda_k1, lambda_q2, lambda_k2, attn_norm_g,
                   rwkv_mu, rwkv_w0, rwkv_w2, rwkv_a0, rwkv_a2, rwkv_g2,
                   rwkv_k_k, rwkv_k_a, rwkv_r_k, rwkv_gn_g, rwkv_gn_b,
                   w_out, ln2_g, ln2_b, ffn2_w_gate, ffn2_w_up, ffn2_w_down, ln3_g, ln3_b)
    return x
```

```python
import functools
import math

import jax
import jax.numpy as jnp
from jax import lax
from jax.experimental import pallas as pl
from jax.experimental.pallas import tpu as pltpu

F32 = jnp.float32
BF16 = jnp.bfloat16

DEPTH = 1
LN_EPS = 1e-5
ATTN_NORM_EPS = 1e-5
GN_EPS = 64e-5
HEAD = 64
LANES = 128
SUBLANES = 8
CHUNK = 64
VMEM_LIMIT = 56 * 1024 * 1024
NEG_BIG = -1e30


def _cparams(sem):
    return pltpu.CompilerParams(dimension_semantics=sem, vmem_limit_bytes=VMEM_LIMIT)


def _layer_norm(y, g, b):
    mu = jnp.mean(y, axis=-1, keepdims=True)
    d = y - mu
    var = jnp.mean(d * d, axis=-1, keepdims=True)
    return d * lax.rsqrt(var + LN_EPS) * g + b


def _dot(a, b):
    return jnp.dot(a, b, preferred_element_type=F32)


def _dot_nt(a, b):
    return lax.dot_general(a, b, (((1,), (1,)), ((), ())), preferred_element_type=F32)


def _dot_tn(a, b):
    return lax.dot_general(a, b, (((0,), (0,)), ((), ())), preferred_element_type=F32)


def _ffn_ln_kernel(x_ref, wg_ref, wu_ref, wd_ref, g_ref, b_ref, o_ref, xb_ref, *, alpha, ncol):
    j = pl.program_id(1)
    nj = pl.num_programs(1)
    tm, D = o_ref.shape
    cols = [slice(n * ncol, (n + 1) * ncol) for n in range(D // ncol)]

    def body(first, final):
        nrow = 2 if (first or final) else 1
        rows = [slice(r * tm // nrow, (r + 1) * tm // nrow) for r in range(nrow)]
        if first:
            for r in rows:
                x = x_ref[r, :]
                xb_ref[r, :] = x.astype(BF16)
                o_ref[r, :] = alpha * x
        wg = wg_ref[...].astype(BF16)
        wu = wu_ref[...].astype(BF16)
        wd = [wd_ref[:, c].astype(BF16) for c in cols]
        hs = []
        for r in rows:
            xb = xb_ref[r, :]
            hg = _dot(xb, wg)
            hu = _dot(xb, wu)
            hs.append((0.5 * hg * jax.nn.sigmoid(hg) * hu).astype(BF16))
        for r, h in zip(rows, hs):
            for c, w in zip(cols, wd):
                o_ref[r, c] += _dot(h, w)
        if final:
            for r in rows:
                o_ref[r, :] = _layer_norm(o_ref[r, :], g_ref[...], b_ref[...])

    pl.when(j == 0)(lambda: body(True, False))
    pl.when(jnp.logical_and(j > 0, j < nj - 1))(lambda: body(False, False))
    pl.when(jnp.logical_and(j > 0, j == nj - 1))(lambda: body(False, True))


FFN_TF = 256


def ffn_ln(x, wg, wu, wd, g, b, *, alpha, tm=1024, tf=FFN_TF, ncol=512):
    T, D = x.shape
    Fd = wg.shape[1]
    tm = min(tm, T)
    tf = min(tf, Fd // 2)
    ncol = min(ncol, D)
    return pl.pallas_call(
        functools.partial(_ffn_ln_kernel, alpha=alpha, ncol=ncol),
        grid=(T // tm, Fd // tf),
        in_specs=[
            pl.BlockSpec((tm, D), lambda i, j: (i, 0)),
            pl.BlockSpec((D, tf), lambda i, j: (0, j)),
            pl.BlockSpec((D, tf), lambda i, j: (0, j)),
            pl.BlockSpec((tf, D), lambda i, j: (j, 0)),
            pl.BlockSpec((1, D), lambda i, j: (0, 0)),
            pl.BlockSpec((1, D), lambda i, j: (0, 0)),
        ],
        out_specs=pl.BlockSpec((tm, D), lambda i, j: (i, 0)),
        out_shape=jax.ShapeDtypeStruct((T, D), F32),
        scratch_shapes=[pltpu.VMEM((tm, D), BF16)],
        compiler_params=_cparams(("parallel", "arbitrary")),
    )(x, wg, wu, wd, g, b)


def _pick_tile(limit, *sizes):
    t = limit // LANES * LANES
    while any(s % t for s in sizes):
        t -= LANES
    return t


def _stage_weight_tile(wt_ref, wb_ref, s, first_tile, n_cols):
    tile = wt_ref.shape[0]
    col = (first_tile + s) * tile + lax.broadcasted_iota(jnp.int32, wt_ref.shape, 0)
    wb_ref[pl.ds(pl.multiple_of(s * tile, tile), tile), :] = jnp.where(col < n_cols, wt_ref[...], 0.0).astype(BF16)


def _proj_attn_kernel(x_ref, wt_ref, o_ref, wb_ref, *, n_wt, n_cols, n_scaled, scale):
    s = pl.program_id(0)

    @pl.when(s < n_wt)
    def _():
        _stage_weight_tile(wt_ref, wb_ref, s, 0, n_cols)

    @pl.when(s >= n_wt)
    def _():
        tile = wt_ref.shape[0]
        xb = x_ref[...].astype(BF16)
        for n in range(n_wt):
            acc = _dot_nt(xb, wb_ref[n * tile:(n + 1) * tile, :])
            if n < n_scaled:
                acc = acc * scale
            o_ref[:, n * tile:(n + 1) * tile] = acc.astype(o_ref.dtype)


def proj_attn(x, wt, *, n_cols, n_scaled_cols, scale, tm=512, tile=512):
    T, D = x.shape
    tm = min(tm, T)
    tile = _pick_tile(tile, n_cols, n_scaled_cols)
    n_wt = n_cols // tile
    tok = lambda s: (jnp.maximum(s - n_wt, 0), 0)
    return pl.pallas_call(
        functools.partial(_proj_attn_kernel, n_wt=n_wt, n_cols=wt.shape[0],
                          n_scaled=n_scaled_cols // tile, scale=scale),
        grid=(n_wt + T // tm,),
        in_specs=[
            pl.BlockSpec((tm, D), tok),
            pl.BlockSpec((tile, D), lambda s: (jnp.minimum(s, n_wt - 1), 0)),
        ],
        out_specs=pl.BlockSpec((tm, n_cols), tok),
        out_shape=jax.ShapeDtypeStruct((T, n_cols), BF16),
        scratch_shapes=[pltpu.VMEM((n_cols, D), BF16)],
        compiler_params=_cparams(("arbitrary",)),
    )(x, wt)


ATT_TQ = 256
ATT_TK = 512
N_BIAS_TERMS = 3
POS_RADIX = 64
BF16_ROWS = 16
VT_ROWS = LANES + BF16_ROWS


def alibi_tables(H, S):
    slope = jnp.exp2(-8.0 * (jnp.arange(H, dtype=F32) + 1.0) / H) * math.log2(math.e)
    terms, rest = [], slope
    for _ in range(N_BIAS_TERMS):
        t = rest.astype(BF16).astype(F32)
        terms += [POS_RADIX * t, t]
        rest = rest - t
    tab = jnp.stack(terms, axis=1)
    pos = jnp.arange(S, dtype=jnp.int32)
    hi, lo = (pos // POS_RADIX).astype(F32), (pos % POS_RADIX).astype(F32)
    feat = jnp.stack([hi, lo] * N_BIAS_TERMS, axis=1)
    feat = jnp.pad(feat, ((0, 0), (0, LANES - feat.shape[1]))).astype(BF16)
    return tab, feat


def _diff_attn_kernel(tab_ref, q_ref, k_ref, v_ref, feat_ref, lq1_ref, lk1_ref, lq2_ref, lk2_ref, ng_ref,
                      o_ref, ka_ref, vt_ref, qa_ref, acc_ref, m_ref, *, hb, nck, lambda_init):
    tq = ATT_TQ
    tk = vt_ref.shape[2]
    g = pl.program_id(1)
    i = pl.program_id(2)
    heads = range(hb)
    hs = lambda h: slice(h * LANES, (h + 1) * LANES)

    @pl.when(i == 0)
    def _():
        for h in heads:
            ka_ref[h, :, 0:LANES] = k_ref[:, hs(h)]
            ka_ref[h, :, LANES:2 * LANES] = feat_ref[...]
            for c in range(nck):
                vt_ref[h * nck + c, 0:LANES, :] = v_ref[c * tk:(c + 1) * tk, hs(h)].astype(F32).T.astype(BF16)
                vt_ref[h * nck + c, LANES:VT_ROWS, :] = jnp.ones((BF16_ROWS, tk), BF16)

    lane = lax.broadcasted_iota(jnp.int32, (tq, LANES), 1)
    for h in heads:
        q = q_ref[:, hs(h)]
        zero = jnp.zeros_like(q)
        qf = jnp.zeros((tq, LANES), F32)
        for n in range(2 * N_BIAS_TERMS):
            qf = jnp.where(lane == n, tab_ref[g * hb + h, n], qf)
        qf = qf.astype(BF16)
        qa_ref[h, 0:tq, 0:LANES] = jnp.where(lane < HEAD, q, zero)
        qa_ref[h, tq:2 * tq, 0:LANES] = jnp.where(lane >= HEAD, q, zero)
        qa_ref[h, 0:tq, LANES:2 * LANES] = qf
        qa_ref[h, tq:2 * tq, LANES:2 * LANES] = qf
        m_ref[h] = jnp.full((1, 2 * tq), NEG_BIG, F32)
        acc_ref[h] = jnp.zeros((VT_ROWS, 2 * tq), F32)

    def chunk(c, part=None):
        start = pl.multiple_of(c * tk, tk)
        nk, lanes = tk, slice(None)
        if part is not None:
            start, nk, lanes = start + part * tq, tq, slice(part * tq, (part + 1) * tq)
        ss = [_dot_nt(ka_ref[h, pl.ds(start, nk), :], qa_ref[h]) for h in heads]
        if part is not None:
            key = start + lax.broadcasted_iota(jnp.int32, (nk, 2 * tq), 0)
            qcol = lax.broadcasted_iota(jnp.int32, (nk, 2 * tq), 1)
            qpos = i * tq + jnp.where(qcol >= tq, qcol - tq, qcol)
            keep = key <= qpos
            ss = [jnp.where(keep, s, -jnp.inf) for s in ss]
        m_olds = [m_ref[h] for h in heads]
        m_news = [jnp.maximum(m_olds[h], jnp.max(ss[h], axis=0, keepdims=True)) for h in heads]
        ps = [jnp.exp2(ss[h] - m_news[h]).astype(BF16) for h in heads]
        alphas = [jnp.exp2(m_olds[h] - m_news[h]) for h in heads]
        pvs = [_dot(vt_ref[h * nck + c, :, lanes], ps[h]) for h in heads]
        for h in heads:
            m_ref[h] = m_news[h]
            acc_ref[h] = alphas[h] * acc_ref[h] + pvs[h]

    ratio = tk // tq
    n_full = i // ratio

    def body(c, carry):
        chunk(c)
        return carry

    lax.fori_loop(0, n_full, body, 0)
    chunk(n_full, 0)
    for part in range(1, ratio):
        pl.when(i % ratio >= part)(functools.partial(chunk, n_full, part))

    lam = (jnp.exp(jnp.sum(lq1_ref[...] * lk1_ref[...], axis=-1, keepdims=True))
           - jnp.exp(jnp.sum(lq2_ref[...] * lk2_ref[...], axis=-1, keepdims=True)) + lambda_init)
    for h in heads:
        acc = acc_ref[h]
        rl = 1.0 / acc[LANES:LANES + 1, :]
        ot = (acc[0:LANES, 0:tq] * rl[:, 0:tq]
              - lam * (acc[0:LANES, tq:2 * tq] * rl[:, tq:2 * tq]))
        o = ot.T
        o = o * lax.rsqrt(jnp.mean(o * o, axis=-1, keepdims=True) + ATTN_NORM_EPS) * ng_ref[...]
        o_ref[:, hs(h)] = (o * (1.0 - lambda_init)).astype(o_ref.dtype)


def diff_attn(pa, tab, feat, lq1, lk1, lq2, lk2, norm_g, *, B, S, H, lambda_init, hb=8):
    hb = min(hb, H)
    tq, tk = ATT_TQ, min(ATT_TK, S)
    assert S % tk == 0 and H % hb == 0 and tk % tq == 0
    nq, nck, ng = S // tq, S // tk, H // hb
    wb = hb * LANES
    small = lambda n: pl.BlockSpec((1, n), lambda b, g, i: (0, 0))
    return pl.pallas_call(
        functools.partial(_diff_attn_kernel, hb=hb, nck=nck, lambda_init=lambda_init),
        grid=(B, ng, nq),
        in_specs=[
            pl.BlockSpec(memory_space=pltpu.SMEM),
            pl.BlockSpec((tq, wb), lambda b, g, i: (b * nq + i, g)),
            pl.BlockSpec((S, wb), lambda b, g, i: (b, ng + g)),
            pl.BlockSpec((S, wb), lambda b, g, i: (b, 2 * ng + g)),
            pl.BlockSpec((S, LANES), lambda b, g, i: (0, 0)),
            small(HEAD), small(HEAD), small(HEAD), small(HEAD), small(2 * HEAD),
        ],
        out_specs=pl.BlockSpec((tq, wb), lambda b, g, i: (b * nq + i, g)),
        out_shape=jax.ShapeDtypeStruct((B * S, H * LANES), BF16),
        scratch_shapes=[
            pltpu.VMEM((hb, S, 2 * LANES), BF16),
            pltpu.VMEM((hb * nck, VT_ROWS, tk), BF16),
            pltpu.VMEM((hb, 2 * tq, 2 * LANES), BF16),
            pltpu.VMEM((hb, VT_ROWS, 2 * tq), F32),
            pltpu.VMEM((hb, 1, 2 * tq), F32),
        ],
        compiler_params=_cparams(("parallel", "parallel", "arbitrary")),
    )(tab, pa, pa, pa, feat, lq1, lk1, lq2, lk2, norm_g)


def _same_head(n):
    r = lax.broadcasted_iota(jnp.int32, (n, n), 0) // HEAD
    c = lax.broadcasted_iota(jnp.int32, (n, n), 1) // HEAD
    return r == c


def _head_ones(n):
    return jnp.where(_same_head(n), 1.0, 0.0).astype(BF16)


def _head_sum(x, ones):
    return _dot(x.astype(BF16), ones)


RWKV_TAIL = 3 * LANES


def _proj_rwkv_kernel(x_ref, wt_ref, mu_ref, w0_ref, a0_ref, kk_ref, ka_ref, w12_ref, g2_ref,
                      r_ref, lw_ref, k2_ref, v_ref, kkn_ref, b_ref, g_ref, wb_ref, pr_ref, prev_ref,
                      *, n_wt, first_tile, n_cols, width, tiles_per_seq):
    s = pl.program_id(0)

    @pl.when(s == 0)
    def _():
        prev_ref[...] = jnp.zeros_like(prev_ref)

    @pl.when(s < n_wt)
    def _():
        _stage_weight_tile(wt_ref, wb_ref, s, first_tile, n_cols)

    @pl.when(s >= n_wt)
    def _():
        tm = x_ref.shape[0]
        xb = x_ref[...].astype(BF16)
        c3 = 3 * width
        first = (s - n_wt) % tiles_per_seq == 0
        row = lax.broadcasted_iota(jnp.int32, (tm, 1), 0)

        def project(c0, c1):
            pr_ref[:, c0:c1] = _dot_nt(xb, wb_ref[c0:c1, :])

        def mixed(c0, c1):
            x = pr_ref[:, c0:c1]
            prow = jnp.where(first, 0.0, prev_ref[0:1, c0:c1])
            xs = jnp.where(row == 0, prow, pltpu.roll(x, 1, axis=0))
            return x + mu_ref[:, c0:c1] * (xs - x)

        project(c3, c3 + RWKV_TAIL)
        project(width, 2 * width)

        lora_in = mixed(c3, c3 + LANES)
        lane = lax.broadcasted_iota(jnp.int32, lora_in.shape, 1)
        lora_in = jnp.where(lane < HEAD, jnp.tanh(lora_in), lora_in)
        z = _dot(lora_in.astype(BF16), w12_ref[...])
        g_ref[...] = _dot(jax.nn.sigmoid(mixed(c3 + LANES, c3 + RWKV_TAIL)).astype(BF16), g2_ref[...])
        lw_ref[...] = -math.exp(-0.5) * jax.nn.sigmoid(w0_ref[...] + z[:, 0:width])
        a = jax.nn.sigmoid(a0_ref[...] + z[:, width:2 * width])

        project(0, width)

        k = mixed(width, 2 * width)
        k2_ref[...] = k * (1.0 + (a - 1.0) * ka_ref[...])
        kx = k * kk_ref[...]
        gw = min(2 * LANES, width)
        ones = _head_ones(gw)
        ss = jnp.concatenate(
            [_head_sum(kx[:, t * gw:(t + 1) * gw] * kx[:, t * gw:(t + 1) * gw], ones)
             for t in range(width // gw)], axis=1)
        kkn = kx / jnp.maximum(jnp.sqrt(ss), 1e-12)
        kkn_ref[...] = kkn
        b_ref[...] = kkn * a

        project(2 * width, c3)

        r_ref[...] = mixed(0, width)
        v_ref[...] = mixed(2 * width, c3)
        prev_ref[0:1, :] = pr_ref[tm - 1:tm, :]


def proj_rwkv(x, wt, mu, w0, a0, k_k, k_a, w12, g2p, *, S, col0, width, tm=256, tile=384):
    T, D = x.shape
    tm = min(tm, S)
    W = 3 * width + RWKV_TAIL
    tile = _pick_tile(tile, col0, W)
    assert S % tm == 0
    n_wt, first_tile = W // tile, col0 // tile
    tok = lambda s: (jnp.maximum(s - n_wt, 0), 0)
    row = lambda n: pl.BlockSpec((1, n), lambda s: (0, 0))
    out = pl.BlockSpec((tm, width), tok)
    return pl.pallas_call(
        functools.partial(_proj_rwkv_kernel, n_wt=n_wt, first_tile=first_tile, n_cols=wt.shape[0],
                          width=width, tiles_per_seq=S // tm),
        grid=(n_wt + T // tm,),
        in_specs=[
            pl.BlockSpec((tm, D), tok),
            pl.BlockSpec((tile, D), lambda s: (first_tile + jnp.minimum(s, n_wt - 1), 0)),
            row(W), row(width), row(width), row(width), row(width),
            pl.BlockSpec((LANES, 2 * width), lambda s: (0, 0)),
            pl.BlockSpec((2 * LANES, width), lambda s: (0, 0)),
        ],
        out_specs=[out] * 7,
        out_shape=[jax.ShapeDtypeStruct((T, width), F32)] * 7,
        scratch_shapes=[pltpu.VMEM((W, D), BF16), pltpu.VMEM((tm, W), F32), pltpu.VMEM((SUBLANES, W), F32)],
        compiler_params=_cparams(("arbitrary",)),
    )(x, wt, mu, w0, a0, k_k, k_a, w12, g2p)


GROUP = 2


def _block_diag(x):
    head = lax.broadcasted_iota(jnp.int32, x.shape, 1) // HEAD
    zero = jnp.zeros_like(x)
    return jnp.concatenate([jnp.where(head == h, x, zero) for h in range(x.shape[1] // HEAD)], axis=0)


def _chunk_step(rs, lws, k2s, vs, kks, bs, sts):
    C = CHUNK
    GL = rs[0].shape[1]
    idx = range(len(rs))
    row = lax.broadcasted_iota(jnp.int32, (C, GL), 0)
    tcol = lax.broadcasted_iota(jnp.int32, (C, GL), 1) % HEAD
    strict = row > tcol
    incl = row >= tcol
    eye = jnp.where(row == tcol, 1.0, 0.0)
    bdb = lambda x: _block_diag(x).astype(BF16)

    cums, tots, Mbs, Mks, ARSs = [], [], [], [], []
    for i in idx:
        cm = lws[i]
        sh = 1
        while sh < C:
            cm = cm + jnp.where(row >= sh, pltpu.roll(cm, sh, axis=0), 0.0)
            sh *= 2
        e_neg = jnp.exp(-cm)
        AR = jnp.concatenate([-kks[i] * jnp.exp(cm - lws[i]), rs[i] * jnp.exp(cm)], axis=0).astype(BF16)
        cums.append(cm)
        tots.append(cm[C - 1:C, :])
        Mbk = _dot_nt(AR, jnp.concatenate([bdb(bs[i] * e_neg), bdb(k2s[i] * e_neg)], axis=0))
        Mbs.append(Mbk[:, 0:GL])
        Mks.append(Mbk[:, GL:2 * GL])
        ARSs.append(_dot_nt(AR, sts[i].astype(BF16)))
    vbds = [bdb(v) for v in vs]

    Ls = [jnp.where(strict, Mbs[i][0:C], 0.0) for i in idx]
    Ps = [eye + L for L in Ls]
    Lps = [_dot(L.astype(BF16), bdb(L)) for L in Ls]
    Xs = [ARSs[i][0:C] + _dot(jnp.where(strict, Mks[i][0:C], 0.0).astype(BF16), vbds[i]) for i in idx]
    m = 2
    while 2 * m < C:
        both = [_dot(jnp.concatenate([Lps[i], Ps[i]], axis=0).astype(BF16), bdb(Lps[i])) for i in idx]
        Ps = [Ps[i] + both[i][C:2 * C] for i in idx]
        Lps = [bt[0:C] for bt in both]
        m *= 2
    Ps = [Ps[i] + _dot(Ps[i].astype(BF16), bdb(Lps[i])) for i in idx]

    Us = [_dot(Ps[i].astype(BF16), bdb(Xs[i])) for i in idx]
    Ys = [ARSs[i][C:2 * C]
          + _dot(jnp.concatenate([jnp.where(incl, Mbs[i][C:2 * C], 0.0),
                                  jnp.where(incl, Mks[i][C:2 * C], 0.0)], axis=1).astype(BF16),
                 jnp.concatenate([bdb(Us[i]), vbds[i]], axis=0)) for i in idx]
    same_head = _same_head(GL)
    st_news = []
    for i in idx:
        e_rem = jnp.exp(tots[i] - cums[i])
        UV = jnp.concatenate([Us[i], vs[i]], axis=0).astype(BF16)
        BK = jnp.concatenate([bs[i] * e_rem, k2s[i] * e_rem], axis=0).astype(BF16)
        st_news.append(jnp.where(same_head, sts[i] * jnp.exp(tots[i]) + _dot_tn(UV, BK), 0.0))
    return Ys, st_news


def _rwkv_chunk_kernel(r_ref, lw_ref, k2_ref, v_ref, kk_ref, b_ref, g_ref, rk_ref, gng_ref, gnb_ref,
                       *rest, nbatch, ngroups, n_ride):
    ride_in, o_ref, ride_out, st_ref = rest[:n_ride], rest[n_ride], rest[n_ride + 1:-1], rest[-1]
    for w_ref, wb_ref in zip(ride_in, ride_out):
        wb_ref[...] = w_ref[...].astype(BF16)
    c = pl.program_id(2)

    @pl.when(c == 0)
    def _():
        st_ref[...] = jnp.zeros_like(st_ref)

    GL = st_ref.shape[1]
    ones = _head_ones(GL)
    chains = [(bi, p) for bi in range(nbatch) for p in range(ngroups)]
    sl = lambda p: slice(p * GL, (p + 1) * GL)
    get = lambda ref: [ref[bi, :, sl(p)] for bi, p in chains]
    rs, k2s, vs = get(r_ref), get(k2_ref), get(v_ref)
    ys, st_news = _chunk_step(rs, get(lw_ref), k2s, vs, get(kk_ref), get(b_ref),
                              [st_ref[i] for i in range(len(chains))])
    for i in range(len(chains)):
        st_ref[i] = st_news[i]
    ds = [y - _head_sum(y, ones) * (1.0 / HEAD) for y in ys]
    vars_ = [_head_sum(d * d, ones) * (1.0 / HEAD) for d in ds]
    for i, (bi, p) in enumerate(chains):
        yn = ds[i] * lax.rsqrt(vars_[i] + GN_EPS) * gng_ref[:, sl(p)] + gnb_ref[:, sl(p)]
        bonus = _head_sum(rs[i] * k2s[i] * rk_ref[:, sl(p)], ones) * vs[i]
        o_ref[bi, :, sl(p)] = ((yn + bonus) * g_ref[bi, :, sl(p)]).astype(o_ref.dtype)


def rwkv_chunk(r, lw, k2, v, kk, b, g, r_k, gn_g, gn_b, *, ride_along=(), nbatch=4, ngroups=8):
    B, S, W = r.shape
    GL = min(GROUP * HEAD, W)
    ngroups = min(ngroups, W // GL)
    nbatch = min(nbatch, B)
    wb = ngroups * GL
    grid = (B // nbatch, W // wb, S // CHUNK)
    n_steps = grid[0] * grid[1] * grid[2]
    seq = pl.BlockSpec((nbatch, CHUNK, wb), lambda bi, p, c: (bi, c, p))
    par = pl.BlockSpec((1, wb), lambda bi, p, c: (0, p))
    rides = [w for w in ride_along if w.shape[0] % (n_steps * BF16_ROWS) == 0]
    slab = lambda w: pl.BlockSpec((w.shape[0] // n_steps, w.shape[1]),
                                  lambda bi, p, c: ((bi * grid[1] + p) * grid[2] + c, 0))
    outs = pl.pallas_call(
        functools.partial(_rwkv_chunk_kernel, nbatch=nbatch, ngroups=ngroups, n_ride=len(rides)),
        grid=grid,
        in_specs=[seq] * 7 + [par] * 3 + [slab(w) for w in rides],
        out_specs=[seq] + [slab(w) for w in rides],
        out_shape=[jax.ShapeDtypeStruct((B, S, W), BF16)] + [jax.ShapeDtypeStruct(w.shape, BF16) for w in rides],
        scratch_shapes=[pltpu.VMEM((nbatch * ngroups, GL, GL), F32)],
        compiler_params=_cparams(("parallel", "parallel", "arbitrary")),
    )(r, lw, k2, v, kk, b, g, r_k, gn_g, gn_b, *rides)
    cast = iter(outs[1:])
    return outs[0], tuple(next(cast) if any(w is q for q in rides) else w.astype(BF16) for w in ride_along)


def _out_ln_kernel(x_ref, oa_ref, or_ref, wa_ref, wr_ref, g_ref, b_ref, o_ref, *, alpha):
    half = x_ref.shape[0] // 2
    rows = [slice(0, half), slice(half, 2 * half)]
    mixes = [_dot(oa_ref[r, :], wa_ref[...]) + _dot(or_ref[r, :], wr_ref[...]) for r in rows]
    for r, mix in zip(rows, mixes):
        o_ref[r, :] = _layer_norm(alpha * x_ref[r, :] + mix, g_ref[...], b_ref[...])


def out_ln(x, oa, orw, w, g, b, *, alpha, tm=512):
    T, D = x.shape
    tm = min(tm, T)
    ka, kr = oa.shape[1], orw.shape[1]
    assert ka % kr == 0 and w.shape[0] == ka + kr
    return pl.pallas_call(
        functools.partial(_out_ln_kernel, alpha=alpha),
        grid=(T // tm,),
        in_specs=[
            pl.BlockSpec((tm, D), lambda i: (i, 0)),
            pl.BlockSpec((tm, ka), lambda i: (i, 0)),
            pl.BlockSpec((tm, kr), lambda i: (i, 0)),
            pl.BlockSpec((ka, D), lambda i: (0, 0)),
            pl.BlockSpec((kr, D), lambda i: (ka // kr, 0)),
            pl.BlockSpec((1, D), lambda i: (0, 0)),
            pl.BlockSpec((1, D), lambda i: (0, 0)),
        ],
        out_specs=pl.BlockSpec((tm, D), lambda i: (i, 0)),
        out_shape=jax.ShapeDtypeStruct((T, D), F32),
        compiler_params=_cparams(("parallel",)),
    )(x, oa, orw, w, w, g, b)


def _layer(x, l, ffn1_w_gate, ffn1_w_up, ffn1_w_down, ln1_g, ln1_b, w_in,
           lambda_q1, lambda_k1, lambda_q2, lambda_k2, attn_norm_g,
           rwkv_mu, rwkv_w0, rwkv_w2, rwkv_a0, rwkv_a2, rwkv_g2,
           rwkv_k_k, rwkv_k_a, rwkv_r_k, rwkv_gn_g, rwkv_gn_b,
           w_out, ln2_g, ln2_b, ffn2_w_gate, ffn2_w_up, ffn2_w_down, ln3_g, ln3_b):
    B, S, D = x.shape
    T = B * S
    alpha = (2.0 * DEPTH) ** 0.25
    lambda_init = 0.8 - 0.6 * math.exp(-0.3 * l)
    rw = rwkv_w0.shape[-1]
    aw = w_out.shape[1] - rw
    H = aw // LANES
    n_wd, n_ad, n_gd = rwkv_w2.shape[1], rwkv_a2.shape[1], rwkv_g2.shape[1]
    assert n_wd == HEAD and n_ad == HEAD and n_gd <= 2 * LANES
    row = lambda a: a.reshape(1, -1)
    bf = lambda a: a.astype(BF16)

    x1 = ffn_ln(x.reshape(T, D), ffn1_w_gate[l], ffn1_w_up[l], ffn1_w_down[l],
                row(ln1_g[l]), row(ln1_b[l]), alpha=alpha)

    wi = jnp.swapaxes(w_in[l], 0, 1)
    p_attn = proj_attn(x1, wi, n_cols=3 * aw, n_scaled_cols=aw, scale=HEAD ** -0.5 * math.log2(math.e))
    tab, feat = alibi_tables(H, S)
    o_attn = diff_attn(p_attn, tab, feat, row(lambda_q1[l]), row(lambda_k1[l]), row(lambda_q2[l]),
                       row(lambda_k2[l]), row(attn_norm_g[l]), B=B, S=S, H=H, lambda_init=lambda_init)

    mu = jnp.pad(rwkv_mu[l], (0, RWKV_TAIL - (n_wd + n_ad + n_gd)))
    w12 = jnp.zeros((LANES, 2 * rw), F32)
    w12 = w12.at[:HEAD, :rw].set(rwkv_w2[l]).at[HEAD:, rw:].set(rwkv_a2[l])
    g2p = jnp.pad(rwkv_g2[l], ((0, 2 * LANES - n_gd), (0, 0)))
    seqs = proj_rwkv(x1, wi, row(mu), row(rwkv_w0[l]), row(rwkv_a0[l]), row(rwkv_k_k[l]), row(rwkv_k_a[l]),
                     bf(w12), bf(g2p), S=S, col0=3 * aw, width=rw)
    r, lw, k2, v, kk, b, g = [t.reshape(B, S, rw) for t in seqs]
    o_rwkv, later_w = rwkv_chunk(r, lw, k2, v, kk, b, g, row(rwkv_r_k[l]), row(rwkv_gn_g[l]), row(rwkv_gn_b[l]),
                                 ride_along=(w_out[l], ffn2_w_gate[l], ffn2_w_up[l], ffn2_w_down[l]))

    x2 = out_ln(x1, o_attn, o_rwkv.reshape(T, rw), later_w[0], row(ln2_g[l]), row(ln2_b[l]), alpha=alpha)
    x3 = ffn_ln(x2, *later_w[1:], row(ln3_g[l]), row(ln3_b[l]), alpha=alpha, tf=2 * FFN_TF)
    return x3.reshape(B, S, D)


def kernel(x, ffn1_w_gate, ffn1_w_up, ffn1_w_down, ln1_g, ln1_b, w_in, lambda_q1, lambda_k1, lambda_q2, lambda_k2, attn_norm_g, rwkv_mu, rwkv_w0, rwkv_w2, rwkv_a0, rwkv_a2, rwkv_g2, rwkv_k_k, rwkv_k_a, rwkv_r_k, rwkv_gn_g, rwkv_gn_b, w_out, ln2_g, ln2_b, ffn2_w_gate, ffn2_w_up, ffn2_w_down, ln3_g, ln3_b):
    for l in range(DEPTH):
        x = _layer(x, l, ffn1_w_gate, ffn1_w_up, ffn1_w_down, ln1_g, ln1_b, w_in,
                   lambda_q1, lambda_k1, lambda_q2, lambda_k2, attn_norm_g,
                   rwkv_mu, rwkv_w0, rwkv_w2, rwkv_a0, rwkv_a2, rwkv_g2,
                   rwkv_k_k, rwkv_k_a, rwkv_r_k, rwkv_gn_g, rwkv_gn_b,
                   w_out, ln2_g, ln2_b, ffn2_w_gate, ffn2_w_up, ffn2_w_down, ln3_g, ln3_b)
    return x
```

```python
import functools
import math

import jax
import jax.numpy as jnp
from jax import lax
from jax.experimental import pallas as pl
from jax.experimental.pallas import tpu as pltpu

F32 = jnp.float32
BF16 = jnp.bfloat16

DEPTH = 1
LN_EPS = 1e-5
ATTN_NORM_EPS = 1e-5
GN_EPS = 64e-5
HEAD = 64
LANES = 128
SUBLANES = 8
CHUNK = 64
VMEM_LIMIT = 56 * 1024 * 1024
NEG_BIG = -1e30


def _cparams(sem):
    return pltpu.CompilerParams(dimension_semantics=sem, vmem_limit_bytes=VMEM_LIMIT)


def _layer_norm(y, g, b):
    mu = jnp.mean(y, axis=-1, keepdims=True)
    d = y - mu
    var = jnp.mean(d * d, axis=-1, keepdims=True)
    return d * lax.rsqrt(var + LN_EPS) * g + b


def _dot(a, b):
    return jnp.dot(a, b, preferred_element_type=F32)


def _dot_nt(a, b):
    return lax.dot_general(a, b, (((1,), (1,)), ((), ())), preferred_element_type=F32)


def _dot_tn(a, b):
    return lax.dot_general(a, b, (((0,), (0,)), ((), ())), preferred_element_type=F32)


def _ffn_ln_kernel(x_ref, wg_ref, wu_ref, wd_ref, g_ref, b_ref, o_ref, *rest, alpha, ncol, emit_w, n_skip):
    w_outs, xb_ref = rest[:-1], rest[-1]
    i = pl.program_id(0)
    j = pl.program_id(1)
    nj = pl.num_programs(1)
    tm, D = o_ref.shape
    cols = [slice(n * ncol, (n + 1) * ncol) for n in range(D // ncol)]

    def body(first, final):
        nrow = 2 if (first or final) else 1
        rows = [slice(r * tm // nrow, (r + 1) * tm // nrow) for r in range(nrow)]
        if first:
            for r in rows:
                x = x_ref[r, :]
                xb_ref[r, :] = x.astype(BF16)
                o_ref[r, :] = alpha * x
        wg = wg_ref[...].astype(BF16)
        wu = wu_ref[...].astype(BF16)
        wd = [wd_ref[:, c].astype(BF16) for c in cols]
        if emit_w:
            w_outs[0][...] = wg
            w_outs[1][...] = wu
            for c, w in zip(cols, wd):
                w_outs[2][:, c] = w
        hs = []
        for r in rows:
            xb = xb_ref[r, :]
            hg = _dot(xb, wg)
            hu = _dot(xb, wu)
            hs.append((0.5 * hg * jax.nn.sigmoid(hg) * hu).astype(BF16))
        for r, h in zip(rows, hs):
            for c, w in zip(cols, wd):
                o_ref[r, c] += _dot(h, w)
        if final:
            for r in rows:
                o_ref[r, :] = _layer_norm(o_ref[r, :], g_ref[...], b_ref[...])

    live = i >= n_skip
    pl.when(jnp.logical_and(live, j == 0))(lambda: body(True, False))
    pl.when(jnp.logical_and(live, jnp.logical_and(j > 0, j < nj - 1)))(lambda: body(False, False))
    pl.when(jnp.logical_and(live, jnp.logical_and(j > 0, j == nj - 1)))(lambda: body(False, True))
    if n_skip:
        @pl.when(jnp.logical_and(i < n_skip, j == nj - 1))
        def _():
            o_ref[...] = jnp.zeros_like(o_ref)


FFN_TM = 1024
FFN_TF = 256


def ffn_ln(x, wg, wu, wd, g, b, *, alpha, tf=FFN_TF, ncol=512, n_tiles=None, emit_bf16_weights=False, n_skip=0):
    T, D = x.shape
    Fd = wg.shape[1]
    tm = min(FFN_TM, T)
    tf = min(tf, Fd // 2)
    ncol = min(ncol, D)
    n_tiles = n_tiles or T // tm
    assert not emit_bf16_weights or n_tiles == 1
    wcol = lambda i, j: (0, jnp.where(i < n_skip, 0, j))
    wrow = lambda i, j: (jnp.where(i < n_skip, 0, j), 0)
    w_specs = [pl.BlockSpec((D, tf), wcol), pl.BlockSpec((D, tf), wcol), pl.BlockSpec((tf, D), wrow)]
    outs = pl.pallas_call(
        functools.partial(_ffn_ln_kernel, alpha=alpha, ncol=ncol, emit_w=emit_bf16_weights, n_skip=n_skip),
        grid=(n_tiles, Fd // tf),
        in_specs=[pl.BlockSpec((tm, D), lambda i, j: (i, 0))] + w_specs
                 + [pl.BlockSpec((1, D), lambda i, j: (0, 0))] * 2,
        out_specs=[pl.BlockSpec((tm, D), lambda i, j: (i, 0))] + (w_specs if emit_bf16_weights else []),
        out_shape=[jax.ShapeDtypeStruct((n_tiles * tm, D), F32)]
                  + ([jax.ShapeDtypeStruct(w.shape, BF16) for w in (wg, wu, wd)] if emit_bf16_weights else []),
        scratch_shapes=[pltpu.VMEM((tm, D), BF16)],
        compiler_params=_cparams(("parallel", "arbitrary")),
    )(x, wg, wu, wd, g, b)
    return outs if emit_bf16_weights else outs[0]


def _pick_tile(limit, *sizes):
    t = limit // LANES * LANES
    while any(s % t for s in sizes):
        t -= LANES
    return t


def _stage_weight_tile(wt_ref, wb_ref, s, first_tile, n_cols):
    tile = wt_ref.shape[0]
    col = (first_tile + s) * tile + lax.broadcasted_iota(jnp.int32, wt_ref.shape, 0)
    wb_ref[pl.ds(pl.multiple_of(s * tile, tile), tile), :] = jnp.where(col < n_cols, wt_ref[...], 0.0).astype(BF16)


def _proj_attn_kernel(x_ref, wt_ref, o_ref, wb_ref, *, n_wt, n_cols, n_scaled, scale):
    s = pl.program_id(0)

    @pl.when(s < n_wt)
    def _():
        _stage_weight_tile(wt_ref, wb_ref, s, 0, n_cols)

    @pl.when(s >= n_wt)
    def _():
        tile = wt_ref.shape[0]
        xb = x_ref[...].astype(BF16)
        for n in range(n_wt):
            acc = _dot_nt(xb, wb_ref[n * tile:(n + 1) * tile, :])
            if n < n_scaled:
                acc = acc * scale
            o_ref[:, n * tile:(n + 1) * tile] = acc.astype(o_ref.dtype)


def proj_attn(x, wt, *, n_cols, n_scaled_cols, scale, tm=512, tile=512):
    T, D = x.shape
    tm = min(tm, T)
    tile = _pick_tile(tile, n_cols, n_scaled_cols)
    n_wt = n_cols // tile
    tok = lambda s: (jnp.maximum(s - n_wt, 0), 0)
    return pl.pallas_call(
        functools.partial(_proj_attn_kernel, n_wt=n_wt, n_cols=wt.shape[0],
                          n_scaled=n_scaled_cols // tile, scale=scale),
        grid=(n_wt + T // tm,),
        in_specs=[
            pl.BlockSpec((tm, D), tok),
            pl.BlockSpec((tile, D), lambda s: (jnp.minimum(s, n_wt - 1), 0)),
        ],
        out_specs=pl.BlockSpec((tm, n_cols), tok),
        out_shape=jax.ShapeDtypeStruct((T, n_cols), BF16),
        scratch_shapes=[pltpu.VMEM((n_cols, D), BF16)],
        compiler_params=_cparams(("arbitrary",)),
    )(x, wt)


ATT_TQ = 256
ATT_TK = 512
N_BIAS_TERMS = 3
POS_RADIX = 64
BF16_ROWS = 16
VT_ROWS = LANES + BF16_ROWS


def alibi_tables(H, S):
    slope = jnp.exp2(-8.0 * (jnp.arange(H, dtype=F32) + 1.0) / H) * math.log2(math.e)
    terms, rest = [], slope
    for _ in range(N_BIAS_TERMS):
        t = rest.astype(BF16).astype(F32)
        terms += [POS_RADIX * t, t]
        rest = rest - t
    tab = jnp.stack(terms, axis=1)
    pos = jnp.arange(S, dtype=jnp.int32)
    hi, lo = (pos // POS_RADIX).astype(F32), (pos % POS_RADIX).astype(F32)
    feat = jnp.stack([hi, lo] * N_BIAS_TERMS, axis=1)
    feat = jnp.pad(feat, ((0, 0), (0, LANES - feat.shape[1]))).astype(BF16)
    return tab, feat


def _diff_attn_kernel(tab_ref, q_ref, k_ref, v_ref, feat_ref, lq1_ref, lk1_ref, lq2_ref, lk2_ref, ng_ref,
                      o_ref, ka_ref, vt_ref, qa_ref, acc_ref, m_ref, *, hb, nck, lambda_init):
    tq = ATT_TQ
    tk = vt_ref.shape[2]
    g = pl.program_id(1)
    i = pl.program_id(2)
    heads = range(hb)
    hs = lambda h: slice(h * LANES, (h + 1) * LANES)

    @pl.when(i == 0)
    def _():
        for h in heads:
            ka_ref[h, :, 0:LANES] = k_ref[:, hs(h)]
            ka_ref[h, :, LANES:2 * LANES] = feat_ref[...]
            for c in range(nck):
                vt_ref[h * nck + c, 0:LANES, :] = v_ref[c * tk:(c + 1) * tk, hs(h)].astype(F32).T.astype(BF16)
                vt_ref[h * nck + c, LANES:VT_ROWS, :] = jnp.ones((BF16_ROWS, tk), BF16)

    lane = lax.broadcasted_iota(jnp.int32, (tq, LANES), 1)
    for h in heads:
        q = q_ref[:, hs(h)]
        zero = jnp.zeros_like(q)
        qf = jnp.zeros((tq, LANES), F32)
        for n in range(2 * N_BIAS_TERMS):
            qf = jnp.where(lane == n, tab_ref[g * hb + h, n], qf)
        qf = qf.astype(BF16)
        qa_ref[h, 0:tq, 0:LANES] = jnp.where(lane < HEAD, q, zero)
        qa_ref[h, tq:2 * tq, 0:LANES] = jnp.where(lane >= HEAD, q, zero)
        qa_ref[h, 0:tq, LANES:2 * LANES] = qf
        qa_ref[h, tq:2 * tq, LANES:2 * LANES] = qf
        m_ref[h] = jnp.full((1, 2 * tq), NEG_BIG, F32)
        acc_ref[h] = jnp.zeros((VT_ROWS, 2 * tq), F32)

    def chunk(c, part=None):
        start = pl.multiple_of(c * tk, tk)
        nk, lanes = tk, slice(None)
        if part is not None:
            start, nk, lanes = start + part * tq, tq, slice(part * tq, (part + 1) * tq)
        ss = [_dot_nt(ka_ref[h, pl.ds(start, nk), :], qa_ref[h]) for h in heads]
        if part is not None:
            key = start + lax.broadcasted_iota(jnp.int32, (nk, 2 * tq), 0)
            qcol = lax.broadcasted_iota(jnp.int32, (nk, 2 * tq), 1)
            qpos = i * tq + jnp.where(qcol >= tq, qcol - tq, qcol)
            keep = key <= qpos
            ss = [jnp.where(keep, s, -jnp.inf) for s in ss]
        m_olds = [m_ref[h] for h in heads]
        m_news = [jnp.maximum(m_olds[h], jnp.max(ss[h], axis=0, keepdims=True)) for h in heads]
        ps = [jnp.exp2(ss[h] - m_news[h]).astype(BF16) for h in heads]
        alphas = [jnp.exp2(m_olds[h] - m_news[h]) for h in heads]
        pvs = [_dot(vt_ref[h * nck + c, :, lanes], ps[h]) for h in heads]
        for h in heads:
            m_ref[h] = m_news[h]
            acc_ref[h] = alphas[h] * acc_ref[h] + pvs[h]

    ratio = tk // tq
    n_full = i // ratio

    def body(c, carry):
        chunk(c)
        return carry

    lax.fori_loop(0, n_full, body, 0)
    chunk(n_full, 0)
    for part in range(1, ratio):
        pl.when(i % ratio >= part)(functools.partial(chunk, n_full, part))

    lam = (jnp.exp(jnp.sum(lq1_ref[...] * lk1_ref[...], axis=-1, keepdims=True))
           - jnp.exp(jnp.sum(lq2_ref[...] * lk2_ref[...], axis=-1, keepdims=True)) + lambda_init)
    for h in heads:
        acc = acc_ref[h]
        rl = 1.0 / acc[LANES:LANES + 1, :]
        ot = (acc[0:LANES, 0:tq] * rl[:, 0:tq]
              - lam * (acc[0:LANES, tq:2 * tq] * rl[:, tq:2 * tq]))
        o = ot.T
        o = o * lax.rsqrt(jnp.mean(o * o, axis=-1, keepdims=True) + ATTN_NORM_EPS) * ng_ref[...]
        o_ref[:, hs(h)] = (o * (1.0 - lambda_init)).astype(o_ref.dtype)


def diff_attn(pa, tab, feat, lq1, lk1, lq2, lk2, norm_g, *, B, S, H, lambda_init, hb=8):
    hb = min(hb, H)
    tq, tk = ATT_TQ, min(ATT_TK, S)
    assert S % tk == 0 and H % hb == 0 and tk % tq == 0
    nq, nck, ng = S // tq, S // tk, H // hb
    wb = hb * LANES
    small = lambda n: pl.BlockSpec((1, n), lambda b, g, i: (0, 0))
    return pl.pallas_call(
        functools.partial(_diff_attn_kernel, hb=hb, nck=nck, lambda_init=lambda_init),
        grid=(B, ng, nq),
        in_specs=[
            pl.BlockSpec(memory_space=pltpu.SMEM),
            pl.BlockSpec((tq, wb), lambda b, g, i: (b * nq + i, g)),
            pl.BlockSpec((S, wb), lambda b, g, i: (b, ng + g)),
            pl.BlockSpec((S, wb), lambda b, g, i: (b, 2 * ng + g)),
            pl.BlockSpec((S, LANES), lambda b, g, i: (0, 0)),
            small(HEAD), small(HEAD), small(HEAD), small(HEAD), small(2 * HEAD),
        ],
        out_specs=pl.BlockSpec((tq, wb), lambda b, g, i: (b * nq + i, g)),
        out_shape=jax.ShapeDtypeStruct((B * S, H * LANES), BF16),
        scratch_shapes=[
            pltpu.VMEM((hb, S, 2 * LANES), BF16),
            pltpu.VMEM((hb * nck, VT_ROWS, tk), BF16),
            pltpu.VMEM((hb, 2 * tq, 2 * LANES), BF16),
            pltpu.VMEM((hb, VT_ROWS, 2 * tq), F32),
            pltpu.VMEM((hb, 1, 2 * tq), F32),
        ],
        compiler_params=_cparams(("parallel", "parallel", "arbitrary")),
    )(tab, pa, pa, pa, feat, lq1, lk1, lq2, lk2, norm_g)


def _same_head(n):
    r = lax.broadcasted_iota(jnp.int32, (n, n), 0) // HEAD
    c = lax.broadcasted_iota(jnp.int32, (n, n), 1) // HEAD
    return r == c


def _head_ones(n):
    return jnp.where(_same_head(n), 1.0, 0.0).astype(BF16)


def _head_sum(x, ones):
    return _dot(x.astype(BF16), ones)


RWKV_TAIL = 3 * LANES


def _proj_rwkv_kernel(x_ref, wt_ref, mu_ref, w0_ref, a0_ref, kk_ref, ka_ref, w12_ref, g2_ref,
                      r_ref, lw_ref, k2_ref, v_ref, kkn_ref, b_ref, g_ref, wb_ref, pr_ref, prev_ref,
                      *, n_wt, first_tile, n_cols, width, tiles_per_seq):
    s = pl.program_id(0)

    @pl.when(s == 0)
    def _():
        prev_ref[...] = jnp.zeros_like(prev_ref)

    @pl.when(s < n_wt)
    def _():
        _stage_weight_tile(wt_ref, wb_ref, s, first_tile, n_cols)

    @pl.when(s >= n_wt)
    def _():
        tm = x_ref.shape[0]
        xb = x_ref[...].astype(BF16)
        c3 = 3 * width
        first = (s - n_wt) % tiles_per_seq == 0
        row = lax.broadcasted_iota(jnp.int32, (tm, 1), 0)

        def project(c0, c1):
            pr_ref[:, c0:c1] = _dot_nt(xb, wb_ref[c0:c1, :])

        def mixed(c0, c1):
            x = pr_ref[:, c0:c1]
            prow = jnp.where(first, 0.0, prev_ref[0:1, c0:c1])
            xs = jnp.where(row == 0, prow, pltpu.roll(x, 1, axis=0))
            return x + mu_ref[:, c0:c1] * (xs - x)

        project(c3, c3 + RWKV_TAIL)
        project(width, 2 * width)

        lora_in = mixed(c3, c3 + LANES)
        lane = lax.broadcasted_iota(jnp.int32, lora_in.shape, 1)
        lora_in = jnp.where(lane < HEAD, jnp.tanh(lora_in), lora_in)
        z = _dot(lora_in.astype(BF16), w12_ref[...])
        g_ref[...] = _dot(jax.nn.sigmoid(mixed(c3 + LANES, c3 + RWKV_TAIL)).astype(BF16), g2_ref[...])
        lw_ref[...] = -math.exp(-0.5) * jax.nn.sigmoid(w0_ref[...] + z[:, 0:width])
        a = jax.nn.sigmoid(a0_ref[...] + z[:, width:2 * width])

        project(0, width)

        k = mixed(width, 2 * width)
        k2_ref[...] = k * (1.0 + (a - 1.0) * ka_ref[...])
        kx = k * kk_ref[...]
        gw = min(2 * LANES, width)
        ones = _head_ones(gw)
        ss = jnp.concatenate(
            [_head_sum(kx[:, t * gw:(t + 1) * gw] * kx[:, t * gw:(t + 1) * gw], ones)
             for t in range(width // gw)], axis=1)
        kkn = kx / jnp.maximum(jnp.sqrt(ss), 1e-12)
        kkn_ref[...] = kkn
        b_ref[...] = kkn * a

        project(2 * width, c3)

        r_ref[...] = mixed(0, width)
        v_ref[...] = mixed(2 * width, c3)
        prev_ref[0:1, :] = pr_ref[tm - 1:tm, :]


def proj_rwkv(x, wt, mu, w0, a0, k_k, k_a, w12, g2p, *, S, col0, width, tm=256, tile=384):
    T, D = x.shape
    tm = min(tm, S)
    W = 3 * width + RWKV_TAIL
    tile = _pick_tile(tile, col0, W)
    assert S % tm == 0
    n_wt, first_tile = W // tile, col0 // tile
    tok = lambda s: (jnp.maximum(s - n_wt, 0), 0)
    row = lambda n: pl.BlockSpec((1, n), lambda s: (0, 0))
    out = pl.BlockSpec((tm, width), tok)
    return pl.pallas_call(
        functools.partial(_proj_rwkv_kernel, n_wt=n_wt, first_tile=first_tile, n_cols=wt.shape[0],
                          width=width, tiles_per_seq=S // tm),
        grid=(n_wt + T // tm,),
        in_specs=[
            pl.BlockSpec((tm, D), tok),
            pl.BlockSpec((tile, D), lambda s: (first_tile + jnp.minimum(s, n_wt - 1), 0)),
            row(W), row(width), row(width), row(width), row(width),
            pl.BlockSpec((LANES, 2 * width), lambda s: (0, 0)),
            pl.BlockSpec((2 * LANES, width), lambda s: (0, 0)),
        ],
        out_specs=[out] * 7,
        out_shape=[jax.ShapeDtypeStruct((T, width), F32)] * 7,
        scratch_shapes=[pltpu.VMEM((W, D), BF16), pltpu.VMEM((tm, W), F32), pltpu.VMEM((SUBLANES, W), F32)],
        compiler_params=_cparams(("arbitrary",)),
    )(x, wt, mu, w0, a0, k_k, k_a, w12, g2p)


GROUP = 2


def _block_diag(x):
    head = lax.broadcasted_iota(jnp.int32, x.shape, 1) // HEAD
    zero = jnp.zeros_like(x)
    return jnp.concatenate([jnp.where(head == h, x, zero) for h in range(x.shape[1] // HEAD)], axis=0)


def _chunk_step(rs, lws, k2s, vs, kks, bs, sts):
    C = CHUNK
    GL = rs[0].shape[1]
    idx = range(len(rs))
    row = lax.broadcasted_iota(jnp.int32, (C, GL), 0)
    tcol = lax.broadcasted_iota(jnp.int32, (C, GL), 1) % HEAD
    strict = row > tcol
    incl = row >= tcol
    eye = jnp.where(row == tcol, 1.0, 0.0)
    bdb = lambda x: _block_diag(x).astype(BF16)

    cums, tots, Mbs, Mks, ARSs = [], [], [], [], []
    for i in idx:
        cm = lws[i]
        sh = 1
        while sh < C:
            cm = cm + jnp.where(row >= sh, pltpu.roll(cm, sh, axis=0), 0.0)
            sh *= 2
        e_neg = jnp.exp(-cm)
        AR = jnp.concatenate([-kks[i] * jnp.exp(cm - lws[i]), rs[i] * jnp.exp(cm)], axis=0).astype(BF16)
        cums.append(cm)
        tots.append(cm[C - 1:C, :])
        Mbk = _dot_nt(AR, jnp.concatenate([bdb(bs[i] * e_neg), bdb(k2s[i] * e_neg)], axis=0))
        Mbs.append(Mbk[:, 0:GL])
        Mks.append(Mbk[:, GL:2 * GL])
        ARSs.append(_dot_nt(AR, sts[i].astype(BF16)))
    vbds = [bdb(v) for v in vs]

    Ls = [jnp.where(strict, Mbs[i][0:C], 0.0) for i in idx]
    Ps = [eye + L for L in Ls]
    Lps = [_dot(L.astype(BF16), bdb(L)) for L in Ls]
    Xs = [ARSs[i][0:C] + _dot(jnp.where(strict, Mks[i][0:C], 0.0).astype(BF16), vbds[i]) for i in idx]
    m = 2
    while 2 * m < C:
        both = [_dot(jnp.concatenate([Lps[i], Ps[i]], axis=0).astype(BF16), bdb(Lps[i])) for i in idx]
        Ps = [Ps[i] + both[i][C:2 * C] for i in idx]
        Lps = [bt[0:C] for bt in both]
        m *= 2
    Ps = [Ps[i] + _dot(Ps[i].astype(BF16), bdb(Lps[i])) for i in idx]

    Us = [_dot(Ps[i].astype(BF16), bdb(Xs[i])) for i in idx]
    Ys = [ARSs[i][C:2 * C]
          + _dot(jnp.concatenate([jnp.where(incl, Mbs[i][C:2 * C], 0.0),
                                  jnp.where(incl, Mks[i][C:2 * C], 0.0)], axis=1).astype(BF16),
                 jnp.concatenate([bdb(Us[i]), vbds[i]], axis=0)) for i in idx]
    same_head = _same_head(GL)
    st_news = []
    for i in idx:
        e_rem = jnp.exp(tots[i] - cums[i])
        UV = jnp.concatenate([Us[i], vs[i]], axis=0).astype(BF16)
        BK = jnp.concatenate([bs[i] * e_rem, k2s[i] * e_rem], axis=0).astype(BF16)
        st_news.append(jnp.where(same_head, sts[i] * jnp.exp(tots[i]) + _dot_tn(UV, BK), 0.0))
    return Ys, st_news


def _rwkv_chunk_kernel(r_ref, lw_ref, k2_ref, v_ref, kk_ref, b_ref, g_ref, rk_ref, gng_ref, gnb_ref,
                       *rest, nbatch, ngroups, n_ride):
    ride_in, o_ref, ride_out, st_ref = rest[:n_ride], rest[n_ride], rest[n_ride + 1:-1], rest[-1]
    for w_ref, wb_ref in zip(ride_in, ride_out):
        wb_ref[...] = w_ref[...].astype(BF16)
    c = pl.program_id(2)

    @pl.when(c == 0)
    def _():
        st_ref[...] = jnp.zeros_like(st_ref)

    GL = st_ref.shape[1]
    ones = _head_ones(GL)
    chains = [(bi, p) for bi in range(nbatch) for p in range(ngroups)]
    sl = lambda p: slice(p * GL, (p + 1) * GL)
    get = lambda ref: [ref[bi, :, sl(p)] for bi, p in chains]
    rs, k2s, vs = get(r_ref), get(k2_ref), get(v_ref)
    ys, st_news = _chunk_step(rs, get(lw_ref), k2s, vs, get(kk_ref), get(b_ref),
                              [st_ref[i] for i in range(len(chains))])
    for i in range(len(chains)):
        st_ref[i] = st_news[i]
    ds = [y - _head_sum(y, ones) * (1.0 / HEAD) for y in ys]
    vars_ = [_head_sum(d * d, ones) * (1.0 / HEAD) for d in ds]
    for i, (bi, p) in enumerate(chains):
        yn = ds[i] * lax.rsqrt(vars_[i] + GN_EPS) * gng_ref[:, sl(p)] + gnb_ref[:, sl(p)]
        bonus = _head_sum(rs[i] * k2s[i] * rk_ref[:, sl(p)], ones) * vs[i]
        o_ref[bi, :, sl(p)] = ((yn + bonus) * g_ref[bi, :, sl(p)]).astype(o_ref.dtype)


def rwkv_chunk(r, lw, k2, v, kk, b, g, r_k, gn_g, gn_b, *, ride_along=(), nbatch=4, ngroups=8):
    B, S, W = r.shape
    GL = min(GROUP * HEAD, W)
    ngroups = min(ngroups, W // GL)
    nbatch = min(nbatch, B)
    wb = ngroups * GL
    grid = (B // nbatch, W // wb, S // CHUNK)
    n_steps = grid[0] * grid[1] * grid[2]
    seq = pl.BlockSpec((nbatch, CHUNK, wb), lambda bi, p, c: (bi, c, p))
    par = pl.BlockSpec((1, wb), lambda bi, p, c: (0, p))
    rides = [w for w in ride_along if w.shape[0] % (n_steps * BF16_ROWS) == 0]
    slab = lambda w: pl.BlockSpec((w.shape[0] // n_steps, w.shape[1]),
                                  lambda bi, p, c: ((bi * grid[1] + p) * grid[2] + c, 0))
    outs = pl.pallas_call(
        functools.partial(_rwkv_chunk_kernel, nbatch=nbatch, ngroups=ngroups, n_ride=len(rides)),
        grid=grid,
        in_specs=[seq] * 7 + [par] * 3 + [slab(w) for w in rides],
        out_specs=[seq] + [slab(w) for w in rides],
        out_shape=[jax.ShapeDtypeStruct((B, S, W), BF16)] + [jax.ShapeDtypeStruct(w.shape, BF16) for w in rides],
        scratch_shapes=[pltpu.VMEM((nbatch * ngroups, GL, GL), F32)],
        compiler_params=_cparams(("parallel", "parallel", "arbitrary")),
    )(r, lw, k2, v, kk, b, g, r_k, gn_g, gn_b, *rides)
    cast = iter(outs[1:])
    return outs[0], tuple(next(cast) if any(w is q for q in rides) else w.astype(BF16) for w in ride_along)


def _out_ln_kernel(x_ref, oa_ref, or_ref, wa_ref, wr_ref, g_ref, b_ref, o_ref, *, alpha):
    half = x_ref.shape[0] // 2
    rows = [slice(0, half), slice(half, 2 * half)]
    mixes = [_dot(oa_ref[r, :], wa_ref[...]) + _dot(or_ref[r, :], wr_ref[...]) for r in rows]
    for r, mix in zip(rows, mixes):
        o_ref[r, :] = _layer_norm(alpha * x_ref[r, :] + mix, g_ref[...], b_ref[...])


def out_ln(x, oa, orw, w, g, b, *, alpha, tm=512):
    T, D = x.shape
    tm = min(tm, T)
    ka, kr = oa.shape[1], orw.shape[1]
    assert ka % kr == 0 and w.shape[0] == ka + kr
    return pl.pallas_call(
        functools.partial(_out_ln_kernel, alpha=alpha),
        grid=(T // tm,),
        in_specs=[
            pl.BlockSpec((tm, D), lambda i: (i, 0)),
            pl.BlockSpec((tm, ka), lambda i: (i, 0)),
            pl.BlockSpec((tm, kr), lambda i: (i, 0)),
            pl.BlockSpec((ka, D), lambda i: (0, 0)),
            pl.BlockSpec((kr, D), lambda i: (ka // kr, 0)),
            pl.BlockSpec((1, D), lambda i: (0, 0)),
            pl.BlockSpec((1, D), lambda i: (0, 0)),
        ],
        out_specs=pl.BlockSpec((tm, D), lambda i: (i, 0)),
        out_shape=jax.ShapeDtypeStruct((T, D), F32),
        compiler_params=_cparams(("parallel",)),
    )(x, oa, orw, w, w, g, b)


def _layer(x, l, ffn1_w_gate, ffn1_w_up, ffn1_w_down, ln1_g, ln1_b, w_in,
           lambda_q1, lambda_k1, lambda_q2, lambda_k2, attn_norm_g,
           rwkv_mu, rwkv_w0, rwkv_w2, rwkv_a0, rwkv_a2, rwkv_g2,
           rwkv_k_k, rwkv_k_a, rwkv_r_k, rwkv_gn_g, rwkv_gn_b,
           w_out, ln2_g, ln2_b, ffn2_w_gate, ffn2_w_up, ffn2_w_down, ln3_g, ln3_b):
    B, S, D = x.shape
    T = B * S
    alpha = (2.0 * DEPTH) ** 0.25
    lambda_init = 0.8 - 0.6 * math.exp(-0.3 * l)
    rw = rwkv_w0.shape[-1]
    aw = w_out.shape[1] - rw
    H = aw // LANES
    n_wd, n_ad, n_gd = rwkv_w2.shape[1], rwkv_a2.shape[1], rwkv_g2.shape[1]
    assert n_wd == HEAD and n_ad == HEAD and n_gd <= 2 * LANES
    row = lambda a: a.reshape(1, -1)
    bf = lambda a: a.astype(BF16)

    ln1 = (row(ln1_g[l]), row(ln1_b[l]))
    x0 = x.reshape(T, D)
    x1, *ffn1_w = ffn_ln(x0, ffn1_w_gate[l], ffn1_w_up[l], ffn1_w_down[l], *ln1, alpha=alpha,
                         n_tiles=1, emit_bf16_weights=True)
    if T > x1.shape[0]:
        rest = ffn_ln(x0, *ffn1_w, *ln1, alpha=alpha, tf=2 * FFN_TF, n_skip=1)
        x1 = lax.dynamic_update_slice(rest, x1, (0, 0))

    wi = jnp.swapaxes(w_in[l], 0, 1)
    p_attn = proj_attn(x1, wi, n_cols=3 * aw, n_scaled_cols=aw, scale=HEAD ** -0.5 * math.log2(math.e))
    tab, feat = alibi_tables(H, S)
    o_attn = diff_attn(p_attn, tab, feat, row(lambda_q1[l]), row(lambda_k1[l]), row(lambda_q2[l]),
                       row(lambda_k2[l]), row(attn_norm_g[l]), B=B, S=S, H=H, lambda_init=lambda_init)

    mu = jnp.pad(rwkv_mu[l], (0, RWKV_TAIL - (n_wd + n_ad + n_gd)))
    w12 = jnp.zeros((LANES, 2 * rw), F32)
    w12 = w12.at[:HEAD, :rw].set(rwkv_w2[l]).at[HEAD:, rw:].set(rwkv_a2[l])
    g2p = jnp.pad(rwkv_g2[l], ((0, 2 * LANES - n_gd), (0, 0)))
    seqs = proj_rwkv(x1, wi, row(mu), row(rwkv_w0[l]), row(rwkv_a0[l]), row(rwkv_k_k[l]), row(rwkv_k_a[l]),
                     bf(w12), bf(g2p), S=S, col0=3 * aw, width=rw)
    r, lw, k2, v, kk, b, g = [t.reshape(B, S, rw) for t in seqs]
    o_rwkv, later_w = rwkv_chunk(r, lw, k2, v, kk, b, g, row(rwkv_r_k[l]), row(rwkv_gn_g[l]), row(rwkv_gn_b[l]),
                                 ride_along=(w_out[l], ffn2_w_gate[l], ffn2_w_up[l], ffn2_w_down[l]))

    x2 = out_ln(x1, o_attn, o_rwkv.reshape(T, rw), later_w[0], row(ln2_g[l]), row(ln2_b[l]), alpha=alpha)
    x3 = ffn_ln(x2, *later_w[1:], row(ln3_g[l]), row(ln3_b[l]), alpha=alpha, tf=2 * FFN_TF)
    return x3.reshape(B, S, D)


def kernel(x, ffn1_w_gate, ffn1_w_up, ffn1_w_down, ln1_g, ln1_b, w_in, lambda_q1, lambda_k1, lambda_q2, lambda_k2, attn_norm_g, rwkv_mu, rwkv_w0, rwkv_w2, rwkv_a0, rwkv_a2, rwkv_g2, rwkv_k_k, rwkv_k_a, rwkv_r_k, rwkv_gn_g, rwkv_gn_b, w_out, ln2_g, ln2_b, ffn2_w_gate, ffn2_w_up, ffn2_w_down, ln3_g, ln3_b):
    for l in range(DEPTH):
        x = _layer(x, l, ffn1_w_gate, ffn1_w_up, ffn1_w_down, ln1_g, ln1_b, w_in,
                   lambda_q1, lambda_k1, lambda_q2, lambda_k2, attn_norm_g,
                   rwkv_mu, rwkv_w0, rwkv_w2, rwkv_a0, rwkv_a2, rwkv_g2,
                   rwkv_k_k, rwkv_k_a, rwkv_r_k, rwkv_gn_g, rwkv_gn_b,
                   w_out, ln2_g, ln2_b, ffn2_w_gate, ffn2_w_up, ffn2_w_down, ln3_g, ln3_b)
    return x
```

```python
import functools
import math

import jax
import jax.numpy as jnp
from jax import lax
from jax.experimental import pallas as pl
from jax.experimental.pallas import tpu as pltpu

F32 = jnp.float32
BF16 = jnp.bfloat16

DEPTH = 1
LN_EPS = 1e-5
ATTN_NORM_EPS = 1e-5
GN_EPS = 64e-5
HEAD = 64
LANES = 128
SUBLANES = 8
CHUNK = 64
VMEM_LIMIT = 56 * 1024 * 1024
VMEM_LEAN_LIMIT = 61 * 1024 * 1024
NEG_BIG = -1e30


def _cparams(sem, vmem=VMEM_LIMIT):
    return pltpu.CompilerParams(dimension_semantics=sem, vmem_limit_bytes=vmem)


def _layer_norm(y, g, b):
    mu = jnp.mean(y, axis=-1, keepdims=True)
    d = y - mu
    var = jnp.mean(d * d, axis=-1, keepdims=True)
    return d * lax.rsqrt(var + LN_EPS) * g + b


def _dot(a, b):
    return jnp.dot(a, b, preferred_element_type=F32)


def _dot_nt(a, b):
    return lax.dot_general(a, b, (((1,), (1,)), ((), ())), preferred_element_type=F32)


def _dot_tn(a, b):
    return lax.dot_general(a, b, (((0,), (0,)), ((), ())), preferred_element_type=F32)


def _ffn_ln_kernel(x_ref, wg_ref, wu_ref, wd_ref, g_ref, b_ref, o_ref, xb_ref, *, alpha, ncol):
    j = pl.program_id(1)
    nj = pl.num_programs(1)
    tm, D = o_ref.shape
    cols = [slice(n * ncol, (n + 1) * ncol) for n in range(D // ncol)]

    def body(first, final):
        nrow = 2 if (first or final) else 1
        rows = [slice(r * tm // nrow, (r + 1) * tm // nrow) for r in range(nrow)]
        if first:
            for r in rows:
                x = x_ref[r, :]
                xb_ref[r, :] = x.astype(BF16)
                o_ref[r, :] = alpha * x
        wg = wg_ref[...].astype(BF16)
        wu = wu_ref[...].astype(BF16)
        wd = [wd_ref[:, c].astype(BF16) for c in cols]
        hs = []
        for r in rows:
            xb = xb_ref[r, :]
            hg = _dot(xb, wg)
            hu = _dot(xb, wu)
            hs.append((0.5 * hg * jax.nn.sigmoid(hg) * hu).astype(BF16))
        for r, h in zip(rows, hs):
            for c, w in zip(cols, wd):
                o_ref[r, c] += _dot(h, w)
        if final:
            for r in rows:
                o_ref[r, :] = _layer_norm(o_ref[r, :], g_ref[...], b_ref[...])

    pl.when(j == 0)(lambda: body(True, False))
    pl.when(jnp.logical_and(j > 0, j < nj - 1))(lambda: body(False, False))
    pl.when(jnp.logical_and(j > 0, j == nj - 1))(lambda: body(False, True))


FFN_TF = 256


def ffn_ln(x, wg, wu, wd, g, b, *, alpha, tm=1024, tf=FFN_TF, ncol=512, lean=False):
    T, D = x.shape
    Fd = wg.shape[1]
    tm = min(tm, T)
    tf = min(tf, Fd // 2)
    ncol = min(ncol, D)
    return pl.pallas_call(
        functools.partial(_ffn_ln_kernel, alpha=alpha, ncol=ncol),
        grid=(T // tm, Fd // tf),
        in_specs=[
            pl.BlockSpec((tm, D), lambda i, j: (i, 0), pipeline_mode=pl.Buffered(1) if lean else None),
            pl.BlockSpec((D, tf), lambda i, j: (0, j)),
            pl.BlockSpec((D, tf), lambda i, j: (0, j)),
            pl.BlockSpec((tf, D), lambda i, j: (j, 0)),
            pl.BlockSpec((1, D), lambda i, j: (0, 0)),
            pl.BlockSpec((1, D), lambda i, j: (0, 0)),
        ],
        out_specs=pl.BlockSpec((tm, D), lambda i, j: (i, 0)),
        out_shape=jax.ShapeDtypeStruct((T, D), F32),
        scratch_shapes=[pltpu.VMEM((tm, D), BF16)],
        compiler_params=_cparams(("parallel", "arbitrary"), VMEM_LEAN_LIMIT if lean else VMEM_LIMIT),
    )(x, wg, wu, wd, g, b)


def _pick_tile(limit, *sizes):
    t = limit // LANES * LANES
    while any(s % t for s in sizes):
        t -= LANES
    return t


def _stage_weight_tile(wt_ref, wb_ref, s, first_tile, n_cols):
    tile = wt_ref.shape[0]
    col = (first_tile + s) * tile + lax.broadcasted_iota(jnp.int32, wt_ref.shape, 0)
    wb_ref[pl.ds(pl.multiple_of(s * tile, tile), tile), :] = jnp.where(col < n_cols, wt_ref[...], 0.0).astype(BF16)


def _proj_attn_kernel(x_ref, wt_ref, o_ref, wb_ref, *, n_wt, n_cols, n_scaled, scale):
    s = pl.program_id(0)

    @pl.when(s < n_wt)
    def _():
        _stage_weight_tile(wt_ref, wb_ref, s, 0, n_cols)

    @pl.when(s >= n_wt)
    def _():
        tile = wt_ref.shape[0]
        xb = x_ref[...].astype(BF16)
        for n in range(n_wt):
            acc = _dot_nt(xb, wb_ref[n * tile:(n + 1) * tile, :])
            if n < n_scaled:
                acc = acc * scale
            o_ref[:, n * tile:(n + 1) * tile] = acc.astype(o_ref.dtype)


def proj_attn(x, wt, *, n_cols, n_scaled_cols, scale, tm=512, tile=512):
    T, D = x.shape
    tm = min(tm, T)
    tile = _pick_tile(tile, n_cols, n_scaled_cols)
    n_wt = n_cols // tile
    tok = lambda s: (jnp.maximum(s - n_wt, 0), 0)
    return pl.pallas_call(
        functools.partial(_proj_attn_kernel, n_wt=n_wt, n_cols=wt.shape[0],
                          n_scaled=n_scaled_cols // tile, scale=scale),
        grid=(n_wt + T // tm,),
        in_specs=[
            pl.BlockSpec((tm, D), tok),
            pl.BlockSpec((tile, D), lambda s: (jnp.minimum(s, n_wt - 1), 0)),
        ],
        out_specs=pl.BlockSpec((tm, n_cols), tok),
        out_shape=jax.ShapeDtypeStruct((T, n_cols), BF16),
        scratch_shapes=[pltpu.VMEM((n_cols, D), BF16)],
        compiler_params=_cparams(("arbitrary",)),
    )(x, wt)


ATT_TQ = 256
ATT_TK = 512
N_BIAS_TERMS = 3
POS_RADIX = 64
BF16_ROWS = 16
VT_ROWS = LANES + BF16_ROWS


def alibi_tables(H, S):
    slope = jnp.exp2(-8.0 * (jnp.arange(H, dtype=F32) + 1.0) / H) * math.log2(math.e)
    terms, rest = [], slope
    for _ in range(N_BIAS_TERMS):
        t = rest.astype(BF16).astype(F32)
        terms += [POS_RADIX * t, t]
        rest = rest - t
    tab = jnp.stack(terms, axis=1)
    pos = jnp.arange(S, dtype=jnp.int32)
    hi, lo = (pos // POS_RADIX).astype(F32), (pos % POS_RADIX).astype(F32)
    feat = jnp.stack([hi, lo] * N_BIAS_TERMS, axis=1)
    feat = jnp.pad(feat, ((0, 0), (0, LANES - feat.shape[1]))).astype(BF16)
    return tab, feat


def _diff_attn_kernel(tab_ref, q_ref, k_ref, v_ref, feat_ref, lq1_ref, lk1_ref, lq2_ref, lk2_ref, ng_ref,
                      o_ref, ka_ref, vt_ref, qa_ref, acc_ref, m_ref, *, hb, nck, lambda_init):
    tq = ATT_TQ
    tk = vt_ref.shape[2]
    g = pl.program_id(1)
    i = pl.program_id(2)
    heads = range(hb)
    hs = lambda h: slice(h * LANES, (h + 1) * LANES)

    @pl.when(i == 0)
    def _():
        for h in heads:
            ka_ref[h, :, 0:LANES] = k_ref[:, hs(h)]
            ka_ref[h, :, LANES:2 * LANES] = feat_ref[...]
            for c in range(nck):
                vt_ref[h * nck + c, 0:LANES, :] = v_ref[c * tk:(c + 1) * tk, hs(h)].astype(F32).T.astype(BF16)
                vt_ref[h * nck + c, LANES:VT_ROWS, :] = jnp.ones((BF16_ROWS, tk), BF16)

    lane = lax.broadcasted_iota(jnp.int32, (tq, LANES), 1)
    for h in heads:
        q = q_ref[:, hs(h)]
        zero = jnp.zeros_like(q)
        qf = jnp.zeros((tq, LANES), F32)
        for n in range(2 * N_BIAS_TERMS):
            qf = jnp.where(lane == n, tab_ref[g * hb + h, n], qf)
        qf = qf.astype(BF16)
        qa_ref[h, 0:tq, 0:LANES] = jnp.where(lane < HEAD, q, zero)
        qa_ref[h, tq:2 * tq, 0:LANES] = jnp.where(lane >= HEAD, q, zero)
        qa_ref[h, 0:tq, LANES:2 * LANES] = qf
        qa_ref[h, tq:2 * tq, LANES:2 * LANES] = qf
        m_ref[h] = jnp.full((1, 2 * tq), NEG_BIG, F32)
        acc_ref[h] = jnp.zeros((VT_ROWS, 2 * tq), F32)

    def chunk(c, part=None):
        start = pl.multiple_of(c * tk, tk)
        nk, lanes = tk, slice(None)
        if part is not None:
            start, nk, lanes = start + part * tq, tq, slice(part * tq, (part + 1) * tq)
        ss = [_dot_nt(ka_ref[h, pl.ds(start, nk), :], qa_ref[h]) for h in heads]
        if part is not None:
            key = start + lax.broadcasted_iota(jnp.int32, (nk, 2 * tq), 0)
            qcol = lax.broadcasted_iota(jnp.int32, (nk, 2 * tq), 1)
            qpos = i * tq + jnp.where(qcol >= tq, qcol - tq, qcol)
            keep = key <= qpos
            ss = [jnp.where(keep, s, -jnp.inf) for s in ss]
        m_olds = [m_ref[h] for h in heads]
        m_news = [jnp.maximum(m_olds[h], jnp.max(ss[h], axis=0, keepdims=True)) for h in heads]
        ps = [jnp.exp2(ss[h] - m_news[h]).astype(BF16) for h in heads]
        alphas = [jnp.exp2(m_olds[h] - m_news[h]) for h in heads]
        pvs = [_dot(vt_ref[h * nck + c, :, lanes], ps[h]) for h in heads]
        for h in heads:
            m_ref[h] = m_news[h]
            acc_ref[h] = alphas[h] * acc_ref[h] + pvs[h]

    ratio = tk // tq
    n_full = i // ratio

    def body(c, carry):
        chunk(c)
        return carry

    lax.fori_loop(0, n_full, body, 0)
    chunk(n_full, 0)
    for part in range(1, ratio):
        pl.when(i % ratio >= part)(functools.partial(chunk, n_full, part))

    lam = (jnp.exp(jnp.sum(lq1_ref[...] * lk1_ref[...], axis=-1, keepdims=True))
           - jnp.exp(jnp.sum(lq2_ref[...] * lk2_ref[...], axis=-1, keepdims=True)) + lambda_init)
    for h in heads:
        acc = acc_ref[h]
        rl = 1.0 / acc[LANES:LANES + 1, :]
        ot = (acc[0:LANES, 0:tq] * rl[:, 0:tq]
              - lam * (acc[0:LANES, tq:2 * tq] * rl[:, tq:2 * tq]))
        o = ot.T
        o = o * lax.rsqrt(jnp.mean(o * o, axis=-1, keepdims=True) + ATTN_NORM_EPS) * ng_ref[...]
        o_ref[:, hs(h)] = (o * (1.0 - lambda_init)).astype(o_ref.dtype)


def diff_attn(pa, tab, feat, lq1, lk1, lq2, lk2, norm_g, *, B, S, H, lambda_init, hb=8):
    hb = min(hb, H)
    tq, tk = ATT_TQ, min(ATT_TK, S)
    assert S % tk == 0 and H % hb == 0 and tk % tq == 0
    nq, nck, ng = S // tq, S // tk, H // hb
    wb = hb * LANES
    small = lambda n: pl.BlockSpec((1, n), lambda b, g, i: (0, 0))
    return pl.pallas_call(
        functools.partial(_diff_attn_kernel, hb=hb, nck=nck, lambda_init=lambda_init),
        grid=(B, ng, nq),
        in_specs=[
            pl.BlockSpec(memory_space=pltpu.SMEM),
            pl.BlockSpec((tq, wb), lambda b, g, i: (b * nq + i, g)),
            pl.BlockSpec((S, wb), lambda b, g, i: (b, ng + g)),
            pl.BlockSpec((S, wb), lambda b, g, i: (b, 2 * ng + g)),
            pl.BlockSpec((S, LANES), lambda b, g, i: (0, 0)),
            small(HEAD), small(HEAD), small(HEAD), small(HEAD), small(2 * HEAD),
        ],
        out_specs=pl.BlockSpec((tq, wb), lambda b, g, i: (b * nq + i, g)),
        out_shape=jax.ShapeDtypeStruct((B * S, H * LANES), BF16),
        scratch_shapes=[
            pltpu.VMEM((hb, S, 2 * LANES), BF16),
            pltpu.VMEM((hb * nck, VT_ROWS, tk), BF16),
            pltpu.VMEM((hb, 2 * tq, 2 * LANES), BF16),
            pltpu.VMEM((hb, VT_ROWS, 2 * tq), F32),
            pltpu.VMEM((hb, 1, 2 * tq), F32),
        ],
        compiler_params=_cparams(("parallel", "parallel", "arbitrary")),
    )(tab, pa, pa, pa, feat, lq1, lk1, lq2, lk2, norm_g)


def _same_head(n):
    r = lax.broadcasted_iota(jnp.int32, (n, n), 0) // HEAD
    c = lax.broadcasted_iota(jnp.int32, (n, n), 1) // HEAD
    return r == c


def _head_ones(n):
    return jnp.where(_same_head(n), 1.0, 0.0).astype(BF16)


def _head_sum(x, ones):
    return _dot(x.astype(BF16), ones)


RWKV_TAIL = 3 * LANES


def _proj_rwkv_kernel(x_ref, wt_ref, mu_ref, w0_ref, a0_ref, kk_ref, ka_ref, w12_ref, g2_ref,
                      r_ref, lw_ref, k2_ref, v_ref, kkn_ref, b_ref, g_ref, wb_ref, pr_ref, prev_ref,
                      *, n_wt, first_tile, n_cols, width, tiles_per_seq):
    s = pl.program_id(0)

    @pl.when(s == 0)
    def _():
        prev_ref[...] = jnp.zeros_like(prev_ref)

    @pl.when(s < n_wt)
    def _():
        _stage_weight_tile(wt_ref, wb_ref, s, first_tile, n_cols)

    @pl.when(s >= n_wt)
    def _():
        tm = x_ref.shape[0]
        xb = x_ref[...].astype(BF16)
        c3 = 3 * width
        first = (s - n_wt) % tiles_per_seq == 0
        row = lax.broadcasted_iota(jnp.int32, (tm, 1), 0)

        def project(c0, c1):
            pr_ref[:, c0:c1] = _dot_nt(xb, wb_ref[c0:c1, :])

        def mixed(c0, c1):
            x = pr_ref[:, c0:c1]
            prow = jnp.where(first, 0.0, prev_ref[0:1, c0:c1])
            xs = jnp.where(row == 0, prow, pltpu.roll(x, 1, axis=0))
            return x + mu_ref[:, c0:c1] * (xs - x)

        project(c3, c3 + RWKV_TAIL)
        project(width, 2 * width)

        lora_in = mixed(c3, c3 + LANES)
        lane = lax.broadcasted_iota(jnp.int32, lora_in.shape, 1)
        lora_in = jnp.where(lane < HEAD, jnp.tanh(lora_in), lora_in)
        z = _dot(lora_in.astype(BF16), w12_ref[...])
        g_ref[...] = _dot(jax.nn.sigmoid(mixed(c3 + LANES, c3 + RWKV_TAIL)).astype(BF16), g2_ref[...])
        lw_ref[...] = -math.exp(-0.5) * jax.nn.sigmoid(w0_ref[...] + z[:, 0:width])
        a = jax.nn.sigmoid(a0_ref[...] + z[:, width:2 * width])

        project(0, width)

        k = mixed(width, 2 * width)
        k2_ref[...] = k * (1.0 + (a - 1.0) * ka_ref[...])
        kx = k * kk_ref[...]
        gw = min(2 * LANES, width)
        ones = _head_ones(gw)
        ss = jnp.concatenate(
            [_head_sum(kx[:, t * gw:(t + 1) * gw] * kx[:, t * gw:(t + 1) * gw], ones)
             for t in range(width // gw)], axis=1)
        kkn = kx / jnp.maximum(jnp.sqrt(ss), 1e-12)
        kkn_ref[...] = kkn
        b_ref[...] = kkn * a

        project(2 * width, c3)

        r_ref[...] = mixed(0, width)
        v_ref[...] = mixed(2 * width, c3)
        prev_ref[0:1, :] = pr_ref[tm - 1:tm, :]


def proj_rwkv(x, wt, mu, w0, a0, k_k, k_a, w12, g2p, *, S, col0, width, tm=256, tile=384):
    T, D = x.shape
    tm = min(tm, S)
    W = 3 * width + RWKV_TAIL
    tile = _pick_tile(tile, col0, W)
    assert S % tm == 0
    n_wt, first_tile = W // tile, col0 // tile
    tok = lambda s: (jnp.maximum(s - n_wt, 0), 0)
    row = lambda n: pl.BlockSpec((1, n), lambda s: (0, 0))
    out = pl.BlockSpec((tm, width), tok)
    return pl.pallas_call(
        functools.partial(_proj_rwkv_kernel, n_wt=n_wt, first_tile=first_tile, n_cols=wt.shape[0],
                          width=width, tiles_per_seq=S // tm),
        grid=(n_wt + T // tm,),
        in_specs=[
            pl.BlockSpec((tm, D), tok),
            pl.BlockSpec((tile, D), lambda s: (first_tile + jnp.minimum(s, n_wt - 1), 0)),
            row(W), row(width), row(width), row(width), row(width),
            pl.BlockSpec((LANES, 2 * width), lambda s: (0, 0)),
            pl.BlockSpec((2 * LANES, width), lambda s: (0, 0)),
        ],
        out_specs=[out] * 7,
        out_shape=[jax.ShapeDtypeStruct((T, width), F32)] * 7,
        scratch_shapes=[pltpu.VMEM((W, D), BF16), pltpu.VMEM((tm, W), F32), pltpu.VMEM((SUBLANES, W), F32)],
        compiler_params=_cparams(("arbitrary",)),
    )(x, wt, mu, w0, a0, k_k, k_a, w12, g2p)


GROUP = 2


def _block_diag(x):
    head = lax.broadcasted_iota(jnp.int32, x.shape, 1) // HEAD
    zero = jnp.zeros_like(x)
    return jnp.concatenate([jnp.where(head == h, x, zero) for h in range(x.shape[1] // HEAD)], axis=0)


def _chunk_step(rs, lws, k2s, vs, kks, bs, sts):
    C = CHUNK
    GL = rs[0].shape[1]
    idx = range(len(rs))
    row = lax.broadcasted_iota(jnp.int32, (C, GL), 0)
    tcol = lax.broadcasted_iota(jnp.int32, (C, GL), 1) % HEAD
    strict = row > tcol
    incl = row >= tcol
    eye = jnp.where(row == tcol, 1.0, 0.0)
    bdb = lambda x: _block_diag(x).astype(BF16)

    cums, tots, Mbs, Mks, ARSs = [], [], [], [], []
    for i in idx:
        cm = lws[i]
        sh = 1
        while sh < C:
            cm = cm + jnp.where(row >= sh, pltpu.roll(cm, sh, axis=0), 0.0)
            sh *= 2
        e_neg = jnp.exp(-cm)
        AR = jnp.concatenate([-kks[i] * jnp.exp(cm - lws[i]), rs[i] * jnp.exp(cm)], axis=0).astype(BF16)
        cums.append(cm)
        tots.append(cm[C - 1:C, :])
        Mbk = _dot_nt(AR, jnp.concatenate([bdb(bs[i] * e_neg), bdb(k2s[i] * e_neg)], axis=0))
        Mbs.append(Mbk[:, 0:GL])
        Mks.append(Mbk[:, GL:2 * GL])
        ARSs.append(_dot_nt(AR, sts[i].astype(BF16)))
    vbds = [bdb(v) for v in vs]

    Ls = [jnp.where(strict, Mbs[i][0:C], 0.0) for i in idx]
    Ps = [eye + L for L in Ls]
    Lps = [_dot(L.astype(BF16), bdb(L)) for L in Ls]
    Xs = [ARSs[i][0:C] + _dot(jnp.where(strict, Mks[i][0:C], 0.0).astype(BF16), vbds[i]) for i in idx]
    m = 2
    while 2 * m < C:
        both = [_dot(jnp.concatenate([Lps[i], Ps[i]], axis=0).astype(BF16), bdb(Lps[i])) for i in idx]
        Ps = [Ps[i] + both[i][C:2 * C] for i in idx]
        Lps = [bt[0:C] for bt in both]
        m *= 2
    Ps = [Ps[i] + _dot(Ps[i].astype(BF16), bdb(Lps[i])) for i in idx]

    Us = [_dot(Ps[i].astype(BF16), bdb(Xs[i])) for i in idx]
    Ys = [ARSs[i][C:2 * C]
          + _dot(jnp.concatenate([jnp.where(incl, Mbs[i][C:2 * C], 0.0),
                                  jnp.where(incl, Mks[i][C:2 * C], 0.0)], axis=1).astype(BF16),
                 jnp.concatenate([bdb(Us[i]), vbds[i]], axis=0)) for i in idx]
    same_head = _same_head(GL)
    st_news = []
    for i in idx:
        e_rem = jnp.exp(tots[i] - cums[i])
        UV = jnp.concatenate([Us[i], vs[i]], axis=0).astype(BF16)
        BK = jnp.concatenate([bs[i] * e_rem, k2s[i] * e_rem], axis=0).astype(BF16)
        st_news.append(jnp.where(same_head, sts[i] * jnp.exp(tots[i]) + _dot_tn(UV, BK), 0.0))
    return Ys, st_news


def _rwkv_chunk_kernel(r_ref, lw_ref, k2_ref, v_ref, kk_ref, b_ref, g_ref, rk_ref, gng_ref, gnb_ref,
                       *rest, nbatch, ngroups, n_ride):
    ride_in, o_ref, ride_out, st_ref = rest[:n_ride], rest[n_ride], rest[n_ride + 1:-1], rest[-1]
    for w_ref, wb_ref in zip(ride_in, ride_out):
        wb_ref[...] = w_ref[...].astype(BF16)
    c = pl.program_id(2)

    @pl.when(c == 0)
    def _():
        st_ref[...] = jnp.zeros_like(st_ref)

    GL = st_ref.shape[1]
    ones = _head_ones(GL)
    chains = [(bi, p) for bi in range(nbatch) for p in range(ngroups)]
    sl = lambda p: slice(p * GL, (p + 1) * GL)
    get = lambda ref: [ref[bi, :, sl(p)] for bi, p in chains]
    rs, k2s, vs = get(r_ref), get(k2_ref), get(v_ref)
    ys, st_news = _chunk_step(rs, get(lw_ref), k2s, vs, get(kk_ref), get(b_ref),
                              [st_ref[i] for i in range(len(chains))])
    for i in range(len(chains)):
        st_ref[i] = st_news[i]
    ds = [y - _head_sum(y, ones) * (1.0 / HEAD) for y in ys]
    vars_ = [_head_sum(d * d, ones) * (1.0 / HEAD) for d in ds]
    for i, (bi, p) in enumerate(chains):
        yn = ds[i] * lax.rsqrt(vars_[i] + GN_EPS) * gng_ref[:, sl(p)] + gnb_ref[:, sl(p)]
        bonus = _head_sum(rs[i] * k2s[i] * rk_ref[:, sl(p)], ones) * vs[i]
        o_ref[bi, :, sl(p)] = ((yn + bonus) * g_ref[bi, :, sl(p)]).astype(o_ref.dtype)


def rwkv_chunk(r, lw, k2, v, kk, b, g, r_k, gn_g, gn_b, *, ride_along=(), nbatch=4, ngroups=8):
    B, S, W = r.shape
    GL = min(GROUP * HEAD, W)
    ngroups = min(ngroups, W // GL)
    nbatch = min(nbatch, B)
    wb = ngroups * GL
    grid = (B // nbatch, W // wb, S // CHUNK)
    n_steps = grid[0] * grid[1] * grid[2]
    seq = pl.BlockSpec((nbatch, CHUNK, wb), lambda bi, p, c: (bi, c, p))
    par = pl.BlockSpec((1, wb), lambda bi, p, c: (0, p))
    rides = [w for w in ride_along if w.shape[0] % (n_steps * BF16_ROWS) == 0]
    slab = lambda w: pl.BlockSpec((w.shape[0] // n_steps, w.shape[1]),
                                  lambda bi, p, c: ((bi * grid[1] + p) * grid[2] + c, 0))
    outs = pl.pallas_call(
        functools.partial(_rwkv_chunk_kernel, nbatch=nbatch, ngroups=ngroups, n_ride=len(rides)),
        grid=grid,
        in_specs=[seq] * 7 + [par] * 3 + [slab(w) for w in rides],
        out_specs=[seq] + [slab(w) for w in rides],
        out_shape=[jax.ShapeDtypeStruct((B, S, W), BF16)] + [jax.ShapeDtypeStruct(w.shape, BF16) for w in rides],
        scratch_shapes=[pltpu.VMEM((nbatch * ngroups, GL, GL), F32)],
        compiler_params=_cparams(("parallel", "parallel", "arbitrary")),
    )(r, lw, k2, v, kk, b, g, r_k, gn_g, gn_b, *rides)
    cast = iter(outs[1:])
    return outs[0], tuple(next(cast) if any(w is q for q in rides) else w.astype(BF16) for w in ride_along)


def _out_ln_kernel(x_ref, oa_ref, or_ref, wa_ref, wr_ref, g_ref, b_ref, o_ref, *, alpha):
    half = x_ref.shape[0] // 2
    rows = [slice(0, half), slice(half, 2 * half)]
    mixes = [_dot(oa_ref[r, :], wa_ref[...]) + _dot(or_ref[r, :], wr_ref[...]) for r in rows]
    for r, mix in zip(rows, mixes):
        o_ref[r, :] = _layer_norm(alpha * x_ref[r, :] + mix, g_ref[...], b_ref[...])


def out_ln(x, oa, orw, w, g, b, *, alpha, tm=512):
    T, D = x.shape
    tm = min(tm, T)
    ka, kr = oa.shape[1], orw.shape[1]
    assert ka % kr == 0 and w.shape[0] == ka + kr
    return pl.pallas_call(
        functools.partial(_out_ln_kernel, alpha=alpha),
        grid=(T // tm,),
        in_specs=[
            pl.BlockSpec((tm, D), lambda i: (i, 0)),
            pl.BlockSpec((tm, ka), lambda i: (i, 0)),
            pl.BlockSpec((tm, kr), lambda i: (i, 0)),
            pl.BlockSpec((ka, D), lambda i: (0, 0)),
            pl.BlockSpec((kr, D), lambda i: (ka // kr, 0)),
            pl.BlockSpec((1, D), lambda i: (0, 0)),
            pl.BlockSpec((1, D), lambda i: (0, 0)),
        ],
        out_specs=pl.BlockSpec((tm, D), lambda i: (i, 0)),
        out_shape=jax.ShapeDtypeStruct((T, D), F32),
        compiler_params=_cparams(("parallel",)),
    )(x, oa, orw, w, w, g, b)


def _layer(x, l, ffn1_w_gate, ffn1_w_up, ffn1_w_down, ln1_g, ln1_b, w_in,
           lambda_q1, lambda_k1, lambda_q2, lambda_k2, attn_norm_g,
           rwkv_mu, rwkv_w0, rwkv_w2, rwkv_a0, rwkv_a2, rwkv_g2,
           rwkv_k_k, rwkv_k_a, rwkv_r_k, rwkv_gn_g, rwkv_gn_b,
           w_out, ln2_g, ln2_b, ffn2_w_gate, ffn2_w_up, ffn2_w_down, ln3_g, ln3_b):
    B, S, D = x.shape
    T = B * S
    alpha = (2.0 * DEPTH) ** 0.25
    lambda_init = 0.8 - 0.6 * math.exp(-0.3 * l)
    rw = rwkv_w0.shape[-1]
    aw = w_out.shape[1] - rw
    H = aw // LANES
    n_wd, n_ad, n_gd = rwkv_w2.shape[1], rwkv_a2.shape[1], rwkv_g2.shape[1]
    assert n_wd == HEAD and n_ad == HEAD and n_gd <= 2 * LANES
    row = lambda a: a.reshape(1, -1)
    bf = lambda a: a.astype(BF16)

    x1 = ffn_ln(x.reshape(T, D), ffn1_w_gate[l], ffn1_w_up[l], ffn1_w_down[l],
                row(ln1_g[l]), row(ln1_b[l]), alpha=alpha, tf=2 * FFN_TF, lean=True)

    wi = jnp.swapaxes(w_in[l], 0, 1)
    p_attn = proj_attn(x1, wi, n_cols=3 * aw, n_scaled_cols=aw, scale=HEAD ** -0.5 * math.log2(math.e))
    tab, feat = alibi_tables(H, S)
    o_attn = diff_attn(p_attn, tab, feat, row(lambda_q1[l]), row(lambda_k1[l]), row(lambda_q2[l]),
                       row(lambda_k2[l]), row(attn_norm_g[l]), B=B, S=S, H=H, lambda_init=lambda_init)

    mu = jnp.pad(rwkv_mu[l], (0, RWKV_TAIL - (n_wd + n_ad + n_gd)))
    w12 = jnp.zeros((LANES, 2 * rw), F32)
    w12 = w12.at[:HEAD, :rw].set(rwkv_w2[l]).at[HEAD:, rw:].set(rwkv_a2[l])
    g2p = jnp.pad(rwkv_g2[l], ((0, 2 * LANES - n_gd), (0, 0)))
    seqs = proj_rwkv(x1, wi, row(mu), row(rwkv_w0[l]), row(rwkv_a0[l]), row(rwkv_k_k[l]), row(rwkv_k_a[l]),
                     bf(w12), bf(g2p), S=S, col0=3 * aw, width=rw)
    r, lw, k2, v, kk, b, g = [t.reshape(B, S, rw) for t in seqs]
    o_rwkv, later_w = rwkv_chunk(r, lw, k2, v, kk, b, g, row(rwkv_r_k[l]), row(rwkv_gn_g[l]), row(rwkv_gn_b[l]),
                                 ride_along=(w_out[l], ffn2_w_gate[l], ffn2_w_up[l], ffn2_w_down[l]))

    x2 = out_ln(x1, o_attn, o_rwkv.reshape(T, rw), later_w[0], row(ln2_g[l]), row(ln2_b[l]), alpha=alpha)
    x3 = ffn_ln(x2, *later_w[1:], row(ln3_g[l]), row(ln3_b[l]), alpha=alpha, tf=2 * FFN_TF)
    return x3.reshape(B, S, D)


def kernel(x, ffn1_w_gate, ffn1_w_up, ffn1_w_down, ln1_g, ln1_b, w_in, lambda_q1, lambda_k1, lambda_q2, lambda_k2, attn_norm_g, rwkv_mu, rwkv_w0, rwkv_w2, rwkv_a0, rwkv_a2, rwkv_g2, rwkv_k_k, rwkv_k_a, rwkv_r_k, rwkv_gn_g, rwkv_gn_b, w_out, ln2_g, ln2_b, ffn2_w_gate, ffn2_w_up, ffn2_w_down, ln3_g, ln3_b):
    for l in range(DEPTH):
        x = _layer(x, l, ffn1_w_gate, ffn1_w_up, ffn1_w_down, ln1_g, ln1_b, w_in,
                   lambda_q1, lambda_k1, lambda_q2, lambda_k2, attn_norm_g,
                   rwkv_mu, rwkv_w0, rwkv_w2, rwkv_a0, rwkv_a2, rwkv_g2,
                   rwkv_k_k, rwkv_k_a, rwkv_r_k, rwkv_gn_g, rwkv_gn_b,
                   w_out, ln2_g, ln2_b, ffn2_w_gate, ffn2_w_up, ffn2_w_down, ln3_g, ln3_b)
    return x
```

```python
import functools
import math

import jax
import jax.numpy as jnp
from jax import lax
from jax.experimental import pallas as pl
from jax.experimental.pallas import tpu as pltpu

F32 = jnp.float32
BF16 = jnp.bfloat16

DEPTH = 1
LN_EPS = 1e-5
ATTN_NORM_EPS = 1e-5
GN_EPS = 64e-5
HEAD = 64
LANES = 128
SUBLANES = 8
CHUNK = 64
VMEM_LIMIT = 56 * 1024 * 1024
NEG_BIG = -1e30


def _cparams(sem):
    return pltpu.CompilerParams(dimension_semantics=sem, vmem_limit_bytes=VMEM_LIMIT)


def _layer_norm(y, g, b):
    mu = jnp.mean(y, axis=-1, keepdims=True)
    d = y - mu
    var = jnp.mean(d * d, axis=-1, keepdims=True)
    return d * lax.rsqrt(var + LN_EPS) * g + b


def _dot(a, b):
    return jnp.dot(a, b, preferred_element_type=F32)


def _dot_nt(a, b):
    return lax.dot_general(a, b, (((1,), (1,)), ((), ())), preferred_element_type=F32)


def _dot_tn(a, b):
    return lax.dot_general(a, b, (((0,), (0,)), ((), ())), preferred_element_type=F32)


def _ffn_ln_kernel(x_ref, wg_ref, wu_ref, wd_ref, g_ref, b_ref, o_ref, xb_ref, *, alpha, ncol):
    j = pl.program_id(1)
    nj = pl.num_programs(1)
    tm, D = o_ref.shape
    cols = [slice(n * ncol, (n + 1) * ncol) for n in range(D // ncol)]

    def body(first, final):
        nrow = 2 if (first or final) else 1
        rows = [slice(r * tm // nrow, (r + 1) * tm // nrow) for r in range(nrow)]
        if first:
            for r in rows:
                x = x_ref[r, :]
                xb_ref[r, :] = x.astype(BF16)
                o_ref[r, :] = alpha * x
        wg = wg_ref[...].astype(BF16)
        wu = wu_ref[...].astype(BF16)
        wd = [wd_ref[:, c].astype(BF16) for c in cols]
        hs = []
        for r in rows:
            xb = xb_ref[r, :]
            hg = _dot(xb, wg)
            hu = _dot(xb, wu)
            hs.append((0.5 * hg * jax.nn.sigmoid(hg) * hu).astype(BF16))
        for r, h in zip(rows, hs):
            for c, w in zip(cols, wd):
                o_ref[r, c] += _dot(h, w)
        if final:
            for r in rows:
                o_ref[r, :] = _layer_norm(o_ref[r, :], g_ref[...], b_ref[...])

    pl.when(j == 0)(lambda: body(True, False))
    pl.when(jnp.logical_and(j > 0, j < nj - 1))(lambda: body(False, False))
    pl.when(jnp.logical_and(j > 0, j == nj - 1))(lambda: body(False, True))


FFN_TF = 256


def ffn_ln(x, wg, wu, wd, g, b, *, alpha, tm=1024, tf=FFN_TF, ncol=512):
    T, D = x.shape
    Fd = wg.shape[1]
    tm = min(tm, T)
    tf = min(tf, Fd // 2)
    ncol = min(ncol, D)
    return pl.pallas_call(
        functools.partial(_ffn_ln_kernel, alpha=alpha, ncol=ncol),
        grid=(T // tm, Fd // tf),
        in_specs=[
            pl.BlockSpec((tm, D), lambda i, j: (i, 0)),
            pl.BlockSpec((D, tf), lambda i, j: (0, j)),
            pl.BlockSpec((D, tf), lambda i, j: (0, j)),
            pl.BlockSpec((tf, D), lambda i, j: (j, 0)),
            pl.BlockSpec((1, D), lambda i, j: (0, 0)),
            pl.BlockSpec((1, D), lambda i, j: (0, 0)),
        ],
        out_specs=pl.BlockSpec((tm, D), lambda i, j: (i, 0)),
        out_shape=jax.ShapeDtypeStruct((T, D), F32),
        scratch_shapes=[pltpu.VMEM((tm, D), BF16)],
        compiler_params=_cparams(("parallel", "arbitrary")),
    )(x, wg, wu, wd, g, b)


def _ffn_ln_piped_kernel(x_ref, wg_hbm, wu_hbm, wd_hbm, g_ref, b_ref, o_ref, xb_ref, *, alpha, ncol, tf, nbuf):
    tm, D = o_ref.shape
    Fd = wg_hbm.shape[1]
    cols = [slice(n * ncol, (n + 1) * ncol) for n in range(D // ncol)]
    x = x_ref[...]
    xb_ref[...] = x.astype(BF16)
    o_ref[...] = alpha * x

    def tile(wg_ref, wu_ref, wd_ref):
        xb = xb_ref[...]
        hg = _dot(xb, wg_ref[...].astype(BF16))
        hu = _dot(xb, wu_ref[...].astype(BF16))
        h = (0.5 * hg * jax.nn.sigmoid(hg) * hu).astype(BF16)
        for c in cols:
            o_ref[:, c] += _dot(h, wd_ref[:, c].astype(BF16))

    mode = pl.Buffered(nbuf)
    pltpu.emit_pipeline(
        tile, grid=(Fd // tf,),
        in_specs=[pl.BlockSpec((D, tf), lambda j: (0, j), pipeline_mode=mode),
                  pl.BlockSpec((D, tf), lambda j: (0, j), pipeline_mode=mode),
                  pl.BlockSpec((tf, D), lambda j: (j, 0))],
    )(wg_hbm, wu_hbm, wd_hbm)
    o_ref[...] = _layer_norm(o_ref[...], g_ref[...], b_ref[...])


VMEM_PIPED_LIMIT = 62 * 1024 * 1024


def ffn_ln_piped(x, wg, wu, wd, g, b, *, alpha, tm=1024, tf=FFN_TF, ncol=512, nbuf=3):
    T, D = x.shape
    tm = min(tm, T)
    tf = min(tf, wg.shape[1])
    ncol = min(ncol, D)
    hbm = pl.BlockSpec(memory_space=pl.ANY)
    return pl.pallas_call(
        functools.partial(_ffn_ln_piped_kernel, alpha=alpha, ncol=ncol, tf=tf, nbuf=nbuf),
        grid=(T // tm,),
        in_specs=[pl.BlockSpec((tm, D), lambda i: (i, 0)), hbm, hbm, hbm,
                  pl.BlockSpec((1, D), lambda i: (0, 0)), pl.BlockSpec((1, D), lambda i: (0, 0))],
        out_specs=pl.BlockSpec((tm, D), lambda i: (i, 0)),
        out_shape=jax.ShapeDtypeStruct((T, D), F32),
        scratch_shapes=[pltpu.VMEM((tm, D), BF16)],
        compiler_params=pltpu.CompilerParams(dimension_semantics=("arbitrary",),
                                             vmem_limit_bytes=VMEM_PIPED_LIMIT),
    )(x, wg, wu, wd, g, b)


def _pick_tile(limit, *sizes):
    t = limit // LANES * LANES
    while any(s % t for s in sizes):
        t -= LANES
    return t


def _stage_weight_tile(wt_ref, wb_ref, s, first_tile, n_cols):
    tile = wt_ref.shape[0]
    col = (first_tile + s) * tile + lax.broadcasted_iota(jnp.int32, wt_ref.shape, 0)
    wb_ref[pl.ds(pl.multiple_of(s * tile, tile), tile), :] = jnp.where(col < n_cols, wt_ref[...], 0.0).astype(BF16)


def _proj_attn_kernel(x_ref, wt_ref, o_ref, wb_ref, *, n_wt, n_cols, n_scaled, scale):
    s = pl.program_id(0)

    @pl.when(s < n_wt)
    def _():
        _stage_weight_tile(wt_ref, wb_ref, s, 0, n_cols)

    @pl.when(s >= n_wt)
    def _():
        tile = wt_ref.shape[0]
        xb = x_ref[...].astype(BF16)
        for n in range(n_wt):
            acc = _dot_nt(xb, wb_ref[n * tile:(n + 1) * tile, :])
            if n < n_scaled:
                acc = acc * scale
            o_ref[:, n * tile:(n + 1) * tile] = acc.astype(o_ref.dtype)


def proj_attn(x, wt, *, n_cols, n_scaled_cols, scale, tm=512, tile=512):
    T, D = x.shape
    tm = min(tm, T)
    tile = _pick_tile(tile, n_cols, n_scaled_cols)
    n_wt = n_cols // tile
    tok = lambda s: (jnp.maximum(s - n_wt, 0), 0)
    return pl.pallas_call(
        functools.partial(_proj_attn_kernel, n_wt=n_wt, n_cols=wt.shape[0],
                          n_scaled=n_scaled_cols // tile, scale=scale),
        grid=(n_wt + T // tm,),
        in_specs=[
            pl.BlockSpec((tm, D), tok),
            pl.BlockSpec((tile, D), lambda s: (jnp.minimum(s, n_wt - 1), 0)),
        ],
        out_specs=pl.BlockSpec((tm, n_cols), tok),
        out_shape=jax.ShapeDtypeStruct((T, n_cols), BF16),
        scratch_shapes=[pltpu.VMEM((n_cols, D), BF16)],
        compiler_params=_cparams(("arbitrary",)),
    )(x, wt)


ATT_TQ = 256
ATT_TK = 512
N_BIAS_TERMS = 3
POS_RADIX = 64
BF16_ROWS = 16
VT_ROWS = LANES + BF16_ROWS


def alibi_tables(H, S):
    slope = jnp.exp2(-8.0 * (jnp.arange(H, dtype=F32) + 1.0) / H) * math.log2(math.e)
    terms, rest = [], slope
    for _ in range(N_BIAS_TERMS):
        t = rest.astype(BF16).astype(F32)
        terms += [POS_RADIX * t, t]
        rest = rest - t
    tab = jnp.stack(terms, axis=1)
    pos = jnp.arange(S, dtype=jnp.int32)
    hi, lo = (pos // POS_RADIX).astype(F32), (pos % POS_RADIX).astype(F32)
    feat = jnp.stack([hi, lo] * N_BIAS_TERMS, axis=1)
    feat = jnp.pad(feat, ((0, 0), (0, LANES - feat.shape[1]))).astype(BF16)
    return tab, feat


def _diff_attn_kernel(tab_ref, q_ref, k_ref, v_ref, feat_ref, lq1_ref, lk1_ref, lq2_ref, lk2_ref, ng_ref,
                      o_ref, ka_ref, vt_ref, qa_ref, acc_ref, m_ref, *, hb, nck, lambda_init):
    tq = ATT_TQ
    tk = vt_ref.shape[2]
    g = pl.program_id(1)
    i = pl.program_id(2)
    heads = range(hb)
    hs = lambda h: slice(h * LANES, (h + 1) * LANES)

    @pl.when(i == 0)
    def _():
        for h in heads:
            ka_ref[h, :, 0:LANES] = k_ref[:, hs(h)]
            ka_ref[h, :, LANES:2 * LANES] = feat_ref[...]
            for c in range(nck):
                vt_ref[h * nck + c, 0:LANES, :] = v_ref[c * tk:(c + 1) * tk, hs(h)].astype(F32).T.astype(BF16)
                vt_ref[h * nck + c, LANES:VT_ROWS, :] = jnp.ones((BF16_ROWS, tk), BF16)

    lane = lax.broadcasted_iota(jnp.int32, (tq, LANES), 1)
    for h in heads:
        q = q_ref[:, hs(h)]
        zero = jnp.zeros_like(q)
        qf = jnp.zeros((tq, LANES), F32)
        for n in range(2 * N_BIAS_TERMS):
            qf = jnp.where(lane == n, tab_ref[g * hb + h, n], qf)
        qf = qf.astype(BF16)
        qa_ref[h, 0:tq, 0:LANES] = jnp.where(lane < HEAD, q, zero)
        qa_ref[h, tq:2 * tq, 0:LANES] = jnp.where(lane >= HEAD, q, zero)
        qa_ref[h, 0:tq, LANES:2 * LANES] = qf
        qa_ref[h, tq:2 * tq, LANES:2 * LANES] = qf
        m_ref[h] = jnp.full((1, 2 * tq), NEG_BIG, F32)
        acc_ref[h] = jnp.zeros((VT_ROWS, 2 * tq), F32)

    def chunk(c, part=None):
        start = pl.multiple_of(c * tk, tk)
        nk, lanes = tk, slice(None)
        if part is not None:
            start, nk, lanes = start + part * tq, tq, slice(part * tq, (part + 1) * tq)
        ss = [_dot_nt(ka_ref[h, pl.ds(start, nk), :], qa_ref[h]) for h in heads]
        if part is not None:
            key = start + lax.broadcasted_iota(jnp.int32, (nk, 2 * tq), 0)
            qcol = lax.broadcasted_iota(jnp.int32, (nk, 2 * tq), 1)
            qpos = i * tq + jnp.where(qcol >= tq, qcol - tq, qcol)
            keep = key <= qpos
            ss = [jnp.where(keep, s, -jnp.inf) for s in ss]
        m_olds = [m_ref[h] for h in heads]
        m_news = [jnp.maximum(m_olds[h], jnp.max(ss[h], axis=0, keepdims=True)) for h in heads]
        ps = [jnp.exp2(ss[h] - m_news[h]).astype(BF16) for h in heads]
        alphas = [jnp.exp2(m_olds[h] - m_news[h]) for h in heads]
        pvs = [_dot(vt_ref[h * nck + c, :, lanes], ps[h]) for h in heads]
        for h in heads:
            m_ref[h] = m_news[h]
            acc_ref[h] = alphas[h] * acc_ref[h] + pvs[h]

    ratio = tk // tq
    n_full = i // ratio

    def body(c, carry):
        chunk(c)
        return carry

    lax.fori_loop(0, n_full, body, 0)
    chunk(n_full, 0)
    for part in range(1, ratio):
        pl.when(i % ratio >= part)(functools.partial(chunk, n_full, part))

    lam = (jnp.exp(jnp.sum(lq1_ref[...] * lk1_ref[...], axis=-1, keepdims=True))
           - jnp.exp(jnp.sum(lq2_ref[...] * lk2_ref[...], axis=-1, keepdims=True)) + lambda_init)
    for h in heads:
        acc = acc_ref[h]
        rl = 1.0 / acc[LANES:LANES + 1, :]
        ot = (acc[0:LANES, 0:tq] * rl[:, 0:tq]
              - lam * (acc[0:LANES, tq:2 * tq] * rl[:, tq:2 * tq]))
        o = ot.T
        o = o * lax.rsqrt(jnp.mean(o * o, axis=-1, keepdims=True) + ATTN_NORM_EPS) * ng_ref[...]
        o_ref[:, hs(h)] = (o * (1.0 - lambda_init)).astype(o_ref.dtype)


def diff_attn(pa, tab, feat, lq1, lk1, lq2, lk2, norm_g, *, B, S, H, lambda_init, hb=8):
    hb = min(hb, H)
    tq, tk = ATT_TQ, min(ATT_TK, S)
    assert S % tk == 0 and H % hb == 0 and tk % tq == 0
    nq, nck, ng = S // tq, S // tk, H // hb
    wb = hb * LANES
    small = lambda n: pl.BlockSpec((1, n), lambda b, g, i: (0, 0))
    return pl.pallas_call(
        functools.partial(_diff_attn_kernel, hb=hb, nck=nck, lambda_init=lambda_init),
        grid=(B, ng, nq),
        in_specs=[
            pl.BlockSpec(memory_space=pltpu.SMEM),
            pl.BlockSpec((tq, wb), lambda b, g, i: (b * nq + i, g)),
            pl.BlockSpec((S, wb), lambda b, g, i: (b, ng + g)),
            pl.BlockSpec((S, wb), lambda b, g, i: (b, 2 * ng + g)),
            pl.BlockSpec((S, LANES), lambda b, g, i: (0, 0)),
            small(HEAD), small(HEAD), small(HEAD), small(HEAD), small(2 * HEAD),
        ],
        out_specs=pl.BlockSpec((tq, wb), lambda b, g, i: (b * nq + i, g)),
        out_shape=jax.ShapeDtypeStruct((B * S, H * LANES), BF16),
        scratch_shapes=[
            pltpu.VMEM((hb, S, 2 * LANES), BF16),
            pltpu.VMEM((hb * nck, VT_ROWS, tk), BF16),
            pltpu.VMEM((hb, 2 * tq, 2 * LANES), BF16),
            pltpu.VMEM((hb, VT_ROWS, 2 * tq), F32),
            pltpu.VMEM((hb, 1, 2 * tq), F32),
        ],
        compiler_params=_cparams(("parallel", "parallel", "arbitrary")),
    )(tab, pa, pa, pa, feat, lq1, lk1, lq2, lk2, norm_g)


def _same_head(n):
    r = lax.broadcasted_iota(jnp.int32, (n, n), 0) // HEAD
    c = lax.broadcasted_iota(jnp.int32, (n, n), 1) // HEAD
    return r == c


def _head_ones(n):
    return jnp.where(_same_head(n), 1.0, 0.0).astype(BF16)


def _head_sum(x, ones):
    return _dot(x.astype(BF16), ones)


RWKV_TAIL = 3 * LANES


def _proj_rwkv_kernel(x_ref, wt_ref, mu_ref, w0_ref, a0_ref, kk_ref, ka_ref, w12_ref, g2_ref,
                      r_ref, lw_ref, k2_ref, v_ref, kkn_ref, b_ref, g_ref, wb_ref, pr_ref, prev_ref,
                      *, n_wt, first_tile, n_cols, width, tiles_per_seq):
    s = pl.program_id(0)

    @pl.when(s == 0)
    def _():
        prev_ref[...] = jnp.zeros_like(prev_ref)

    @pl.when(s < n_wt)
    def _():
        _stage_weight_tile(wt_ref, wb_ref, s, first_tile, n_cols)

    @pl.when(s >= n_wt)
    def _():
        tm = x_ref.shape[0]
        xb = x_ref[...].astype(BF16)
        c3 = 3 * width
        first = (s - n_wt) % tiles_per_seq == 0
        row = lax.broadcasted_iota(jnp.int32, (tm, 1), 0)

        def project(c0, c1):
            pr_ref[:, c0:c1] = _dot_nt(xb, wb_ref[c0:c1, :])

        def mixed(c0, c1):
            x = pr_ref[:, c0:c1]
            prow = jnp.where(first, 0.0, prev_ref[0:1, c0:c1])
            xs = jnp.where(row == 0, prow, pltpu.roll(x, 1, axis=0))
            return x + mu_ref[:, c0:c1] * (xs - x)

        project(c3, c3 + RWKV_TAIL)
        project(width, 2 * width)

        lora_in = mixed(c3, c3 + LANES)
        lane = lax.broadcasted_iota(jnp.int32, lora_in.shape, 1)
        lora_in = jnp.where(lane < HEAD, jnp.tanh(lora_in), lora_in)
        z = _dot(lora_in.astype(BF16), w12_ref[...])
        g_ref[...] = _dot(jax.nn.sigmoid(mixed(c3 + LANES, c3 + RWKV_TAIL)).astype(BF16), g2_ref[...])
        lw_ref[...] = -math.exp(-0.5) * jax.nn.sigmoid(w0_ref[...] + z[:, 0:width])
        a = jax.nn.sigmoid(a0_ref[...] + z[:, width:2 * width])

        project(0, width)

        k = mixed(width, 2 * width)
        k2_ref[...] = k * (1.0 + (a - 1.0) * ka_ref[...])
        kx = k * kk_ref[...]
        gw = min(2 * LANES, width)
        ones = _head_ones(gw)
        ss = jnp.concatenate(
            [_head_sum(kx[:, t * gw:(t + 1) * gw] * kx[:, t * gw:(t + 1) * gw], ones)
             for t in range(width // gw)], axis=1)
        kkn = kx / jnp.maximum(jnp.sqrt(ss), 1e-12)
        kkn_ref[...] = kkn
        b_ref[...] = kkn * a

        project(2 * width, c3)

        r_ref[...] = mixed(0, width)
        v_ref[...] = mixed(2 * width, c3)
        prev_ref[0:1, :] = pr_ref[tm - 1:tm, :]


def proj_rwkv(x, wt, mu, w0, a0, k_k, k_a, w12, g2p, *, S, col0, width, tm=256, tile=384):
    T, D = x.shape
    tm = min(tm, S)
    W = 3 * width + RWKV_TAIL
    tile = _pick_tile(tile, col0, W)
    assert S % tm == 0
    n_wt, first_tile = W // tile, col0 // tile
    tok = lambda s: (jnp.maximum(s - n_wt, 0), 0)
    row = lambda n: pl.BlockSpec((1, n), lambda s: (0, 0))
    out = pl.BlockSpec((tm, width), tok)
    return pl.pallas_call(
        functools.partial(_proj_rwkv_kernel, n_wt=n_wt, first_tile=first_tile, n_cols=wt.shape[0],
                          width=width, tiles_per_seq=S // tm),
        grid=(n_wt + T // tm,),
        in_specs=[
            pl.BlockSpec((tm, D), tok),
            pl.BlockSpec((tile, D), lambda s: (first_tile + jnp.minimum(s, n_wt - 1), 0)),
            row(W), row(width), row(width), row(width), row(width),
            pl.BlockSpec((LANES, 2 * width), lambda s: (0, 0)),
            pl.BlockSpec((2 * LANES, width), lambda s: (0, 0)),
        ],
        out_specs=[out] * 7,
        out_shape=[jax.ShapeDtypeStruct((T, width), F32)] * 7,
        scratch_shapes=[pltpu.VMEM((W, D), BF16), pltpu.VMEM((tm, W), F32), pltpu.VMEM((SUBLANES, W), F32)],
        compiler_params=_cparams(("arbitrary",)),
    )(x, wt, mu, w0, a0, k_k, k_a, w12, g2p)


GROUP = 2


def _block_diag(x):
    head = lax.broadcasted_iota(jnp.int32, x.shape, 1) // HEAD
    zero = jnp.zeros_like(x)
    return jnp.concatenate([jnp.where(head == h, x, zero) for h in range(x.shape[1] // HEAD)], axis=0)


def _chunk_step(rs, lws, k2s, vs, kks, bs, sts):
    C = CHUNK
    GL = rs[0].shape[1]
    idx = range(len(rs))
    row = lax.broadcasted_iota(jnp.int32, (C, GL), 0)
    tcol = lax.broadcasted_iota(jnp.int32, (C, GL), 1) % HEAD
    strict = row > tcol
    incl = row >= tcol
    eye = jnp.where(row == tcol, 1.0, 0.0)
    bdb = lambda x: _block_diag(x).astype(BF16)

    cums, tots, Mbs, Mks, ARSs = [], [], [], [], []
    for i in idx:
        cm = lws[i]
        sh = 1
        while sh < C:
            cm = cm + jnp.where(row >= sh, pltpu.roll(cm, sh, axis=0), 0.0)
            sh *= 2
        e_neg = jnp.exp(-cm)
        AR = jnp.concatenate([-kks[i] * jnp.exp(cm - lws[i]), rs[i] * jnp.exp(cm)], axis=0).astype(BF16)
        cums.append(cm)
        tots.append(cm[C - 1:C, :])
        Mbk = _dot_nt(AR, jnp.concatenate([bdb(bs[i] * e_neg), bdb(k2s[i] * e_neg)], axis=0))
        Mbs.append(Mbk[:, 0:GL])
        Mks.append(Mbk[:, GL:2 * GL])
        ARSs.append(_dot_nt(AR, sts[i].astype(BF16)))
    vbds = [bdb(v) for v in vs]

    Ls = [jnp.where(strict, Mbs[i][0:C], 0.0) for i in idx]
    Ps = [eye + L for L in Ls]
    Lps = [_dot(L.astype(BF16), bdb(L)) for L in Ls]
    Xs = [ARSs[i][0:C] + _dot(jnp.where(strict, Mks[i][0:C], 0.0).astype(BF16), vbds[i]) for i in idx]
    m = 2
    while 2 * m < C:
        both = [_dot(jnp.concatenate([Lps[i], Ps[i]], axis=0).astype(BF16), bdb(Lps[i])) for i in idx]
        Ps = [Ps[i] + both[i][C:2 * C] for i in idx]
        Lps = [bt[0:C] for bt in both]
        m *= 2
    Ps = [Ps[i] + _dot(Ps[i].astype(BF16), bdb(Lps[i])) for i in idx]

    Us = [_dot(Ps[i].astype(BF16), bdb(Xs[i])) for i in idx]
    Ys = [ARSs[i][C:2 * C]
          + _dot(jnp.concatenate([jnp.where(incl, Mbs[i][C:2 * C], 0.0),
                                  jnp.where(incl, Mks[i][C:2 * C], 0.0)], axis=1).astype(BF16),
                 jnp.concatenate([bdb(Us[i]), vbds[i]], axis=0)) for i in idx]
    same_head = _same_head(GL)
    st_news = []
    for i in idx:
        e_rem = jnp.exp(tots[i] - cums[i])
        UV = jnp.concatenate([Us[i], vs[i]], axis=0).astype(BF16)
        BK = jnp.concatenate([bs[i] * e_rem, k2s[i] * e_rem], axis=0).astype(BF16)
        st_news.append(jnp.where(same_head, sts[i] * jnp.exp(tots[i]) + _dot_tn(UV, BK), 0.0))
    return Ys, st_news


def _rwkv_chunk_kernel(r_ref, lw_ref, k2_ref, v_ref, kk_ref, b_ref, g_ref, rk_ref, gng_ref, gnb_ref,
                       *rest, nbatch, ngroups, n_ride):
    ride_in, o_ref, ride_out, st_ref = rest[:n_ride], rest[n_ride], rest[n_ride + 1:-1], rest[-1]
    for w_ref, wb_ref in zip(ride_in, ride_out):
        wb_ref[...] = w_ref[...].astype(BF16)
    c = pl.program_id(2)

    @pl.when(c == 0)
    def _():
        st_ref[...] = jnp.zeros_like(st_ref)

    GL = st_ref.shape[1]
    ones = _head_ones(GL)
    chains = [(bi, p) for bi in range(nbatch) for p in range(ngroups)]
    sl = lambda p: slice(p * GL, (p + 1) * GL)
    get = lambda ref: [ref[bi, :, sl(p)] for bi, p in chains]
    rs, k2s, vs = get(r_ref), get(k2_ref), get(v_ref)
    ys, st_news = _chunk_step(rs, get(lw_ref), k2s, vs, get(kk_ref), get(b_ref),
                              [st_ref[i] for i in range(len(chains))])
    for i in range(len(chains)):
        st_ref[i] = st_news[i]
    ds = [y - _head_sum(y, ones) * (1.0 / HEAD) for y in ys]
    vars_ = [_head_sum(d * d, ones) * (1.0 / HEAD) for d in ds]
    for i, (bi, p) in enumerate(chains):
        yn = ds[i] * lax.rsqrt(vars_[i] + GN_EPS) * gng_ref[:, sl(p)] + gnb_ref[:, sl(p)]
        bonus = _head_sum(rs[i] * k2s[i] * rk_ref[:, sl(p)], ones) * vs[i]
        o_ref[bi, :, sl(p)] = ((yn + bonus) * g_ref[bi, :, sl(p)]).astype(o_ref.dtype)


def rwkv_chunk(r, lw, k2, v, kk, b, g, r_k, gn_g, gn_b, *, ride_along=(), nbatch=4, ngroups=8):
    B, S, W = r.shape
    GL = min(GROUP * HEAD, W)
    ngroups = min(ngroups, W // GL)
    nbatch = min(nbatch, B)
    wb = ngroups * GL
    grid = (B // nbatch, W // wb, S // CHUNK)
    n_steps = grid[0] * grid[1] * grid[2]
    seq = pl.BlockSpec((nbatch, CHUNK, wb), lambda bi, p, c: (bi, c, p))
    par = pl.BlockSpec((1, wb), lambda bi, p, c: (0, p))
    rides = [w for w in ride_along if w.shape[0] % (n_steps * BF16_ROWS) == 0]
    slab = lambda w: pl.BlockSpec((w.shape[0] // n_steps, w.shape[1]),
                                  lambda bi, p, c: ((bi * grid[1] + p) * grid[2] + c, 0))
    outs = pl.pallas_call(
        functools.partial(_rwkv_chunk_kernel, nbatch=nbatch, ngroups=ngroups, n_ride=len(rides)),
        grid=grid,
        in_specs=[seq] * 7 + [par] * 3 + [slab(w) for w in rides],
        out_specs=[seq] + [slab(w) for w in rides],
        out_shape=[jax.ShapeDtypeStruct((B, S, W), BF16)] + [jax.ShapeDtypeStruct(w.shape, BF16) for w in rides],
        scratch_shapes=[pltpu.VMEM((nbatch * ngroups, GL, GL), F32)],
        compiler_params=_cparams(("parallel", "parallel", "arbitrary")),
    )(r, lw, k2, v, kk, b, g, r_k, gn_g, gn_b, *rides)
    cast = iter(outs[1:])
    return outs[0], tuple(next(cast) if any(w is q for q in rides) else w.astype(BF16) for w in ride_along)


def _out_ln_kernel(x_ref, oa_ref, or_ref, wa_ref, wr_ref, g_ref, b_ref, o_ref, *, alpha):
    half = x_ref.shape[0] // 2
    rows = [slice(0, half), slice(half, 2 * half)]
    mixes = [_dot(oa_ref[r, :], wa_ref[...]) + _dot(or_ref[r, :], wr_ref[...]) for r in rows]
    for r, mix in zip(rows, mixes):
        o_ref[r, :] = _layer_norm(alpha * x_ref[r, :] + mix, g_ref[...], b_ref[...])


def out_ln(x, oa, orw, w, g, b, *, alpha, tm=512):
    T, D = x.shape
    tm = min(tm, T)
    ka, kr = oa.shape[1], orw.shape[1]
    assert ka % kr == 0 and w.shape[0] == ka + kr
    return pl.pallas_call(
        functools.partial(_out_ln_kernel, alpha=alpha),
        grid=(T // tm,),
        in_specs=[
            pl.BlockSpec((tm, D), lambda i: (i, 0)),
            pl.BlockSpec((tm, ka), lambda i: (i, 0)),
            pl.BlockSpec((tm, kr), lambda i: (i, 0)),
            pl.BlockSpec((ka, D), lambda i: (0, 0)),
            pl.BlockSpec((kr, D), lambda i: (ka // kr, 0)),
            pl.BlockSpec((1, D), lambda i: (0, 0)),
            pl.BlockSpec((1, D), lambda i: (0, 0)),
        ],
        out_specs=pl.BlockSpec((tm, D), lambda i: (i, 0)),
        out_shape=jax.ShapeDtypeStruct((T, D), F32),
        compiler_params=_cparams(("parallel",)),
    )(x, oa, orw, w, w, g, b)


def _layer(x, l, ffn1_w_gate, ffn1_w_up, ffn1_w_down, ln1_g, ln1_b, w_in,
           lambda_q1, lambda_k1, lambda_q2, lambda_k2, attn_norm_g,
           rwkv_mu, rwkv_w0, rwkv_w2, rwkv_a0, rwkv_a2, rwkv_g2,
           rwkv_k_k, rwkv_k_a, rwkv_r_k, rwkv_gn_g, rwkv_gn_b,
           w_out, ln2_g, ln2_b, ffn2_w_gate, ffn2_w_up, ffn2_w_down, ln3_g, ln3_b):
    B, S, D = x.shape
    T = B * S
    alpha = (2.0 * DEPTH) ** 0.25
    lambda_init = 0.8 - 0.6 * math.exp(-0.3 * l)
    rw = rwkv_w0.shape[-1]
    aw = w_out.shape[1] - rw
    H = aw // LANES
    n_wd, n_ad, n_gd = rwkv_w2.shape[1], rwkv_a2.shape[1], rwkv_g2.shape[1]
    assert n_wd == HEAD and n_ad == HEAD and n_gd <= 2 * LANES
    row = lambda a: a.reshape(1, -1)
    bf = lambda a: a.astype(BF16)

    x1 = ffn_ln_piped(x.reshape(T, D), ffn1_w_gate[l], ffn1_w_up[l], ffn1_w_down[l],
                      row(ln1_g[l]), row(ln1_b[l]), alpha=alpha)

    wi = jnp.swapaxes(w_in[l], 0, 1)
    p_attn = proj_attn(x1, wi, n_cols=3 * aw, n_scaled_cols=aw, scale=HEAD ** -0.5 * math.log2(math.e))
    tab, feat = alibi_tables(H, S)
    o_attn = diff_attn(p_attn, tab, feat, row(lambda_q1[l]), row(lambda_k1[l]), row(lambda_q2[l]),
                       row(lambda_k2[l]), row(attn_norm_g[l]), B=B, S=S, H=H, lambda_init=lambda_init)

    mu = jnp.pad(rwkv_mu[l], (0, RWKV_TAIL - (n_wd + n_ad + n_gd)))
    w12 = jnp.zeros((LANES, 2 * rw), F32)
    w12 = w12.at[:HEAD, :rw].set(rwkv_w2[l]).at[HEAD:, rw:].set(rwkv_a2[l])
    g2p = jnp.pad(rwkv_g2[l], ((0, 2 * LANES - n_gd), (0, 0)))
    seqs = proj_rwkv(x1, wi, row(mu), row(rwkv_w0[l]), row(rwkv_a0[l]), row(rwkv_k_k[l]), row(rwkv_k_a[l]),
                     bf(w12), bf(g2p), S=S, col0=3 * aw, width=rw)
    r, lw, k2, v, kk, b, g = [t.reshape(B, S, rw) for t in seqs]
    o_rwkv, later_w = rwkv_chunk(r, lw, k2, v, kk, b, g, row(rwkv_r_k[l]), row(rwkv_gn_g[l]), row(rwkv_gn_b[l]),
                                 ride_along=(w_out[l], ffn2_w_gate[l], ffn2_w_up[l], ffn2_w_down[l]))

    x2 = out_ln(x1, o_attn, o_rwkv.reshape(T, rw), later_w[0], row(ln2_g[l]), row(ln2_b[l]), alpha=alpha)
    x3 = ffn_ln(x2, *later_w[1:], row(ln3_g[l]), row(ln3_b[l]), alpha=alpha, tf=2 * FFN_TF)
    return x3.reshape(B, S, D)


def kernel(x, ffn1_w_gate, ffn1_w_up, ffn1_w_down, ln1_g, ln1_b, w_in, lambda_q1, lambda_k1, lambda_q2, lambda_k2, attn_norm_g, rwkv_mu, rwkv_w0, rwkv_w2, rwkv_a0, rwkv_a2, rwkv_g2, rwkv_k_k, rwkv_k_a, rwkv_r_k, rwkv_gn_g, rwkv_gn_b, w_out, ln2_g, ln2_b, ffn2_w_gate, ffn2_w_up, ffn2_w_down, ln3_g, ln3_b):
    for l in range(DEPTH):
        x = _layer(x, l, ffn1_w_gate, ffn1_w_up, ffn1_w_down, ln1_g, ln1_b, w_in,
                   lambda_q1, lambda_k1, lambda_q2, lambda_k2, attn_norm_g,
                   rwkv_mu, rwkv_w0, rwkv_w2, rwkv_a0, rwkv_a2, rwkv_g2,
                   rwkv_k_k, rwkv_k_a, rwkv_r_k, rwkv_gn_g, rwkv_gn_b,
                   w_out, ln2_g, ln2_b, ffn2_w_gate, ffn2_w_up, ffn2_w_down, ln3_g, ln3_b)
    return x
```

```python
import functools
import math

import jax
import jax.numpy as jnp
from jax import lax
from jax.experimental import pallas as pl
from jax.experimental.pallas import tpu as pltpu

F32 = jnp.float32
BF16 = jnp.bfloat16

DEPTH = 1
LN_EPS = 1e-5
ATTN_NORM_EPS = 1e-5
GN_EPS = 64e-5
HEAD = 64
LANES = 128
SUBLANES = 8
CHUNK = 64
VMEM_LIMIT = 56 * 1024 * 1024
NEG_BIG = -1e30


def _cparams(sem):
    return pltpu.CompilerParams(dimension_semantics=sem, vmem_limit_bytes=VMEM_LIMIT)


def _layer_norm(y, g, b):
    mu = jnp.mean(y, axis=-1, keepdims=True)
    d = y - mu
    var = jnp.mean(d * d, axis=-1, keepdims=True)
    return d * lax.rsqrt(var + LN_EPS) * g + b


def _dot(a, b):
    return jnp.dot(a, b, preferred_element_type=F32)


def _dot_nt(a, b):
    return lax.dot_general(a, b, (((1,), (1,)), ((), ())), preferred_element_type=F32)


def _dot_tn(a, b):
    return lax.dot_general(a, b, (((0,), (0,)), ((), ())), preferred_element_type=F32)


def _ffn_ln_kernel(x_ref, wg_ref, wu_ref, wd_ref, g_ref, b_ref, o_ref, xb_ref, *, alpha, ncol):
    j = pl.program_id(1)
    nj = pl.num_programs(1)
    tm, D = o_ref.shape
    cols = [slice(n * ncol, (n + 1) * ncol) for n in range(D // ncol)]

    def body(first, final):
        nrow = 2 if (first or final) else 1
        rows = [slice(r * tm // nrow, (r + 1) * tm // nrow) for r in range(nrow)]
        if first:
            for r in rows:
                x = x_ref[r, :]
                xb_ref[r, :] = x.astype(BF16)
                o_ref[r, :] = alpha * x
        wg = wg_ref[...].astype(BF16)
        wu = wu_ref[...].astype(BF16)
        wd = [wd_ref[:, c].astype(BF16) for c in cols]
        hs = []
        for r in rows:
            xb = xb_ref[r, :]
            hg = _dot(xb, wg)
            hu = _dot(xb, wu)
            hs.append((0.5 * hg * jax.nn.sigmoid(hg) * hu).astype(BF16))
        for r, h in zip(rows, hs):
            for c, w in zip(cols, wd):
                o_ref[r, c] += _dot(h, w)
        if final:
            for r in rows:
                o_ref[r, :] = _layer_norm(o_ref[r, :], g_ref[...], b_ref[...])

    pl.when(j == 0)(lambda: body(True, False))
    pl.when(jnp.logical_and(j > 0, j < nj - 1))(lambda: body(False, False))
    pl.when(jnp.logical_and(j > 0, j == nj - 1))(lambda: body(False, True))


FFN_TF = 256


def ffn_ln(x, wg, wu, wd, g, b, *, alpha, tm=1024, tf=FFN_TF, ncol=512):
    T, D = x.shape
    Fd = wg.shape[1]
    tm = min(tm, T)
    tf = min(tf, Fd // 2)
    ncol = min(ncol, D)
    return pl.pallas_call(
        functools.partial(_ffn_ln_kernel, alpha=alpha, ncol=ncol),
        grid=(T // tm, Fd // tf),
        in_specs=[
            pl.BlockSpec((tm, D), lambda i, j: (i, 0)),
            pl.BlockSpec((D, tf), lambda i, j: (0, j)),
            pl.BlockSpec((D, tf), lambda i, j: (0, j)),
            pl.BlockSpec((tf, D), lambda i, j: (j, 0)),
            pl.BlockSpec((1, D), lambda i, j: (0, 0)),
            pl.BlockSpec((1, D), lambda i, j: (0, 0)),
        ],
        out_specs=pl.BlockSpec((tm, D), lambda i, j: (i, 0)),
        out_shape=jax.ShapeDtypeStruct((T, D), F32),
        scratch_shapes=[pltpu.VMEM((tm, D), BF16)],
        compiler_params=_cparams(("parallel", "arbitrary")),
    )(x, wg, wu, wd, g, b)


def _pick_tile(limit, *sizes):
    t = limit // LANES * LANES
    while any(s % t for s in sizes):
        t -= LANES
    return t


def _stage_weight_tile(wt_ref, wb_ref, s, first_tile, n_cols):
    tile = wt_ref.shape[0]
    col = (first_tile + s) * tile + lax.broadcasted_iota(jnp.int32, wt_ref.shape, 0)
    wb_ref[pl.ds(pl.multiple_of(s * tile, tile), tile), :] = jnp.where(col < n_cols, wt_ref[...], 0.0).astype(BF16)


def _proj_attn_kernel(x_ref, wt_ref, o_ref, wb_ref, *, n_wt, n_cols, n_scaled, scale):
    s = pl.program_id(0)

    @pl.when(s < n_wt)
    def _():
        _stage_weight_tile(wt_ref, wb_ref, s, 0, n_cols)

    @pl.when(s >= n_wt)
    def _():
        tile = wt_ref.shape[0]
        xb = x_ref[...].astype(BF16)
        for n in range(n_wt):
            acc = _dot_nt(xb, wb_ref[n * tile:(n + 1) * tile, :])
            if n < n_scaled:
                acc = acc * scale
            o_ref[:, n * tile:(n + 1) * tile] = acc.astype(o_ref.dtype)


def proj_attn(x, wt, *, n_cols, n_scaled_cols, scale, tm=1024, tile=512):
    T, D = x.shape
    tm = min(tm, T)
    tile = _pick_tile(tile, n_cols, n_scaled_cols)
    n_wt = n_cols // tile
    tok = lambda s: (jnp.maximum(s - n_wt, 0), 0)
    return pl.pallas_call(
        functools.partial(_proj_attn_kernel, n_wt=n_wt, n_cols=wt.shape[0],
                          n_scaled=n_scaled_cols // tile, scale=scale),
        grid=(n_wt + T // tm,),
        in_specs=[
            pl.BlockSpec((tm, D), tok),
            pl.BlockSpec((tile, D), lambda s: (jnp.minimum(s, n_wt - 1), 0)),
        ],
        out_specs=pl.BlockSpec((tm, n_cols), tok),
        out_shape=jax.ShapeDtypeStruct((T, n_cols), BF16),
        scratch_shapes=[pltpu.VMEM((n_cols, D), BF16)],
        compiler_params=_cparams(("arbitrary",)),
    )(x, wt)


ATT_TQ = 256
ATT_TK = 512
N_BIAS_TERMS = 3
POS_RADIX = 64
BF16_ROWS = 16
VT_ROWS = LANES + BF16_ROWS


def alibi_tables(H, S):
    slope = jnp.exp2(-8.0 * (jnp.arange(H, dtype=F32) + 1.0) / H) * math.log2(math.e)
    terms, rest = [], slope
    for _ in range(N_BIAS_TERMS):
        t = rest.astype(BF16).astype(F32)
        terms += [POS_RADIX * t, t]
        rest = rest - t
    tab = jnp.stack(terms, axis=1)
    pos = jnp.arange(S, dtype=jnp.int32)
    hi, lo = (pos // POS_RADIX).astype(F32), (pos % POS_RADIX).astype(F32)
    feat = jnp.stack([hi, lo] * N_BIAS_TERMS, axis=1)
    feat = jnp.pad(feat, ((0, 0), (0, LANES - feat.shape[1]))).astype(BF16)
    return tab, feat


def _diff_attn_kernel(tab_ref, q_ref, k_ref, v_ref, feat_ref, lq1_ref, lk1_ref, lq2_ref, lk2_ref, ng_ref,
                      o_ref, ka_ref, vt_ref, qa_ref, acc_ref, m_ref, *, hb, nck, lambda_init):
    tq = ATT_TQ
    tk = vt_ref.shape[2]
    g = pl.program_id(1)
    i = pl.program_id(2)
    heads = range(hb)
    hs = lambda h: slice(h * LANES, (h + 1) * LANES)

    @pl.when(i == 0)
    def _():
        for h in heads:
            ka_ref[h, :, 0:LANES] = k_ref[:, hs(h)]
            ka_ref[h, :, LANES:2 * LANES] = feat_ref[...]
            for c in range(nck):
                vt_ref[h * nck + c, 0:LANES, :] = v_ref[c * tk:(c + 1) * tk, hs(h)].astype(F32).T.astype(BF16)
                vt_ref[h * nck + c, LANES:VT_ROWS, :] = jnp.ones((BF16_ROWS, tk), BF16)

    lane = lax.broadcasted_iota(jnp.int32, (tq, LANES), 1)
    for h in heads:
        q = q_ref[:, hs(h)]
        zero = jnp.zeros_like(q)
        qf = jnp.zeros((tq, LANES), F32)
        for n in range(2 * N_BIAS_TERMS):
            qf = jnp.where(lane == n, tab_ref[g * hb + h, n], qf)
        qf = qf.astype(BF16)
        qa_ref[h, 0:tq, 0:LANES] = jnp.where(lane < HEAD, q, zero)
        qa_ref[h, tq:2 * tq, 0:LANES] = jnp.where(lane >= HEAD, q, zero)
        qa_ref[h, 0:tq, LANES:2 * LANES] = qf
        qa_ref[h, tq:2 * tq, LANES:2 * LANES] = qf
        m_ref[h] = jnp.full((1, 2 * tq), NEG_BIG, F32)
        acc_ref[h] = jnp.zeros((VT_ROWS, 2 * tq), F32)

    def chunk(c, part=None):
        start = pl.multiple_of(c * tk, tk)
        nk, lanes = tk, slice(None)
        if part is not None:
            start, nk, lanes = start + part * tq, tq, slice(part * tq, (part + 1) * tq)
        ss = [_dot_nt(ka_ref[h, pl.ds(start, nk), :], qa_ref[h]) for h in heads]
        if part is not None:
            key = start + lax.broadcasted_iota(jnp.int32, (nk, 2 * tq), 0)
            qcol = lax.broadcasted_iota(jnp.int32, (nk, 2 * tq), 1)
            qpos = i * tq + jnp.where(qcol >= tq, qcol - tq, qcol)
            keep = key <= qpos
            ss = [jnp.where(keep, s, -jnp.inf) for s in ss]
        m_olds = [m_ref[h] for h in heads]
        m_news = [jnp.maximum(m_olds[h], jnp.max(ss[h], axis=0, keepdims=True)) for h in heads]
        ps = [jnp.exp2(ss[h] - m_news[h]).astype(BF16) for h in heads]
        alphas = [jnp.exp2(m_olds[h] - m_news[h]) for h in heads]
        pvs = [_dot(vt_ref[h * nck + c, :, lanes], ps[h]) for h in heads]
        for h in heads:
            m_ref[h] = m_news[h]
            acc_ref[h] = alphas[h] * acc_ref[h] + pvs[h]

    ratio = tk // tq
    n_full = i // ratio

    def body(c, carry):
        chunk(c)
        return carry

    lax.fori_loop(0, n_full, body, 0)
    chunk(n_full, 0)
    for part in range(1, ratio):
        pl.when(i % ratio >= part)(functools.partial(chunk, n_full, part))

    lam = (jnp.exp(jnp.sum(lq1_ref[...] * lk1_ref[...], axis=-1, keepdims=True))
           - jnp.exp(jnp.sum(lq2_ref[...] * lk2_ref[...], axis=-1, keepdims=True)) + lambda_init)
    for h in heads:
        acc = acc_ref[h]
        rl = 1.0 / acc[LANES:LANES + 1, :]
        ot = (acc[0:LANES, 0:tq] * rl[:, 0:tq]
              - lam * (acc[0:LANES, tq:2 * tq] * rl[:, tq:2 * tq]))
        o = ot.T
        o = o * lax.rsqrt(jnp.mean(o * o, axis=-1, keepdims=True) + ATTN_NORM_EPS) * ng_ref[...]
        o_ref[:, hs(h)] = (o * (1.0 - lambda_init)).astype(o_ref.dtype)


def diff_attn(pa, tab, feat, lq1, lk1, lq2, lk2, norm_g, *, B, S, H, lambda_init, hb=8):
    hb = min(hb, H)
    tq, tk = ATT_TQ, min(ATT_TK, S)
    assert S % tk == 0 and H % hb == 0 and tk % tq == 0
    nq, nck, ng = S // tq, S // tk, H // hb
    wb = hb * LANES
    small = lambda n: pl.BlockSpec((1, n), lambda b, g, i: (0, 0))
    return pl.pallas_call(
        functools.partial(_diff_attn_kernel, hb=hb, nck=nck, lambda_init=lambda_init),
        grid=(B, ng, nq),
        in_specs=[
            pl.BlockSpec(memory_space=pltpu.SMEM),
            pl.BlockSpec((tq, wb), lambda b, g, i: (b * nq + i, g)),
            pl.BlockSpec((S, wb), lambda b, g, i: (b, ng + g)),
            pl.BlockSpec((S, wb), lambda b, g, i: (b, 2 * ng + g)),
            pl.BlockSpec((S, LANES), lambda b, g, i: (0, 0)),
            small(HEAD), small(HEAD), small(HEAD), small(HEAD), small(2 * HEAD),
        ],
        out_specs=pl.BlockSpec((tq, wb), lambda b, g, i: (b * nq + i, g)),
        out_shape=jax.ShapeDtypeStruct((B * S, H * LANES), BF16),
        scratch_shapes=[
            pltpu.VMEM((hb, S, 2 * LANES), BF16),
            pltpu.VMEM((hb * nck, VT_ROWS, tk), BF16),
            pltpu.VMEM((hb, 2 * tq, 2 * LANES), BF16),
            pltpu.VMEM((hb, VT_ROWS, 2 * tq), F32),
            pltpu.VMEM((hb, 1, 2 * tq), F32),
        ],
        compiler_params=_cparams(("parallel", "parallel", "arbitrary")),
    )(tab, pa, pa, pa, feat, lq1, lk1, lq2, lk2, norm_g)


def _same_head(n):
    r = lax.broadcasted_iota(jnp.int32, (n, n), 0) // HEAD
    c = lax.broadcasted_iota(jnp.int32, (n, n), 1) // HEAD
    return r == c


def _head_ones(n):
    return jnp.where(_same_head(n), 1.0, 0.0).astype(BF16)


def _head_sum(x, ones):
    return _dot(x.astype(BF16), ones)


RWKV_TAIL = 3 * LANES


def _proj_rwkv_kernel(x_ref, wt_ref, mu_ref, w0_ref, a0_ref, kk_ref, ka_ref, w12_ref, g2_ref,
                      r_ref, lw_ref, k2_ref, v_ref, kkn_ref, b_ref, g_ref, wb_ref, pr_ref, prev_ref,
                      *, n_wt, first_tile, n_cols, width, tiles_per_seq):
    s = pl.program_id(0)

    @pl.when(s == 0)
    def _():
        prev_ref[...] = jnp.zeros_like(prev_ref)

    @pl.when(s < n_wt)
    def _():
        _stage_weight_tile(wt_ref, wb_ref, s, first_tile, n_cols)

    @pl.when(s >= n_wt)
    def _():
        tm = x_ref.shape[0]
        xb = x_ref[...].astype(BF16)
        c3 = 3 * width
        first = (s - n_wt) % tiles_per_seq == 0
        row = lax.broadcasted_iota(jnp.int32, (tm, 1), 0)

        def project(c0, c1):
            pr_ref[:, c0:c1] = _dot_nt(xb, wb_ref[c0:c1, :])

        def mixed(c0, c1):
            x = pr_ref[:, c0:c1]
            prow = jnp.where(first, 0.0, prev_ref[0:1, c0:c1])
            xs = jnp.where(row == 0, prow, pltpu.roll(x, 1, axis=0))
            return x + mu_ref[:, c0:c1] * (xs - x)

        project(c3, c3 + RWKV_TAIL)
        project(width, 2 * width)

        lora_in = mixed(c3, c3 + LANES)
        lane = lax.broadcasted_iota(jnp.int32, lora_in.shape, 1)
        lora_in = jnp.where(lane < HEAD, jnp.tanh(lora_in), lora_in)
        z = _dot(lora_in.astype(BF16), w12_ref[...])
        g_ref[...] = _dot(jax.nn.sigmoid(mixed(c3 + LANES, c3 + RWKV_TAIL)).astype(BF16), g2_ref[...])
        lw_ref[...] = -math.exp(-0.5) * jax.nn.sigmoid(w0_ref[...] + z[:, 0:width])
        a = jax.nn.sigmoid(a0_ref[...] + z[:, width:2 * width])

        project(0, width)

        k = mixed(width, 2 * width)
        k2_ref[...] = k * (1.0 + (a - 1.0) * ka_ref[...])
        kx = k * kk_ref[...]
        gw = min(2 * LANES, width)
        ones = _head_ones(gw)
        ss = jnp.concatenate(
            [_head_sum(kx[:, t * gw:(t + 1) * gw] * kx[:, t * gw:(t + 1) * gw], ones)
             for t in range(width // gw)], axis=1)
        kkn = kx / jnp.maximum(jnp.sqrt(ss), 1e-12)
        kkn_ref[...] = kkn
        b_ref[...] = kkn * a

        project(2 * width, c3)

        r_ref[...] = mixed(0, width)
        v_ref[...] = mixed(2 * width, c3)
        prev_ref[0:1, :] = pr_ref[tm - 1:tm, :]


def proj_rwkv(x, wt, mu, w0, a0, k_k, k_a, w12, g2p, *, S, col0, width, tm=256, tile=384):
    T, D = x.shape
    tm = min(tm, S)
    W = 3 * width + RWKV_TAIL
    tile = _pick_tile(tile, col0, W)
    assert S % tm == 0
    n_wt, first_tile = W // tile, col0 // tile
    tok = lambda s: (jnp.maximum(s - n_wt, 0), 0)
    row = lambda n: pl.BlockSpec((1, n), lambda s: (0, 0))
    out = pl.BlockSpec((tm, width), tok)
    return pl.pallas_call(
        functools.partial(_proj_rwkv_kernel, n_wt=n_wt, first_tile=first_tile, n_cols=wt.shape[0],
                          width=width, tiles_per_seq=S // tm),
        grid=(n_wt + T // tm,),
        in_specs=[
            pl.BlockSpec((tm, D), tok),
            pl.BlockSpec((tile, D), lambda s: (first_tile + jnp.minimum(s, n_wt - 1), 0)),
            row(W), row(width), row(width), row(width), row(width),
            pl.BlockSpec((LANES, 2 * width), lambda s: (0, 0)),
            pl.BlockSpec((2 * LANES, width), lambda s: (0, 0)),
        ],
        out_specs=[out] * 7,
        out_shape=[jax.ShapeDtypeStruct((T, width), F32)] * 7,
        scratch_shapes=[pltpu.VMEM((W, D), BF16), pltpu.VMEM((tm, W), F32), pltpu.VMEM((SUBLANES, W), F32)],
        compiler_params=_cparams(("arbitrary",)),
    )(x, wt, mu, w0, a0, k_k, k_a, w12, g2p)


GROUP = 2


def _block_diag(x):
    head = lax.broadcasted_iota(jnp.int32, x.shape, 1) // HEAD
    zero = jnp.zeros_like(x)
    return jnp.concatenate([jnp.where(head == h, x, zero) for h in range(x.shape[1] // HEAD)], axis=0)


def _chunk_step(rs, lws, k2s, vs, kks, bs, sts):
    C = CHUNK
    GL = rs[0].shape[1]
    idx = range(len(rs))
    row = lax.broadcasted_iota(jnp.int32, (C, GL), 0)
    tcol = lax.broadcasted_iota(jnp.int32, (C, GL), 1) % HEAD
    strict = row > tcol
    incl = row >= tcol
    eye = jnp.where(row == tcol, 1.0, 0.0)
    bdb = lambda x: _block_diag(x).astype(BF16)

    cums, tots, Mbs, Mks, ARSs = [], [], [], [], []
    for i in idx:
        cm = lws[i]
        sh = 1
        while sh < C:
            cm = cm + jnp.where(row >= sh, pltpu.roll(cm, sh, axis=0), 0.0)
            sh *= 2
        e_neg = jnp.exp(-cm)
        AR = jnp.concatenate([-kks[i] * jnp.exp(cm - lws[i]), rs[i] * jnp.exp(cm)], axis=0).astype(BF16)
        cums.append(cm)
        tots.append(cm[C - 1:C, :])
        Mbk = _dot_nt(AR, jnp.concatenate([bdb(bs[i] * e_neg), bdb(k2s[i] * e_neg)], axis=0))
        Mbs.append(Mbk[:, 0:GL])
        Mks.append(Mbk[:, GL:2 * GL])
        ARSs.append(_dot_nt(AR, sts[i].astype(BF16)))
    vbds = [bdb(v) for v in vs]

    Ls = [jnp.where(strict, Mbs[i][0:C], 0.0) for i in idx]
    Ps = [eye + L for L in Ls]
    Lps = [_dot(L.astype(BF16), bdb(L)) for L in Ls]
    Xs = [ARSs[i][0:C] + _dot(jnp.where(strict, Mks[i][0:C], 0.0).astype(BF16), vbds[i]) for i in idx]
    m = 2
    while 2 * m < C:
        both = [_dot(jnp.concatenate([Lps[i], Ps[i]], axis=0).astype(BF16), bdb(Lps[i])) for i in idx]
        Ps = [Ps[i] + both[i][C:2 * C] for i in idx]
        Lps = [bt[0:C] for bt in both]
        m *= 2
    Ps = [Ps[i] + _dot(Ps[i].astype(BF16), bdb(Lps[i])) for i in idx]

    Us = [_dot(Ps[i].astype(BF16), bdb(Xs[i])) for i in idx]
    Ys = [ARSs[i][C:2 * C]
          + _dot(jnp.concatenate([jnp.where(incl, Mbs[i][C:2 * C], 0.0),
                                  jnp.where(incl, Mks[i][C:2 * C], 0.0)], axis=1).astype(BF16),
                 jnp.concatenate([bdb(Us[i]), vbds[i]], axis=0)) for i in idx]
    same_head = _same_head(GL)
    st_news = []
    for i in idx:
        e_rem = jnp.exp(tots[i] - cums[i])
        UV = jnp.concatenate([Us[i], vs[i]], axis=0).astype(BF16)
        BK = jnp.concatenate([bs[i] * e_rem, k2s[i] * e_rem], axis=0).astype(BF16)
        st_news.append(jnp.where(same_head, sts[i] * jnp.exp(tots[i]) + _dot_tn(UV, BK), 0.0))
    return Ys, st_news


def _rwkv_chunk_kernel(r_ref, lw_ref, k2_ref, v_ref, kk_ref, b_ref, g_ref, rk_ref, gng_ref, gnb_ref,
                       *rest, nbatch, ngroups, n_ride):
    ride_in, o_ref, ride_out, st_ref = rest[:n_ride], rest[n_ride], rest[n_ride + 1:-1], rest[-1]
    for w_ref, wb_ref in zip(ride_in, ride_out):
        wb_ref[...] = w_ref[...].astype(BF16)
    c = pl.program_id(2)

    @pl.when(c == 0)
    def _():
        st_ref[...] = jnp.zeros_like(st_ref)

    GL = st_ref.shape[1]
    ones = _head_ones(GL)
    chains = [(bi, p) for bi in range(nbatch) for p in range(ngroups)]
    sl = lambda p: slice(p * GL, (p + 1) * GL)
    get = lambda ref: [ref[bi, :, sl(p)] for bi, p in chains]
    rs, k2s, vs = get(r_ref), get(k2_ref), get(v_ref)
    ys, st_news = _chunk_step(rs, get(lw_ref), k2s, vs, get(kk_ref), get(b_ref),
                              [st_ref[i] for i in range(len(chains))])
    for i in range(len(chains)):
        st_ref[i] = st_news[i]
    ds = [y - _head_sum(y, ones) * (1.0 / HEAD) for y in ys]
    vars_ = [_head_sum(d * d, ones) * (1.0 / HEAD) for d in ds]
    for i, (bi, p) in enumerate(chains):
        yn = ds[i] * lax.rsqrt(vars_[i] + GN_EPS) * gng_ref[:, sl(p)] + gnb_ref[:, sl(p)]
        bonus = _head_sum(rs[i] * k2s[i] * rk_ref[:, sl(p)], ones) * vs[i]
        o_ref[bi, :, sl(p)] = ((yn + bonus) * g_ref[bi, :, sl(p)]).astype(o_ref.dtype)


def rwkv_chunk(r, lw, k2, v, kk, b, g, r_k, gn_g, gn_b, *, ride_along=(), nbatch=4, ngroups=8):
    B, S, W = r.shape
    GL = min(GROUP * HEAD, W)
    ngroups = min(ngroups, W // GL)
    nbatch = min(nbatch, B)
    wb = ngroups * GL
    grid = (B // nbatch, W // wb, S // CHUNK)
    n_steps = grid[0] * grid[1] * grid[2]
    seq = pl.BlockSpec((nbatch, CHUNK, wb), lambda bi, p, c: (bi, c, p))
    par = pl.BlockSpec((1, wb), lambda bi, p, c: (0, p))
    rides = [w for w in ride_along if w.shape[0] % (n_steps * BF16_ROWS) == 0]
    slab = lambda w: pl.BlockSpec((w.shape[0] // n_steps, w.shape[1]),
                                  lambda bi, p, c: ((bi * grid[1] + p) * grid[2] + c, 0))
    outs = pl.pallas_call(
        functools.partial(_rwkv_chunk_kernel, nbatch=nbatch, ngroups=ngroups, n_ride=len(rides)),
        grid=grid,
        in_specs=[seq] * 7 + [par] * 3 + [slab(w) for w in rides],
        out_specs=[seq] + [slab(w) for w in rides],
        out_shape=[jax.ShapeDtypeStruct((B, S, W), BF16)] + [jax.ShapeDtypeStruct(w.shape, BF16) for w in rides],
        scratch_shapes=[pltpu.VMEM((nbatch * ngroups, GL, GL), F32)],
        compiler_params=_cparams(("parallel", "parallel", "arbitrary")),
    )(r, lw, k2, v, kk, b, g, r_k, gn_g, gn_b, *rides)
    cast = iter(outs[1:])
    return outs[0], tuple(next(cast) if any(w is q for q in rides) else w.astype(BF16) for w in ride_along)


def _out_ln_kernel(x_ref, oa_ref, or_ref, wa_ref, wr_ref, g_ref, b_ref, o_ref, *, alpha):
    half = x_ref.shape[0] // 2
    rows = [slice(0, half), slice(half, 2 * half)]
    mixes = [_dot(oa_ref[r, :], wa_ref[...]) + _dot(or_ref[r, :], wr_ref[...]) for r in rows]
    for r, mix in zip(rows, mixes):
        o_ref[r, :] = _layer_norm(alpha * x_ref[r, :] + mix, g_ref[...], b_ref[...])


def out_ln(x, oa, orw, w, g, b, *, alpha, tm=512):
    T, D = x.shape
    tm = min(tm, T)
    ka, kr = oa.shape[1], orw.shape[1]
    assert ka % kr == 0 and w.shape[0] == ka + kr
    return pl.pallas_call(
        functools.partial(_out_ln_kernel, alpha=alpha),
        grid=(T // tm,),
        in_specs=[
            pl.BlockSpec((tm, D), lambda i: (i, 0)),
            pl.BlockSpec((tm, ka), lambda i: (i, 0)),
            pl.BlockSpec((tm, kr), lambda i: (i, 0)),
            pl.BlockSpec((ka, D), lambda i: (0, 0)),
            pl.BlockSpec((kr, D), lambda i: (ka // kr, 0)),
            pl.BlockSpec((1, D), lambda i: (0, 0)),
            pl.BlockSpec((1, D), lambda i: (0, 0)),
        ],
        out_specs=pl.BlockSpec((tm, D), lambda i: (i, 0)),
        out_shape=jax.ShapeDtypeStruct((T, D), F32),
        compiler_params=_cparams(("parallel",)),
    )(x, oa, orw, w, w, g, b)


def _layer(x, l, ffn1_w_gate, ffn1_w_up, ffn1_w_down, ln1_g, ln1_b, w_in,
           lambda_q1, lambda_k1, lambda_q2, lambda_k2, attn_norm_g,
           rwkv_mu, rwkv_w0, rwkv_w2, rwkv_a0, rwkv_a2, rwkv_g2,
           rwkv_k_k, rwkv_k_a, rwkv_r_k, rwkv_gn_g, rwkv_gn_b,
           w_out, ln2_g, ln2_b, ffn2_w_gate, ffn2_w_up, ffn2_w_down, ln3_g, ln3_b):
    B, S, D = x.shape
    T = B * S
    alpha = (2.0 * DEPTH) ** 0.25
    lambda_init = 0.8 - 0.6 * math.exp(-0.3 * l)
    rw = rwkv_w0.shape[-1]
    aw = w_out.shape[1] - rw
    H = aw // LANES
    n_wd, n_ad, n_gd = rwkv_w2.shape[1], rwkv_a2.shape[1], rwkv_g2.shape[1]
    assert n_wd == HEAD and n_ad == HEAD and n_gd <= 2 * LANES
    row = lambda a: a.reshape(1, -1)
    bf = lambda a: a.astype(BF16)

    x1 = ffn_ln(x.reshape(T, D), ffn1_w_gate[l], ffn1_w_up[l], ffn1_w_down[l],
                row(ln1_g[l]), row(ln1_b[l]), alpha=alpha)

    wi = jnp.swapaxes(w_in[l], 0, 1)
    p_attn = proj_attn(x1, wi, n_cols=3 * aw, n_scaled_cols=aw, scale=HEAD ** -0.5 * math.log2(math.e))
    tab, feat = alibi_tables(H, S)
    o_attn = diff_attn(p_attn, tab, feat, row(lambda_q1[l]), row(lambda_k1[l]), row(lambda_q2[l]),
                       row(lambda_k2[l]), row(attn_norm_g[l]), B=B, S=S, H=H, lambda_init=lambda_init)

    mu = jnp.pad(rwkv_mu[l], (0, RWKV_TAIL - (n_wd + n_ad + n_gd)))
    w12 = jnp.zeros((LANES, 2 * rw), F32)
    w12 = w12.at[:HEAD, :rw].set(rwkv_w2[l]).at[HEAD:, rw:].set(rwkv_a2[l])
    g2p = jnp.pad(rwkv_g2[l], ((0, 2 * LANES - n_gd), (0, 0)))
    seqs = proj_rwkv(x1, wi, row(mu), row(rwkv_w0[l]), row(rwkv_a0[l]), row(rwkv_k_k[l]), row(rwkv_k_a[l]),
                     bf(w12), bf(g2p), S=S, col0=3 * aw, width=rw)
    r, lw, k2, v, kk, b, g = [t.reshape(B, S, rw) for t in seqs]
    o_rwkv, later_w = rwkv_chunk(r, lw, k2, v, kk, b, g, row(rwkv_r_k[l]), row(rwkv_gn_g[l]), row(rwkv_gn_b[l]),
                                 ride_along=(w_out[l], ffn2_w_gate[l], ffn2_w_up[l], ffn2_w_down[l]))

    x2 = out_ln(x1, o_attn, o_rwkv.reshape(T, rw), later_w[0], row(ln2_g[l]), row(ln2_b[l]), alpha=alpha)
    x3 = ffn_ln(x2, *later_w[1:], row(ln3_g[l]), row(ln3_b[l]), alpha=alpha, tf=2 * FFN_TF)
    return x3.reshape(B, S, D)


def kernel(x, ffn1_w_gate, ffn1_w_up, ffn1_w_down, ln1_g, ln1_b, w_in, lambda_q1, lambda_k1, lambda_q2, lambda_k2, attn_norm_g, rwkv_mu, rwkv_w0, rwkv_w2, rwkv_a0, rwkv_a2, rwkv_g2, rwkv_k_k, rwkv_k_a, rwkv_r_k, rwkv_gn_g, rwkv_gn_b, w_out, ln2_g, ln2_b, ffn2_w_gate, ffn2_w_up, ffn2_w_down, ln3_g, ln3_b):
    for l in range(DEPTH):
        x = _layer(x, l, ffn1_w_gate, ffn1_w_up, ffn1_w_down, ln1_g, ln1_b, w_in,
                   lambda_q1, lambda_k1, lambda_q2, lambda_k2, attn_norm_g,
                   rwkv_mu, rwkv_w0, rwkv_w2, rwkv_a0, rwkv_a2, rwkv_g2,
                   rwkv_k_k, rwkv_k_a, rwkv_r_k, rwkv_gn_g, rwkv_gn_b,
                   w_out, ln2_g, ln2_b, ffn2_w_gate, ffn2_w_up, ffn2_w_down, ln3_g, ln3_b)
    return x
```
